```python
import math
import jax, jax.numpy as jnp
from jax import lax
import numpy as np

D_MODEL = 2048
BATCH = 16
SEQ = 256
DEPTH = 2
DEC_BATCH = 8
DEC_SEQ = 1024
PAST_LEN = 256

GRID_W = 64
MIX_W = D_MODEL
GROUP_W = MIX_W // 4
CONV_W = GROUP_W
CONV_K = 31
SSM_W = GROUP_W
SSM_HEAD_DIM = 64
SSM_HEADS = SSM_W // SSM_HEAD_DIM
SSM_GROUPS = 2
SSM_D_STATE = 128
SSM_CONV_K = 5
SSM_BC = SSM_GROUPS * SSM_D_STATE
SSM_CONV_DIM = SSM_W + 2 * SSM_BC
RET_W = GROUP_W
RET_HEAD_DIM = 128
RET_HEADS = RET_W // RET_HEAD_DIM
ATT_HEAD_DIM = 64
ATT_HEADS = GROUP_W // ATT_HEAD_DIM
ATT_KV_HEADS = 2
ATT_WINDOW = 128
ATT_BLOCK = 128
ROPE_THETA = 10000.0
CHUNK = 128
D_FF = 4 * D_MODEL
EPS = 1e-6
IN_SIZES = (2 * CONV_W, SSM_W, SSM_CONV_DIM, 2 * SSM_HEADS, RET_W, RET_W, RET_W, RET_W,
            ATT_HEADS * ATT_HEAD_DIM, ATT_KV_HEADS * ATT_HEAD_DIM, ATT_KV_HEADS * ATT_HEAD_DIM)
N_IN = 2 * CONV_W + SSM_W + SSM_CONV_DIM + 2 * SSM_HEADS + 4 * RET_W + ATT_HEADS * ATT_HEAD_DIM + 2 * ATT_KV_HEADS * ATT_HEAD_DIM

kernel_name = 'hybrid_flow_backbone_step'


def rmsnorm(x, g):
    x32 = x.astype(jnp.float32)
    y = x32 * lax.rsqrt(jnp.mean(x32 * x32, axis=-1, keepdims=True) + EPS)
    return (y * g.astype(jnp.float32)).astype(x.dtype)


def layernorm(x, g, b):
    x32 = x.astype(jnp.float32)
    xc = x32 - jnp.mean(x32, axis=-1, keepdims=True)
    var = jnp.mean(xc * xc, axis=-1, keepdims=True)
    return (xc * lax.rsqrt(var + EPS) * g.astype(jnp.float32) + b.astype(jnp.float32)).astype(x.dtype)


def dwconv(x, w, b):
    pad = w.shape[0] // 2
    y = lax.conv_general_dilated(x, w[:, None, :].astype(x.dtype), (1,), [(pad, pad)],
                                 dimension_numbers=('NWC', 'WIO', 'NWC'),
                                 feature_group_count=x.shape[-1])
    return y + b.astype(x.dtype)


def conv_module(u, w_dw, b_dw, ln_g, ln_b):
    a, g = jnp.split(u, 2, axis=-1)
    h = dwconv(a * jax.nn.sigmoid(g), w_dw, b_dw)
    return jax.nn.silu(layernorm(h, ln_g, ln_b))


def chunked_recurrence(q, k, v, log_a, h0):
    bsz, L, H, N = q.shape
    P = v.shape[-1]
    nc = L // CHUNK
    f32 = jnp.float32
    qc = q.astype(f32).reshape(bsz, nc, CHUNK, H, N)
    kc = k.astype(f32).reshape(bsz, nc, CHUNK, H, N)
    vc = v.astype(f32).reshape(bsz, nc, CHUNK, H, P)
    a_cum = jnp.cumsum(log_a.astype(f32).reshape(bsz, nc, CHUNK, H), axis=2)
    a_t = jnp.moveaxis(a_cum, 3, 2)
    causal = jnp.tril(jnp.ones((CHUNK, CHUNK), dtype=bool))
    decay = jnp.exp(jnp.where(causal, a_t[..., :, None] - a_t[..., None, :], -jnp.inf))
    scores = jnp.einsum('bcqhn,bckhn->bchqk', qc, kc) * decay
    y_intra = jnp.einsum('bchqk,bckhp->bcqhp', scores, vc)
    a_last = a_cum[:, :, -1, :]
    w_end = jnp.exp(a_last[:, :, None, :] - a_cum)
    chunk_states = jnp.einsum('bckhn,bckh,bckhp->bchpn', kc, w_end, vc)

    def step(h, xs):
        s_c, a_c = xs
        return h * jnp.exp(a_c)[..., None, None] + s_c, h

    h_final, h_prev = lax.scan(step, h0.astype(f32),
                               (jnp.moveaxis(chunk_states, 1, 0), jnp.moveaxis(a_last, 1, 0)))
    h_prev = jnp.moveaxis(h_prev, 0, 1)
    y_inter = jnp.einsum('bcqhn,bchpn,bcqh->bcqhp', qc, h_prev, jnp.exp(a_cum))
    y = (y_intra + y_inter).reshape(bsz, L, H, P)
    return y.astype(v.dtype), h_final


def directional(q, k, v, log_a, h0, reverse):
    if reverse:
        q, k, v, log_a = (jnp.flip(t, axis=1) for t in (q, k, v, log_a))
    y, h = chunked_recurrence(q, k, v, log_a, h0)
    if reverse:
        y = jnp.flip(y, axis=1)
    return y, h


def ssm_mixer(z, xbc, dt_raw, conv_w, conv_b, a_log, dt_bias, d_skip, norm_g, h0):
    bsz, L, _ = z.shape
    xbc = jax.nn.silu(dwconv(xbc, conv_w, conv_b))
    xs, bm, cm = jnp.split(xbc, [SSM_W, SSM_W + SSM_BC], axis=-1)
    xs = xs.reshape(bsz, L, SSM_HEADS, SSM_HEAD_DIM)
    rep = SSM_HEADS // SSM_GROUPS
    bm = jnp.repeat(bm.reshape(bsz, L, SSM_GROUPS, SSM_D_STATE), rep, axis=2)
    cm = jnp.repeat(cm.reshape(bsz, L, SSM_GROUPS, SSM_D_STATE), rep, axis=2)
    dt = jax.nn.softplus(dt_raw.astype(jnp.float32).reshape(bsz, L, 2, SSM_HEADS)
                         + dt_bias.astype(jnp.float32))
    a_neg = -jnp.exp(a_log.astype(jnp.float32))
    y = 0.0
    finals = []
    for d in range(2):
        dt_d = dt[:, :, d]
        y_d, h_d = directional(cm, bm * dt_d[..., None], xs, dt_d * a_neg[d], h0[:, d], d == 1)
        y = y + y_d + d_skip[d][:, None] * xs
        finals.append(h_d)
    y = y.reshape(bsz, L, SSM_W)
    out = rmsnorm(y * jax.nn.silu(z), norm_g)
    return out.astype(z.dtype), jnp.stack(finals, axis=1).astype(z.dtype)


def retention_mixer(q, k, v, g, log_decay, gn_g, h0):
    bsz, L, _ = q.shape
    shp = (bsz, L, RET_HEADS, RET_HEAD_DIM)
    qh = q.reshape(shp)
    kh = k.reshape(shp) * (RET_HEAD_DIM ** -0.5)
    vh = v.reshape(shp)
    y = 0.0
    finals = []
    for d in range(2):
        log_a = jnp.broadcast_to(log_decay[d].astype(jnp.float32), (bsz, L, RET_HEADS))
        y_d, h_d = directional(qh, kh, vh, log_a, h0[:, d], d == 1)
        y = y + y_d
        finals.append(h_d)
    y32 = y.astype(jnp.float32)
    yc = y32 - jnp.mean(y32, axis=-1, keepdims=True)
    var = jnp.mean(yc * yc, axis=-1, keepdims=True)
    yn = (yc * lax.rsqrt(var + EPS)).reshape(bsz, L, RET_W) * gn_g.astype(jnp.float32)
    out = jax.nn.silu(g.astype(jnp.float32)) * yn
    return out.astype(q.dtype), jnp.stack(finals, axis=1).astype(q.dtype)


def rope_axis(x, pos):
    half = x.shape[-1] // 2
    freqs = ROPE_THETA ** (-jnp.arange(half, dtype=jnp.float32) / half)
    ang = pos.astype(jnp.float32)[:, None] * freqs
    cos = jnp.cos(ang)[:, None, :]
    sin = jnp.sin(ang)[:, None, :]
    x32 = x.astype(jnp.float32)
    x1, x2 = x32[..., :half], x32[..., half:]
    return jnp.concatenate([x1 * cos - x2 * sin, x2 * cos + x1 * sin], axis=-1).astype(x.dtype)


def rope_2d(x):
    L = x.shape[1]
    n_rows = L // GRID_W
    rows = jnp.repeat(jnp.arange(n_rows), GRID_W)
    cols = jnp.tile(jnp.arange(GRID_W), n_rows)
    d = x.shape[-1] // 2
    return jnp.concatenate([rope_axis(x[..., :d], rows), rope_axis(x[..., d:], cols)], axis=-1)


def ctx_attention(q, k, v, sink):
    bsz, Lc, H, d = q.shape
    G = H // ATT_KV_HEADS
    nb = Lc // ATT_BLOCK
    scale = d ** -0.5
    qb = jnp.moveaxis(q.reshape(bsz, nb, ATT_BLOCK, ATT_KV_HEADS, G, d), 1, 0)
    k32 = k.astype(jnp.float32)
    v32 = v.astype(jnp.float32)
    sink_l = sink.astype(jnp.float32).reshape(1, ATT_KV_HEADS, G, 1, 1)

    def one_block(qi):
        s = jnp.einsum('bqkgd,bskd->bkgqs', qi.astype(jnp.float32), k32) * scale
        s_all = jnp.concatenate([s, jnp.broadcast_to(sink_l, s.shape[:-1] + (1,))], axis=-1)
        p = jax.nn.softmax(s_all, axis=-1)[..., :-1]
        return jnp.einsum('bkgqs,bskd->bqkgd', p, v32)

    o = lax.map(one_block, qb)
    return jnp.moveaxis(o, 0, 1).reshape(bsz, Lc, H * d).astype(q.dtype)


def latent_attention(q, k, v, k_ctx, v_ctx, sink):
    bsz, L, H, d = q.shape
    G = H // ATT_KV_HEADS
    nb = L // ATT_BLOCK
    Lc = k_ctx.shape[1]
    scale = d ** -0.5
    f32 = jnp.float32
    qb = q.astype(f32).reshape(bsz, nb, ATT_BLOCK, ATT_KV_HEADS, G, d)
    pad = ((0, 0), (ATT_BLOCK, ATT_BLOCK), (0, 0), (0, 0))

    def band(t):
        tp = jnp.pad(t.astype(f32), pad).reshape(bsz, nb + 2, ATT_BLOCK, ATT_KV_HEADS, d)
        return jnp.concatenate([tp[:, :-2], tp[:, 1:-1], tp[:, 2:]], axis=2)

    kb, vb = band(k), band(v)
    q_pos = jnp.arange(nb)[:, None] * ATT_BLOCK + jnp.arange(ATT_BLOCK)[None, :]
    k_pos = (jnp.arange(nb)[:, None] - 1) * ATT_BLOCK + jnp.arange(3 * ATT_BLOCK)[None, :]
    kp = k_pos[:, None, :]
    mask = (jnp.abs(q_pos[:, :, None] - kp) <= ATT_WINDOW) & (kp >= 0) & (kp < L)
    s_band = jnp.einsum('bnqkgd,bnskd->bnkgqs', qb, kb) * scale
    s_band = jnp.where(mask[None, :, None, None], s_band, -jnp.inf)
    s_ctx = jnp.einsum('bnqkgd,bskd->bnkgqs', qb, k_ctx.astype(f32)) * scale
    s_sink = jnp.broadcast_to(sink.astype(f32).reshape(1, 1, ATT_KV_HEADS, G, 1, 1),
                              s_band.shape[:-1] + (1,))
    p = jax.nn.softmax(jnp.concatenate([s_ctx, s_band, s_sink], axis=-1), axis=-1)
    o = (jnp.einsum('bnkgqs,bskd->bnqkgd', p[..., :Lc], v_ctx.astype(f32))
         + jnp.einsum('bnkgqs,bnskd->bnqkgd', p[..., Lc:Lc + 3 * ATT_BLOCK], vb))
    return o.reshape(bsz, L, H * d).astype(q.dtype)


def mixer(h, p, ctx):
    bsz, L, _ = h.shape
    u = h @ p['w_in']
    splits = [int(s) for s in np.cumsum(IN_SIZES)[:-1]]
    u_conv, z, xbc, dt_raw, rq, rk, rv, rg, aq, ak, av = jnp.split(u, splits, axis=-1)
    o_conv = conv_module(u_conv, p['conv_w'], p['conv_b'], p['conv_ln_g'], p['conv_ln_b'])
    if ctx is None:
        h_ssm0 = jnp.zeros((bsz, 2, SSM_HEADS, SSM_HEAD_DIM, SSM_D_STATE), jnp.float32)
        h_ret0 = jnp.zeros((bsz, 2, RET_HEADS, RET_HEAD_DIM, RET_HEAD_DIM), jnp.float32)
    else:
        k_c, v_c, h_ssm0, h_ret0 = ctx
    o_ssm, h_ssm = ssm_mixer(z, xbc, dt_raw, p['ssm_conv_w'], p['ssm_conv_b'], p['ssm_a_log'],
                             p['ssm_dt_bias'], p['ssm_d'], p['ssm_norm'], h_ssm0)
    o_ret, h_ret = retention_mixer(rq, rk, rv, rg, p['ret_log_decay'], p['ret_gn_g'], h_ret0)
    aq = aq.reshape(bsz, L, ATT_HEADS, ATT_HEAD_DIM)
    ak = ak.reshape(bsz, L, ATT_KV_HEADS, ATT_HEAD_DIM)
    av = av.reshape(bsz, L, ATT_KV_HEADS, ATT_HEAD_DIM)
    if ctx is None:
        o_att = ctx_attention(aq, ak, av, p['att_sink'])
    else:
        o_att = latent_attention(rope_2d(aq), rope_2d(ak), av, k_c, v_c, p['att_sink'])
    out = jnp.concatenate([o_conv, o_ssm, o_ret, o_att], axis=-1) @ p['w_out']
    return out, (ak, av, h_ssm, h_ret)


def block(x, c_vec, p, ctx):
    mod = jax.nn.silu(c_vec) @ p['ada_w'] + p['ada_b']
    sh1, sc1, g1, sh2, sc2, g2 = jnp.split(mod[:, None, :], 6, axis=-1)
    h = rmsnorm(x, p['norm_mix']) * (1.0 + sc1) + sh1
    mix_out, states = mixer(h, p, ctx)
    x = x + g1 * mix_out
    h = rmsnorm(x, p['norm_mlp']) * (1.0 + sc2) + sh2
    x = x + g2 * (jnp.square(jax.nn.relu(h @ p['w1'])) @ p['w2'])
    return x, states


def setup_inputs(seed: int = 0) -> dict:
    key = jax.random.key(seed)
    ks = jax.random.split(key, 32)
    f32 = jnp.float32

    def nrm(k, shape, s):
        return s * jax.random.normal(k, shape, f32)

    D = D_MODEL
    dt_init = jnp.exp(jax.random.uniform(ks[20], (DEPTH, 2, SSM_HEADS), f32, math.log(1e-3), math.log(1e-1)))
    ret_base = jnp.log1p(-jnp.exp2(-5.0 - jnp.arange(RET_HEADS, dtype=f32)))
    return {
        'x_prompt': nrm(ks[0], (BATCH, SEQ, D), 1.0),
        'x_sample': nrm(ks[1], (DEC_BATCH, DEC_SEQ, D), 1.0),
        'cache_attn_k': nrm(ks[2], (DEC_BATCH, DEPTH, PAST_LEN, ATT_KV_HEADS, ATT_HEAD_DIM), 1.0),
        'cache_attn_v': nrm(ks[3], (DEC_BATCH, DEPTH, PAST_LEN, ATT_KV_HEADS, ATT_HEAD_DIM), 1.0),
        'state_ssm': nrm(ks[4], (DEC_BATCH, DEPTH, 2, SSM_HEADS, SSM_HEAD_DIM, SSM_D_STATE), 0.5),
        'state_ret': nrm(ks[5], (DEC_BATCH, DEPTH, 2, RET_HEADS, RET_HEAD_DIM, RET_HEAD_DIM), 1.0),
        'c': nrm(ks[6], (DEC_BATCH, D), 1.0),
        'c_ctx': nrm(ks[7], (D,), 1.0),
        'ada_w': nrm(ks[8], (DEPTH, D, 6 * D), 0.5 * D ** -0.5),
        'ada_b': nrm(ks[9], (DEPTH, 6 * D), 0.01),
        'norm_mix': 1.0 + nrm(ks[10], (DEPTH, D), 0.02),
        'norm_mlp': 1.0 + nrm(ks[11], (DEPTH, D), 0.02),
        'w_in': nrm(ks[12], (DEPTH, D, N_IN), D ** -0.5),
        'conv_w': nrm(ks[13], (DEPTH, CONV_K, CONV_W), CONV_K ** -0.5),
        'conv_b': nrm(ks[14], (DEPTH, CONV_W), 0.01),
        'conv_ln_g': 1.0 + nrm(ks[15], (DEPTH, CONV_W), 0.02),
        'conv_ln_b': nrm(ks[16], (DEPTH, CONV_W), 0.01),
        'ssm_conv_w': nrm(ks[17], (DEPTH, SSM_CONV_K, SSM_CONV_DIM), SSM_CONV_K ** -0.5),
        'ssm_conv_b': nrm(ks[18], (DEPTH, SSM_CONV_DIM), 0.01),
        'ssm_a_log': jnp.log(jax.random.uniform(ks[19], (DEPTH, 2, SSM_HEADS), f32, 1.0, 16.0)),
        'ssm_dt_bias': dt_init + jnp.log(-jnp.expm1(-dt_init)),
        'ssm_d': 1.0 + nrm(ks[21], (DEPTH, 2, SSM_HEADS), 0.1),
        'ssm_norm': 1.0 + nrm(ks[22], (DEPTH, SSM_W), 0.02),
        'ret_log_decay': ret_base * (1.0 + nrm(ks[23], (DEPTH, 2, RET_HEADS), 0.05)),
        'ret_gn_g': 1.0 + nrm(ks[24], (DEPTH, RET_W), 0.02),
        'att_sink': nrm(ks[25], (DEPTH, ATT_HEADS), 0.5),
        'w_out': nrm(ks[26], (DEPTH, MIX_W, D), MIX_W ** -0.5),
        'w1': nrm(ks[27], (DEPTH, D, D_FF), D ** -0.5),
        'w2': nrm(ks[28], (DEPTH, D_FF, D), D_FF ** -0.5),
        'final_norm': 1.0 + nrm(ks[29], (D,), 0.02),
    }


def reference(x_prompt, x_sample, cache_attn_k, cache_attn_v, state_ssm, state_ret, c, c_ctx,
              ada_w, ada_b, norm_mix, norm_mlp, w_in, conv_w, conv_b, conv_ln_g, conv_ln_b,
              ssm_conv_w, ssm_conv_b, ssm_a_log, ssm_dt_bias, ssm_d, ssm_norm,
              ret_log_decay, ret_gn_g, att_sink, w_out, w1, w2, final_norm):
    y_p = x_prompt
    y_s = x_sample
    new_k, new_v, new_ssm, new_ret = [], [], [], []
    for l in range(DEPTH):
        p = dict(ada_w=ada_w[l], ada_b=ada_b[l], norm_mix=norm_mix[l], norm_mlp=norm_mlp[l],
                 w_in=w_in[l], conv_w=conv_w[l], conv_b=conv_b[l], conv_ln_g=conv_ln_g[l],
                 conv_ln_b=conv_ln_b[l], ssm_conv_w=ssm_conv_w[l], ssm_conv_b=ssm_conv_b[l],
                 ssm_a_log=ssm_a_log[l], ssm_dt_bias=ssm_dt_bias[l], ssm_d=ssm_d[l],
                 ssm_norm=ssm_norm[l], ret_log_decay=ret_log_decay[l], ret_gn_g=ret_gn_g[l],
                 att_sink=att_sink[l], w_out=w_out[l], w1=w1[l], w2=w2[l])
        y_p, (k_l, v_l, hs_l, hr_l) = block(y_p, c_ctx[None, :], p, None)
        new_k.append(k_l)
        new_v.append(v_l)
        new_ssm.append(hs_l)
        new_ret.append(hr_l)
        y_s, _ = block(y_s, c, p, (cache_attn_k[:, l], cache_attn_v[:, l], state_ssm[:, l], state_ret[:, l]))
    y_prompt = rmsnorm(y_p, final_norm)
    y_sample = rmsnorm(y_s, final_norm)
    return (y_prompt, y_sample, jnp.stack(new_k, axis=1), jnp.stack(new_v, axis=1),
            jnp.stack(new_ssm, axis=1), jnp.stack(new_ret, axis=1))
```

```python
import functools

import jax
import jax.numpy as jnp
from jax import lax
from jax.experimental import pallas as pl
from jax.experimental.pallas import tpu as pltpu

f32 = jnp.float32
bf16 = jnp.bfloat16

D = 2048
DEPTH = 2
GW = 512
CONV_K = 31
SSM_HEADS, SSM_P, SSM_N, SSM_GROUPS, SSM_K = 8, 64, 128, 2, 5
RET_HEADS, RET_D = 4, 128
ATT_HEADS, ATT_KV, ATT_D, ATT_WIN = 8, 2, 64, 128
GRID_W = 64
ROPE_THETA = 10000.0
CHUNK = 128
D_FF = 4 * D
EPS = 1e-6

C_CONV, C_XBC, C_Z, C_RQ, C_RK, C_RV, C_RG, C_AQ, C_AK, C_AV, C_DT = (
    0, 1024, 2048, 2560, 3072, 3584, 4096, 4608, 5120, 5248, 5376)
UW = 5504
N_CHUNK = 512

VMEM_LIMIT = 56 * 1024 * 1024


def _cparams(sem):
    return pltpu.CompilerParams(dimension_semantics=sem, vmem_limit_bytes=VMEM_LIMIT)


def _sigmoid(x):
    return 1.0 / (1.0 + jnp.exp(-x))


def _silu(x):
    return x * _sigmoid(x)


def _nt(a, b):
    return lax.dot_general(a, b, (((1,), (1,)), ((), ())), preferred_element_type=f32)


def _tn(a, b):
    return lax.dot_general(a, b, (((0,), (0,)), ((), ())), preferred_element_type=f32)


def _nn(a, b):
    return jnp.dot(a, b, preferred_element_type=f32)


def _split3(x):
    hi = x.astype(bf16)
    r1 = x - hi.astype(f32)
    mid = r1.astype(bf16)
    lo = (r1 - mid.astype(f32)).astype(bf16)
    return hi, mid, lo


def _exact_dot_right(x, m01):
    hi, mid, lo = _split3(x)
    return _nn(hi, m01) + _nn(mid, m01) + _nn(lo, m01)


def _exact_dot_left(m01, x):
    hi, mid, lo = _split3(x)
    return _nn(m01, hi) + _nn(m01, mid) + _nn(m01, lo)


def _ada_kernel(c_ref, w_ref, b_ref, o_ref):
    c = c_ref[...]
    s = _silu(c).astype(bf16)
    o_ref[...] = _nn(s, w_ref[...].astype(bf16)) + b_ref[...]


def _ada_mod(cvec, ada_w, ada_b):
    tn = 1024
    return pl.pallas_call(
        _ada_kernel,
        grid=(DEPTH, 6 * D // tn),
        in_specs=[pl.BlockSpec((16, D), lambda l, j: (0, 0)),
                  pl.BlockSpec((None, D, tn), lambda l, j: (l, 0, j)),
                  pl.BlockSpec((None, 1, tn), lambda l, j: (l, 0, j))],
        out_specs=pl.BlockSpec((None, 16, tn), lambda l, j: (l, 0, j)),
        out_shape=jax.ShapeDtypeStruct((DEPTH, 16, 6 * D), f32),
        compiler_params=_cparams(("arbitrary", "arbitrary")),
        name="ada_mod",
    )(cvec, ada_w, ada_b.reshape(DEPTH, 1, 6 * D))


def _mod_spec(idx, tm, L, per_seq):
    if per_seq:
        return pl.BlockSpec((None, None, 1, D), lambda i, *_: ((i * tm) // L, idx, 0, 0))
    return pl.BlockSpec((None, None, 1, D), lambda i, *_: (0, idx, 0, 0))


def _inproj_kernel(x_ref, g_ref, sc_ref, sh_ref, w_ref, u_ref):
    x = x_ref[...]
    ms = jnp.mean(x * x, axis=-1, keepdims=True)
    h = x * lax.rsqrt(ms + EPS) * g_ref[...] * (1.0 + sc_ref[...]) + sh_ref[...]
    hb = h.astype(bf16)
    for n0 in range(0, UW, N_CHUNK):
        n1 = min(n0 + N_CHUNK, UW)
        u_ref[:, n0:n1] = _nn(hb, w_ref[:, n0:n1])


def _inproj(x, norm_g, mod, w_in_p, L, per_seq):
    T = x.shape[0]
    tm = 256
    return pl.pallas_call(
        _inproj_kernel,
        grid=(T // tm,),
        in_specs=[pl.BlockSpec((tm, D), lambda i: (i, 0)),
                  pl.BlockSpec((1, D), lambda i: (0, 0)),
                  _mod_spec(1, tm, L, per_seq),
                  _mod_spec(0, tm, L, per_seq),
                  pl.BlockSpec((D, UW), lambda i: (0, 0), pipeline_mode=pl.Buffered(1))],
        out_specs=pl.BlockSpec((tm, UW), lambda i: (i, 0)),
        out_shape=jax.ShapeDtypeStruct((T, UW), f32),
        compiler_params=_cparams(("arbitrary",)),
        name="in_proj",
    )(x, norm_g.reshape(1, D), mod, mod, w_in_p)


def _outproj_kernel(oc_ref, os_ref, or_ref, oa_ref, w_ref, x_ref, g_ref, o_ref):
    for n0 in range(0, D, N_CHUNK):
        n1 = n0 + N_CHUNK
        acc = _nn(oc_ref[...], w_ref[0:GW, n0:n1])
        acc = acc + _nn(os_ref[...], w_ref[GW:2 * GW, n0:n1])
        acc = acc + _nn(or_ref[...], w_ref[2 * GW:3 * GW, n0:n1])
        acc = acc + _nn(oa_ref[...], w_ref[3 * GW:4 * GW, n0:n1])
        o_ref[:, n0:n1] = x_ref[:, n0:n1] + g_ref[:, n0:n1] * acc


def _outproj(o_conv, o_ssm, o_ret, o_att, w_out_b, x, mod, L, per_seq):
    T = x.shape[0]
    tm = 512
    ospec = pl.BlockSpec((tm, GW), lambda i: (i, 0))
    return pl.pallas_call(
        _outproj_kernel,
        grid=(T // tm,),
        in_specs=[ospec, ospec, ospec, ospec,
                  pl.BlockSpec((D, D), lambda i: (0, 0), pipeline_mode=pl.Buffered(1)),
                  pl.BlockSpec((tm, D), lambda i: (i, 0)),
                  _mod_spec(2, tm, L, per_seq)],
        out_specs=pl.BlockSpec((tm, D), lambda i: (i, 0)),
        out_shape=jax.ShapeDtypeStruct((T, D), f32),
        compiler_params=_cparams(("arbitrary",)),
        name="out_proj",
    )(o_conv, o_ssm, o_ret, o_att, w_out_b, x, mod)


def _mlp_kernel(x_ref, g_ref, sc_ref, sh_ref, g2_ref, fn_ref, w1_ref, w2_ref, o_ref,
                h_scr, acc_scr, *, final):
    f = pl.program_id(1)

    @pl.when(f == 0)
    def _():
        x = x_ref[...]
        ms = jnp.mean(x * x, axis=-1, keepdims=True)
        h = x * lax.rsqrt(ms + EPS) * g_ref[...] * (1.0 + sc_ref[...]) + sh_ref[...]
        h_scr[...] = h.astype(bf16)
        acc_scr[...] = jnp.zeros_like(acc_scr)

    a = _nn(h_scr[...], w1_ref[...])
    a = jnp.maximum(a, 0.0)
    a = (a * a).astype(bf16)
    acc_scr[...] += _nn(a, w2_ref[...])

    @pl.when(f == pl.num_programs(1) - 1)
    def _():
        y = x_ref[...] + g2_ref[...] * acc_scr[...]
        if final:
            ms = jnp.mean(y * y, axis=-1, keepdims=True)
            y = y * lax.rsqrt(ms + EPS) * fn_ref[...]
        o_ref[...] = y


def _mlp(x, norm_g, mod, final_norm, w1_b, w2_b, L, per_seq, final):
    T = x.shape[0]
    tm, tf = 512, 512
    return pl.pallas_call(
        functools.partial(_mlp_kernel, final=final),
        grid=(T // tm, D_FF // tf),
        in_specs=[pl.BlockSpec((tm, D), lambda i, f: (i, 0)),
                  pl.BlockSpec((1, D), lambda i, f: (0, 0)),
                  _mod_spec(4, tm, L, per_seq),
                  _mod_spec(3, tm, L, per_seq),
                  _mod_spec(5, tm, L, per_seq),
                  pl.BlockSpec((1, D), lambda i, f: (0, 0)),
                  pl.BlockSpec((D, tf), lambda i, f: (0, f)),
                  pl.BlockSpec((tf, D), lambda i, f: (f, 0))],
        out_specs=pl.BlockSpec((tm, D), lambda i, f: (i, 0)),
        out_shape=jax.ShapeDtypeStruct((T, D), f32),
        scratch_shapes=[pltpu.VMEM((tm, D), bf16), pltpu.VMEM((tm, D), f32)],
        compiler_params=_cparams(("arbitrary", "arbitrary")),
        name="mlp",
    )(x, norm_g.reshape(1, D), mod, mod, mod, final_norm.reshape(1, D), w1_b, w2_b)


def _conv_kernel(u_ref, w_ref, b_ref, lg_ref, lb_ref, o_ref, vpad_ref, *, L):
    RC = 64
    pad = 16
    vpad_ref[0:pad, :] = jnp.zeros((pad, GW), f32)
    vpad_ref[pad + L:2 * pad + L, :] = jnp.zeros((pad, GW), f32)

    def fill(i, carry):
        r0 = pl.multiple_of(i * RC, RC)
        a = u_ref[pl.ds(r0, RC), 0:GW]
        g = u_ref[pl.ds(r0, RC), GW:2 * GW]
        vpad_ref[pl.ds(pad + r0, RC), :] = a * _sigmoid(g)
        return carry

    lax.fori_loop(0, L // RC, fill, 0)

    def body(i, carry):
        r0 = pl.multiple_of(i * RC, RC)
        acc = jnp.broadcast_to(b_ref[...], (RC, GW))
        win = vpad_ref.at[pl.ds(r0, RC + 2 * pad), :]
        off = pad - CONV_K // 2
        for k in range(CONV_K):
            acc = acc + win[off + k:off + k + RC, :] * w_ref[k:k + 1, :]
        mu = jnp.mean(acc, axis=-1, keepdims=True)
        xc = acc - mu
        var = jnp.mean(xc * xc, axis=-1, keepdims=True)
        y = xc * lax.rsqrt(var + EPS) * lg_ref[...] + lb_ref[...]
        o_ref[pl.ds(r0, RC), :] = _silu(y).astype(o_ref.dtype)
        return carry

    lax.fori_loop(0, L // RC, body, 0)


def _conv_module(u, conv_w, conv_b, ln_g, ln_b, nb, L):
    return pl.pallas_call(
        functools.partial(_conv_kernel, L=L),
        grid=(nb,),
        in_specs=[pl.BlockSpec((L, 2 * GW), lambda b: (b, C_CONV // (2 * GW))),
                  pl.BlockSpec((CONV_K, GW), lambda b: (0, 0)),
                  pl.BlockSpec((1, GW), lambda b: (0, 0)),
                  pl.BlockSpec((1, GW), lambda b: (0, 0)),
                  pl.BlockSpec((1, GW), lambda b: (0, 0))],
        out_specs=pl.BlockSpec((L, GW), lambda b: (b, 0)),
        out_shape=jax.ShapeDtypeStruct((nb * L, GW), bf16),
        scratch_shapes=[pltpu.VMEM((L + 32, GW), f32)],
        compiler_params=_cparams(("arbitrary",)),
        name="conv_module",
    )(u, conv_w, conv_b.reshape(1, GW), ln_g.reshape(1, GW), ln_b.reshape(1, GW))


def _ret_kernel(*refs, L, has_h0, want_state):
    ld_ref, q_ref, k_ref, v_ref, g_ref, gn_ref = refs[:6]
    pos = 6
    h0_ref = None
    if has_h0:
        h0_ref = refs[pos]
        pos += 1
    o_ref = refs[pos]
    pos += 1
    hout_ref = None
    if want_state:
        hout_ref = refs[pos]
        pos += 1
    y_scr, st_scr, dec_scr = refs[pos:pos + 3]

    nc = L // CHUNK
    row = lax.broadcasted_iota(jnp.int32, (CHUNK, CHUNK), 0)
    col = lax.broadcasted_iota(jnp.int32, (CHUNK, CHUNK), 1)
    rowf = row.astype(f32)
    diff = (row - col).astype(f32)
    kscale = RET_D ** -0.5

    for d in range(2):
        for h in range(RET_HEADS):
            if has_h0:
                st_scr[d, h] = h0_ref[d, h]
            else:
                st_scr[d, h] = jnp.zeros((RET_D, RET_D), f32)
    for h in range(RET_HEADS):
        laf = ld_ref[h]
        lab = ld_ref[RET_HEADS + h]
        dec_scr[h] = jnp.where(col < row, jnp.exp(diff * laf),
                               jnp.where(col > row, jnp.exp(-diff * lab), 2.0))

    def load(r0, h):
        sl = slice(h * RET_D, (h + 1) * RET_D)
        qb = q_ref[pl.ds(r0, CHUNK), sl].astype(bf16)
        kb = (k_ref[pl.ds(r0, CHUNK), sl] * kscale).astype(bf16)
        vh = v_ref[pl.ds(r0, CHUNK), sl]
        return sl, qb, kb, vh

    def fwd(c, carry):
        r0 = pl.multiple_of(c * CHUNK, CHUNK)
        for h in range(RET_HEADS):
            sl, qb, kb, vh = load(r0, h)
            laf = ld_ref[h]
            m = (_nt(qb, kb) * dec_scr[h]).astype(bf16)
            y = _nn(m, vh.astype(bf16))
            st = st_scr[0, h]
            y = y + jnp.exp((rowf + 1.0) * laf) * _nt(qb, st.astype(bf16))
            wend = jnp.exp((CHUNK - 1.0 - rowf) * laf)
            cs = _tn((vh * wend).astype(bf16), kb)
            st_scr[0, h] = st * jnp.exp(jnp.full((RET_D, RET_D), CHUNK * laf, f32)) + cs
            y_scr[pl.ds(r0, CHUNK), sl] = y
        return carry

    lax.fori_loop(0, nc, fwd, 0)

    def bwd(i, carry):
        r0 = pl.multiple_of((nc - 1 - i) * CHUNK, CHUNK)
        for h in range(RET_HEADS):
            sl, qb, kb, vh = load(r0, h)
            lab = ld_ref[RET_HEADS + h]
            st = st_scr[1, h]
            y = y_scr[pl.ds(r0, CHUNK), sl] + jnp.exp((CHUNK - rowf) * lab) * _nt(qb, st.astype(bf16))
            wend = jnp.exp(rowf * lab)
            cs = _tn((vh * wend).astype(bf16), kb)
            st_scr[1, h] = st * jnp.exp(jnp.full((RET_D, RET_D), CHUNK * lab, f32)) + cs
            mu = jnp.mean(y, axis=-1, keepdims=True)
            yc = y - mu
            var = jnp.mean(yc * yc, axis=-1, keepdims=True)
            yn = yc * lax.rsqrt(var + EPS) * gn_ref[:, sl]
            g = g_ref[pl.ds(r0, CHUNK), sl]
            o_ref[pl.ds(r0, CHUNK), sl] = (_silu(g) * yn).astype(o_ref.dtype)
        return carry

    lax.fori_loop(0, nc, bwd, 0)

    if want_state:
        for d in range(2):
            for h in range(RET_HEADS):
                hout_ref[d, h] = st_scr[d, h]


def _retention(u, log_decay, gn_g, h0, layer, nb, L, want_state):
    has_h0 = h0 is not None
    cblk = lambda c: pl.BlockSpec((L, GW), lambda b, c=c: (b, c // GW))
    in_specs = [pl.BlockSpec(memory_space=pltpu.SMEM),
                cblk(C_RQ), cblk(C_RK), cblk(C_RV), cblk(C_RG),
                pl.BlockSpec((1, GW), lambda b: (0, 0))]
    args = [log_decay.reshape(2 * RET_HEADS), u, u, u, u, gn_g.reshape(1, GW)]
    if has_h0:
        in_specs.append(pl.BlockSpec((None, None, 2, RET_HEADS, RET_D, RET_D),
                                     lambda b: (b, layer, 0, 0, 0, 0)))
        args.append(h0)
    out_specs = [pl.BlockSpec((L, GW), lambda b: (b, 0))]
    out_shape = [jax.ShapeDtypeStruct((nb * L, GW), bf16)]
    if want_state:
        out_specs.append(pl.BlockSpec((None, 2, RET_HEADS, RET_D, RET_D), lambda b: (b, 0, 0, 0, 0)))
        out_shape.append(jax.ShapeDtypeStruct((nb, 2, RET_HEADS, RET_D, RET_D), f32))
    res = pl.pallas_call(
        functools.partial(_ret_kernel, L=L, has_h0=has_h0, want_state=want_state),
        grid=(nb,),
        in_specs=in_specs,
        out_specs=out_specs,
        out_shape=out_shape,
        scratch_shapes=[pltpu.VMEM((L, GW), f32),
                        pltpu.VMEM((2, RET_HEADS, RET_D, RET_D), f32),
                        pltpu.VMEM((RET_HEADS, CHUNK, CHUNK), f32)],
        compiler_params=_cparams(("arbitrary",)),
        name="retention",
    )(*args)
    return res if want_state else (res[0], None)


def _softplus(x):
    return jnp.maximum(x, 0.0) + jnp.log(1.0 + jnp.exp(-jnp.abs(x)))


def _ssm_kernel(*refs, L, has_h0, want_state):
    (xbc_ref, z_ref, dtc_ref, dtr_ref, cw_ref, cb_ref, alr_ref, alc_ref, dbr_ref, dbc_ref,
     dsk_ref, ng_ref) = refs[:12]
    pos = 12
    h0_ref = None
    if has_h0:
        h0_ref = refs[pos]
        pos += 1
    o_ref = refs[pos]
    pos += 1
    hout_ref = None
    if want_state:
        hout_ref = refs[pos]
        pos += 1
    xpad_ref, xc_scr, y_scr, st_scr = refs[pos:pos + 4]

    nc = L // CHUNK
    H, P, N = SSM_HEADS, SSM_P, SSM_N
    pad = 8
    row = lax.broadcasted_iota(jnp.int32, (CHUNK, CHUNK), 0)
    col = lax.broadcasted_iota(jnp.int32, (CHUNK, CHUNK), 1)
    lt01 = (row >= col).astype(bf16)
    ut01 = (row <= col).astype(bf16)
    sub16 = lax.broadcasted_iota(jnp.int32, (2 * H, CHUNK), 0)

    xpad_ref[0:pad, :] = jnp.zeros((pad, 2 * GW), f32)
    xpad_ref[pad + L:2 * pad + L, :] = jnp.zeros((pad, 2 * GW), f32)

    def fill(i, carry):
        r0 = pl.multiple_of(i * CHUNK, CHUNK)
        xpad_ref[pl.ds(pad + r0, CHUNK), :] = xbc_ref[pl.ds(r0, CHUNK), :]
        return carry

    lax.fori_loop(0, nc, fill, 0)

    for d in range(2):
        for h in range(H):
            if has_h0:
                st_scr[d, h] = h0_ref[d, h]
            else:
                st_scr[d, h] = jnp.zeros((P, N), f32)

    a_neg_r = -jnp.exp(alr_ref[...])
    a_neg_c = -jnp.exp(alc_ref[...])

    def decays(r0):
        dt_c = _softplus(dtc_ref[pl.ds(r0, CHUNK), :] + dbr_ref[...])
        dt_r = _softplus(dtr_ref[:, pl.ds(r0, CHUNK)] + dbc_ref[...])
        la_c = dt_c * a_neg_r
        la_r = dt_r * a_neg_c
        a_c = jnp.where(col < H, _exact_dot_left(lt01, la_c), _exact_dot_left(ut01, la_c))
        a_r = jnp.where(sub16 < H, _exact_dot_right(la_r, ut01), _exact_dot_right(la_r, lt01))
        return dt_c, dt_r, a_c, a_r

    def fwd(c, carry):
        r0 = pl.multiple_of(c * CHUNK, CHUNK)
        acc = jnp.broadcast_to(cb_ref[...], (CHUNK, 2 * GW))
        win = xpad_ref.at[pl.ds(r0, CHUNK + 2 * pad), :]
        off = pad - SSM_K // 2
        for k in range(SSM_K):
            acc = acc + win[off + k:off + k + CHUNK, :] * cw_ref[k:k + 1, :]
        xc = _silu(acc)
        xc_scr[pl.ds(r0, CHUNK), :] = xc
        dt_c, dt_r, a_c, a_r = decays(r0)
        tot_f = a_c[CHUNK - 1:CHUNK, :]
        ks_f = dt_c * jnp.exp(tot_f - a_c)
        es_f = jnp.exp(a_c)
        for g in range(SSM_GROUPS):
            bm = xc[:, GW + g * N:GW + (g + 1) * N]
            cm = xc[:, GW + SSM_GROUPS * N + g * N:GW + SSM_GROUPS * N + (g + 1) * N]
            bmb = bm.astype(bf16)
            cmb = cm.astype(bf16)
            gmat = _nt(cmb, bmb)
            for hh in range(H // SSM_GROUPS):
                h = g * (H // SSM_GROUPS) + hh
                xs = xc[:, h * P:(h + 1) * P]
                df = jnp.exp(jnp.minimum(a_c[:, h:h + 1] - a_r[h:h + 1, :], 0.0)) * dt_r[h:h + 1, :]
                db = jnp.exp(jnp.minimum(a_c[:, H + h:H + h + 1] - a_r[H + h:H + h + 1, :], 0.0)) \
                    * dt_r[H + h:H + h + 1, :]
                dm = jnp.where(col <= row, df, 0.0) + jnp.where(col >= row, db, 0.0)
                m = (gmat * dm).astype(bf16)
                y = _nn(m, xs.astype(bf16))
                y = y + (dsk_ref[0:1, h * P:(h + 1) * P] + dsk_ref[1:2, h * P:(h + 1) * P]) * xs
                st = st_scr[0, h]
                y = y + es_f[:, h:h + 1] * _nt(cmb, st.astype(bf16))
                cs = _tn((xs * ks_f[:, h:h + 1]).astype(bf16), bmb)
                st_scr[0, h] = st * jnp.exp(tot_f[:, h:h + 1]) + cs
                y_scr[pl.ds(r0, CHUNK), h * P:(h + 1) * P] = y
        return carry

    lax.fori_loop(0, nc, fwd, 0)

    def bwd(i, carry):
        r0 = pl.multiple_of((nc - 1 - i) * CHUNK, CHUNK)
        xc = xc_scr[pl.ds(r0, CHUNK), :]
        dt_c, dt_r, a_c, a_r = decays(r0)
        tot_b = a_c[0:1, :]
        ks_b = dt_c * jnp.exp(tot_b - a_c)
        es_b = jnp.exp(a_c)
        for g in range(SSM_GROUPS):
            bmb = xc[:, GW + g * N:GW + (g + 1) * N].astype(bf16)
            cmb = xc[:, GW + SSM_GROUPS * N + g * N:GW + SSM_GROUPS * N + (g + 1) * N].astype(bf16)
            for hh in range(H // SSM_GROUPS):
                h = g * (H // SSM_GROUPS) + hh
                xs = xc[:, h * P:(h + 1) * P]
                st = st_scr[1, h]
                y = y_scr[pl.ds(r0, CHUNK), h * P:(h + 1) * P] \
                    + es_b[:, H + h:H + h + 1] * _nt(cmb, st.astype(bf16))
                cs = _tn((xs * ks_b[:, H + h:H + h + 1]).astype(bf16), bmb)
                st_scr[1, h] = st * jnp.exp(tot_b[:, H + h:H + h + 1]) + cs
                y_scr[pl.ds(r0, CHUNK), h * P:(h + 1) * P] = y
        yz = y_scr[pl.ds(r0, CHUNK), :] * _silu(z_ref[pl.ds(r0, CHUNK), :])
        ms = jnp.mean(yz * yz, axis=-1, keepdims=True)
        o_ref[pl.ds(r0, CHUNK), :] = (yz * lax.rsqrt(ms + EPS) * ng_ref[...]).astype(o_ref.dtype)
        return carry

    lax.fori_loop(0, nc, bwd, 0)

    if want_state:
        for d in range(2):
            for h in range(H):
                hout_ref[d, h] = st_scr[d, h]


def _pad_lanes(v, n=128):
    return jnp.pad(v, ((0, 0), (0, n - v.shape[1])))


def _ssm(u, dt_t, p, h0, layer, nb, L, want_state):
    has_h0 = h0 is not None
    H = SSM_HEADS
    a_log = p['ssm_a_log'].reshape(1, 2 * H)
    dt_bias = p['ssm_dt_bias'].reshape(1, 2 * H)
    small = lambda shape: pl.BlockSpec(shape, lambda b: (0, 0))
    in_specs = [pl.BlockSpec((L, 2 * GW), lambda b: (b, C_XBC // (2 * GW))),
                pl.BlockSpec((L, GW), lambda b: (b, C_Z // GW)),
                pl.BlockSpec((L, 128), lambda b: (b, C_DT // 128)),
                pl.BlockSpec((2 * H, L), lambda b: (0, b)),
                small((SSM_K, 2 * GW)), small((1, 2 * GW)),
                small((1, 128)), small((2 * H, 128)), small((1, 128)), small((2 * H, 128)),
                small((2, GW)), small((1, GW))]
    args = [u, u, u, dt_t, p['ssm_conv_w'], p['ssm_conv_b'].reshape(1, 2 * GW),
            _pad_lanes(a_log), jnp.broadcast_to(a_log.reshape(2 * H, 1), (2 * H, 128)),
            _pad_lanes(dt_bias), jnp.broadcast_to(dt_bias.reshape(2 * H, 1), (2 * H, 128)),
            jnp.repeat(p['ssm_d'], SSM_P, axis=1), p['ssm_norm'].reshape(1, GW)]
    if has_h0:
        in_specs.append(pl.BlockSpec((None, None, 2, H, SSM_P, SSM_N), lambda b: (b, layer, 0, 0, 0, 0)))
        args.append(h0)
    out_specs = [pl.BlockSpec((L, GW), lambda b: (b, 0))]
    out_shape = [jax.ShapeDtypeStruct((nb * L, GW), bf16)]
    if want_state:
        out_specs.append(pl.BlockSpec((None, 2, H, SSM_P, SSM_N), lambda b: (b, 0, 0, 0, 0)))
        out_shape.append(jax.ShapeDtypeStruct((nb, 2, H, SSM_P, SSM_N), f32))
    res = pl.pallas_call(
        functools.partial(_ssm_kernel, L=L, has_h0=has_h0, want_state=want_state),
        grid=(nb,),
        in_specs=in_specs,
        out_specs=out_specs,
        out_shape=out_shape,
        scratch_shapes=[pltpu.VMEM((L + 16, 2 * GW), f32),
                        pltpu.VMEM((L, 2 * GW), f32),
                        pltpu.VMEM((L, GW), f32),
                        pltpu.VMEM((2, H, SSM_P, SSM_N), f32)],
        compiler_params=_cparams(("arbitrary",)),
        name="ssd_mixer",
    )(*args)
    return res if want_state else (res[0], None)


def _ctx_att_kernel(sink_ref, q_ref, k_ref, v_ref, o_ref, *, L):
    G = ATT_HEADS // ATT_KV
    scale = ATT_D ** -0.5
    for j in range(ATT_KV):
        kb = k_ref[:, j * ATT_D:(j + 1) * ATT_D].astype(bf16)
        vb = v_ref[:, j * ATT_D:(j + 1) * ATT_D].astype(bf16)
        for gg in range(G):
            h = j * G + gg
            qb = q_ref[:, h * ATT_D:(h + 1) * ATT_D].astype(bf16)
            s = _nt(qb, kb) * scale
            sink = sink_ref[h]
            m = jnp.maximum(jnp.max(s, axis=-1, keepdims=True), sink)
            p = jnp.exp(s - m)
            den = jnp.sum(p, axis=-1, keepdims=True) + jnp.exp(sink - m)
            o = _nn(p.astype(bf16), vb) / den
            o_ref[:, h * ATT_D:(h + 1) * ATT_D] = o.astype(o_ref.dtype)


def _ctx_attention(u, sink, nb, L):
    return pl.pallas_call(
        functools.partial(_ctx_att_kernel, L=L),
        grid=(nb,),
        in_specs=[pl.BlockSpec(memory_space=pltpu.SMEM),
                  pl.BlockSpec((L, GW), lambda b: (b, C_AQ // GW)),
                  pl.BlockSpec((L, 128), lambda b: (b, C_AK // 128)),
                  pl.BlockSpec((L, 128), lambda b: (b, C_AV // 128))],
        out_specs=pl.BlockSpec((L, GW), lambda b: (b, 0)),
        out_shape=jax.ShapeDtypeStruct((nb * L, GW), bf16),
        compiler_params=_cparams(("arbitrary",)),
        name="ctx_attention",
    )(sink, u, u, u)


def _rope(x, cos, sin):
    w = x.shape[1]
    lane = lax.broadcasted_iota(jnp.int32, x.shape, 1)
    first = (lane % 32) < 16
    rot = jnp.where(first, -pltpu.roll(x, w - 16, 1), pltpu.roll(x, 16, 1))
    return x * cos + rot * sin


def _lat_att_kernel(sink_ref, q_ref, k_ref, v_ref, kc_ref, vc_ref, cos_ref, sin_ref, o_ref,
                    qr_scr, kpad_scr, vpad_scr, *, L):
    G = ATT_HEADS // ATT_KV
    B = CHUNK
    nb = L // B
    Lc = kc_ref.shape[0]
    scale = ATT_D ** -0.5
    zeros = jnp.zeros((B, 128), bf16)
    kpad_scr[0:B, :] = zeros
    kpad_scr[B + L:2 * B + L, :] = zeros
    vpad_scr[0:B, :] = zeros
    vpad_scr[B + L:2 * B + L, :] = zeros

    def prep(n, carry):
        r0 = pl.multiple_of(n * B, B)
        cos = cos_ref[pl.ds(r0, B), :]
        sin = sin_ref[pl.ds(r0, B), :]
        cos4 = jnp.concatenate([cos] * 4, axis=1)
        sin4 = jnp.concatenate([sin] * 4, axis=1)
        qr_scr[pl.ds(r0, B), :] = _rope(q_ref[pl.ds(r0, B), :], cos4, sin4).astype(bf16)
        kpad_scr[pl.ds(B + r0, B), :] = _rope(k_ref[pl.ds(r0, B), :], cos, sin).astype(bf16)
        vpad_scr[pl.ds(B + r0, B), :] = v_ref[pl.ds(r0, B), :].astype(bf16)
        return carry

    lax.fori_loop(0, nb, prep, 0)

    rowi = lax.broadcasted_iota(jnp.int32, (G * B, 3 * B), 0) % B
    coli = lax.broadcasted_iota(jnp.int32, (G * B, 3 * B), 1)

    def blk(n, carry):
        r0 = pl.multiple_of(n * B, B)
        kpos = (n - 1) * B + coli
        qpos = n * B + rowi
        valid = (jnp.abs(qpos - kpos) <= ATT_WIN) & (kpos >= 0) & (kpos < L)
        qblk = qr_scr[pl.ds(r0, B), :]
        for j in range(ATT_KV):
            sl = slice(j * ATT_D, (j + 1) * ATT_D)
            qs = jnp.concatenate([qblk[:, (j * G + gg) * ATT_D:(j * G + gg + 1) * ATT_D]
                                  for gg in range(G)], axis=0)
            kc = kc_ref[:, sl].astype(bf16)
            vc = vc_ref[:, sl].astype(bf16)
            kb = kpad_scr[pl.ds(r0, 3 * B), sl]
            vb = vpad_scr[pl.ds(r0, 3 * B), sl]
            s_c = _nt(qs, kc) * scale
            s_b = jnp.where(valid, _nt(qs, kb) * scale, -jnp.inf)
            sink = jnp.concatenate(
                [jnp.full((B, 1), sink_ref[j * G + gg], f32) for gg in range(G)], axis=0)
            m = jnp.maximum(jnp.maximum(jnp.max(s_c, axis=-1, keepdims=True),
                                        jnp.max(s_b, axis=-1, keepdims=True)), sink)
            p_c = jnp.exp(s_c - m)
            p_b = jnp.exp(s_b - m)
            den = (jnp.sum(p_c, axis=-1, keepdims=True) + jnp.sum(p_b, axis=-1, keepdims=True)
                   + jnp.exp(sink - m))
            o = (_nn(p_c.astype(bf16), vc) + _nn(p_b.astype(bf16), vb)) / den
            for gg in range(G):
                h = j * G + gg
                o_ref[pl.ds(r0, B), h * ATT_D:(h + 1) * ATT_D] = \
                    o[gg * B:(gg + 1) * B, :].astype(o_ref.dtype)
        return carry

    lax.fori_loop(0, nb, blk, 0)


def _rope_tables(L):
    pos = jnp.arange(L)
    rows = (pos // GRID_W).astype(f32)
    cols = (pos % GRID_W).astype(f32)
    half = ATT_D // 4
    freqs = ROPE_THETA ** (-jnp.arange(half, dtype=f32) / half)
    ang_r = rows[:, None] * freqs
    ang_c = cols[:, None] * freqs
    ang = jnp.concatenate([ang_r, ang_r, ang_c, ang_c], axis=1)
    ang = jnp.concatenate([ang, ang], axis=1)
    return jnp.cos(ang), jnp.sin(ang)


def _lat_attention(u, k_cache, v_cache, sink, layer, nb, L):
    Lc = k_cache.shape[2]
    cos, sin = _rope_tables(L)
    kc = k_cache.reshape(nb, DEPTH, Lc, ATT_KV * ATT_D)
    vc = v_cache.reshape(nb, DEPTH, Lc, ATT_KV * ATT_D)
    cspec = pl.BlockSpec((None, None, Lc, 128), lambda b: (b, layer, 0, 0))
    return pl.pallas_call(
        functools.partial(_lat_att_kernel, L=L),
        grid=(nb,),
        in_specs=[pl.BlockSpec(memory_space=pltpu.SMEM),
                  pl.BlockSpec((L, GW), lambda b: (b, C_AQ // GW)),
                  pl.BlockSpec((L, 128), lambda b: (b, C_AK // 128)),
                  pl.BlockSpec((L, 128), lambda b: (b, C_AV // 128)),
                  cspec, cspec,
                  pl.BlockSpec((L, 128), lambda b: (0, 0)),
                  pl.BlockSpec((L, 128), lambda b: (0, 0))],
        out_specs=pl.BlockSpec((L, GW), lambda b: (b, 0)),
        out_shape=jax.ShapeDtypeStruct((nb * L, GW), bf16),
        scratch_shapes=[pltpu.VMEM((L, GW), bf16),
                        pltpu.VMEM((L + 2 * CHUNK, 128), bf16),
                        pltpu.VMEM((L + 2 * CHUNK, 128), bf16)],
        compiler_params=_cparams(("arbitrary",)),
        name="lat_attention",
    )(sink, u, u, u, kc, vc, cos, sin)


def _permute_w_in(w):
    z = jnp.zeros((D, 128 - 2 * SSM_HEADS), w.dtype)
    return jnp.concatenate([w[:, 0:1024], w[:, 1536:2560], w[:, 1024:1536], w[:, 2576:5392],
                            w[:, 2560:2576], z], axis=1).astype(bf16)


def _layer(x, mod, p, ctx, layer, nb, L, final):
    per_seq = ctx is not None
    u = _inproj(x, p['norm_mix'], mod, p['w_in_p'], L, per_seq)
    dt_t = u[:, C_DT:C_DT + 2 * SSM_HEADS].T
    want_state = ctx is None
    o_conv = _conv_module(u, p['conv_w'], p['conv_b'], p['conv_ln_g'], p['conv_ln_b'], nb, L)
    if ctx is None:
        o_ssm, h_ssm = _ssm(u, dt_t, p, None, layer, nb, L, True)
        o_ret, h_ret = _retention(u, p['ret_log_decay'], p['ret_gn_g'], None, layer, nb, L, True)
        o_att = _ctx_attention(u, p['att_sink'], nb, L)
    else:
        k_c, v_c, s_ssm, s_ret = ctx
        o_ssm, h_ssm = _ssm(u, dt_t, p, s_ssm, layer, nb, L, False)
        o_ret, h_ret = _retention(u, p['ret_log_decay'], p['ret_gn_g'], s_ret, layer, nb, L, False)
        o_att = _lat_attention(u, k_c, v_c, p['att_sink'], layer, nb, L)
    x1 = _outproj(o_conv, o_ssm, o_ret, o_att, p['w_out_b'], x, mod, L, per_seq)
    x2 = _mlp(x1, p['norm_mlp'], mod, p['final_norm'], p['w1_b'], p['w2_b'], L, per_seq, final)
    states = None
    if want_state:
        ak = u[:, C_AK:C_AK + 128].reshape(nb, L, ATT_KV, ATT_D)
        av = u[:, C_AV:C_AV + 128].reshape(nb, L, ATT_KV, ATT_D)
        states = (ak, av, h_ssm, h_ret)
    return x2, states


def kernel(x_prompt, x_sample, cache_attn_k, cache_attn_v, state_ssm, state_ret, c, c_ctx, ada_w, ada_b, norm_mix, norm_mlp, w_in, conv_w, conv_b, conv_ln_g, conv_ln_b, ssm_conv_w, ssm_conv_b, ssm_a_log, ssm_dt_bias, ssm_d, ssm_norm, ret_log_decay, ret_gn_g, att_sink, w_out, w1, w2, final_norm):
    nbp, Lp, _ = x_prompt.shape
    nbs, Ls, _ = x_sample.shape
    cvec = jnp.concatenate([c_ctx[None, :], c, jnp.zeros((16 - 1 - nbs, D), f32)], axis=0)
    mod = _ada_mod(cvec, ada_w, ada_b)
    y_p = x_prompt.reshape(nbp * Lp, D)
    y_s = x_sample.reshape(nbs * Ls, D)
    new_k, new_v, new_ssm, new_ret = [], [], [], []
    for l in range(DEPTH):
        p = dict(norm_mix=norm_mix[l], norm_mlp=norm_mlp[l], w_in_p=_permute_w_in(w_in[l]),
                 conv_w=conv_w[l], conv_b=conv_b[l], conv_ln_g=conv_ln_g[l], conv_ln_b=conv_ln_b[l],
                 ssm_conv_w=ssm_conv_w[l], ssm_conv_b=ssm_conv_b[l], ssm_a_log=ssm_a_log[l],
                 ssm_dt_bias=ssm_dt_bias[l], ssm_d=ssm_d[l], ssm_norm=ssm_norm[l],
                 ret_log_decay=ret_log_decay[l], ret_gn_g=ret_gn_g[l], att_sink=att_sink[l],
                 w_out_b=w_out[l].astype(bf16), w1_b=w1[l].astype(bf16), w2_b=w2[l].astype(bf16),
                 final_norm=final_norm)
        mod_l = mod[l].reshape(16, 6, 1, D)
        final = l == DEPTH - 1
        y_p, (k_l, v_l, hs_l, hr_l) = _layer(y_p, mod_l[0:1], p, None, l, nbp, Lp, final)
        new_k.append(k_l)
        new_v.append(v_l)
        new_ssm.append(hs_l)
        new_ret.append(hr_l)
        ctx = (cache_attn_k, cache_attn_v, state_ssm, state_ret)
        y_s, _ = _layer(y_s, mod_l[1:1 + nbs], p, ctx, l, nbs, Ls, final)
    return (y_p.reshape(nbp, Lp, D), y_s.reshape(nbs, Ls, D),
            jnp.stack(new_k, axis=1), jnp.stack(new_v, axis=1),
            jnp.stack(new_ssm, axis=1), jnp.stack(new_ret, axis=1))
```

```python
import functools

import jax
import jax.numpy as jnp
from jax import lax
from jax.experimental import pallas as pl
from jax.experimental.pallas import tpu as pltpu

f32 = jnp.float32
bf16 = jnp.bfloat16

D = 2048
DEPTH = 2
GW = 512
CONV_K = 31
SSM_HEADS, SSM_P, SSM_N, SSM_GROUPS, SSM_K = 8, 64, 128, 2, 5
RET_HEADS, RET_D = 4, 128
ATT_HEADS, ATT_KV, ATT_D, ATT_WIN = 8, 2, 64, 128
GRID_W = 64
ROPE_THETA = 10000.0
CHUNK = 128
D_FF = 4 * D
EPS = 1e-6

C_CONV, C_XBC, C_Z, C_RQ, C_RK, C_RV, C_RG, C_AQ, C_AK, C_AV, C_DT = (
    0, 1024, 2048, 2560, 3072, 3584, 4096, 4608, 5120, 5248, 5376)
UW = 5504
N_CHUNK = 512

VMEM_LIMIT = 56 * 1024 * 1024


def _cparams(sem):
    return pltpu.CompilerParams(dimension_semantics=sem, vmem_limit_bytes=VMEM_LIMIT)


def _sigmoid(x):
    return 1.0 / (1.0 + jnp.exp(-x))


def _silu(x):
    return x * _sigmoid(x)


def _nt(a, b):
    return lax.dot_general(a, b, (((1,), (1,)), ((), ())), preferred_element_type=f32)


def _tn(a, b):
    return lax.dot_general(a, b, (((0,), (0,)), ((), ())), preferred_element_type=f32)


def _nn(a, b):
    return jnp.dot(a, b, preferred_element_type=f32)


def _split3(x):
    hi = x.astype(bf16)
    r1 = x - hi.astype(f32)
    mid = r1.astype(bf16)
    lo = (r1 - mid.astype(f32)).astype(bf16)
    return hi, mid, lo


def _exact_dot_right(x, m01):
    hi, mid, lo = _split3(x)
    return _nn(hi, m01) + _nn(mid, m01) + _nn(lo, m01)


def _exact_dot_left(m01, x):
    hi, mid, lo = _split3(x)
    return _nn(m01, hi) + _nn(m01, mid) + _nn(m01, lo)


def _ada_kernel(c_ref, w_ref, b_ref, o_ref):
    c = c_ref[...]
    s = _silu(c).astype(bf16)
    o_ref[...] = _nn(s, w_ref[...].astype(bf16)) + b_ref[...]


def _ada_mod(cvec, ada_w, ada_b):
    tn = 1024
    return pl.pallas_call(
        _ada_kernel,
        grid=(DEPTH, 6 * D // tn),
        in_specs=[pl.BlockSpec((16, D), lambda l, j: (0, 0)),
                  pl.BlockSpec((None, D, tn), lambda l, j: (l, 0, j)),
                  pl.BlockSpec((None, 1, tn), lambda l, j: (l, 0, j))],
        out_specs=pl.BlockSpec((None, 16, tn), lambda l, j: (l, 0, j)),
        out_shape=jax.ShapeDtypeStruct((DEPTH, 16, 6 * D), f32),
        compiler_params=_cparams(("arbitrary", "arbitrary")),
        name="ada_mod",
    )(cvec, ada_w, ada_b.reshape(DEPTH, 1, 6 * D))


def _mod_spec(idx, tm, L, per_seq):
    if per_seq:
        return pl.BlockSpec((None, None, 1, D), lambda i, *_: ((i * tm) // L, idx, 0, 0))
    return pl.BlockSpec((None, None, 1, D), lambda i, *_: (0, idx, 0, 0))


def _inproj_kernel(x_ref, g_ref, sc_ref, sh_ref, w_ref, u_ref):
    x = x_ref[...]
    ms = jnp.mean(x * x, axis=-1, keepdims=True)
    h = x * lax.rsqrt(ms + EPS) * g_ref[...] * (1.0 + sc_ref[...]) + sh_ref[...]
    hb = h.astype(bf16)
    for n0 in range(0, UW, N_CHUNK):
        n1 = min(n0 + N_CHUNK, UW)
        u_ref[:, n0:n1] = _nn(hb, w_ref[:, n0:n1])


def _inproj(x, norm_g, mod, w_in_p, layer, L, per_seq):
    T = x.shape[0]
    tm = 256
    return pl.pallas_call(
        _inproj_kernel,
        grid=(T // tm,),
        in_specs=[pl.BlockSpec((tm, D), lambda i: (i, 0)),
                  pl.BlockSpec((1, D), lambda i: (0, 0)),
                  _mod_spec(1, tm, L, per_seq),
                  _mod_spec(0, tm, L, per_seq),
                  pl.BlockSpec((None, D, UW), lambda i: (layer, 0, 0), pipeline_mode=pl.Buffered(1))],
        out_specs=pl.BlockSpec((tm, UW), lambda i: (i, 0)),
        out_shape=jax.ShapeDtypeStruct((T, UW), f32),
        compiler_params=_cparams(("arbitrary",)),
        name="in_proj",
    )(x, norm_g.reshape(1, D), mod, mod, w_in_p)


def _outproj_kernel(oc_ref, os_ref, or_ref, oa_ref, w_ref, x_ref, g_ref, o_ref):
    for n0 in range(0, D, N_CHUNK):
        n1 = n0 + N_CHUNK
        acc = _nn(oc_ref[...], w_ref[0:GW, n0:n1])
        acc = acc + _nn(os_ref[...], w_ref[GW:2 * GW, n0:n1])
        acc = acc + _nn(or_ref[...], w_ref[2 * GW:3 * GW, n0:n1])
        acc = acc + _nn(oa_ref[...], w_ref[3 * GW:4 * GW, n0:n1])
        o_ref[:, n0:n1] = x_ref[:, n0:n1] + g_ref[:, n0:n1] * acc


def _outproj(o_conv, o_ssm, o_ret, o_att, w_out_b, layer, x, mod, L, per_seq):
    T = x.shape[0]
    tm = 512
    ospec = pl.BlockSpec((tm, GW), lambda i: (i, 0))
    return pl.pallas_call(
        _outproj_kernel,
        grid=(T // tm,),
        in_specs=[ospec, ospec, ospec, ospec,
                  pl.BlockSpec((None, D, D), lambda i: (layer, 0, 0), pipeline_mode=pl.Buffered(1)),
                  pl.BlockSpec((tm, D), lambda i: (i, 0)),
                  _mod_spec(2, tm, L, per_seq)],
        out_specs=pl.BlockSpec((tm, D), lambda i: (i, 0)),
        out_shape=jax.ShapeDtypeStruct((T, D), f32),
        compiler_params=_cparams(("arbitrary",)),
        name="out_proj",
    )(o_conv, o_ssm, o_ret, o_att, w_out_b, x, mod)


def _mlp_kernel(x_ref, g_ref, sc_ref, sh_ref, g2_ref, fn_ref, w1_ref, w2_ref, o_ref, h_scr, *, final):
    f = pl.program_id(1)
    tm = x_ref.shape[0]
    RB = 256

    @pl.when(f == 0)
    def _():
        for r in range(0, tm, RB):
            x = x_ref[r:r + RB, :]
            ms = jnp.mean(x * x, axis=-1, keepdims=True)
            h = x * lax.rsqrt(ms + EPS) * g_ref[...] * (1.0 + sc_ref[...]) + sh_ref[...]
            h_scr[r:r + RB, :] = h.astype(bf16)
        o_ref[...] = jnp.zeros_like(o_ref)

    for r in range(0, tm, 2 * RB):
        a = _nn(h_scr[r:r + 2 * RB, :], w1_ref[...])
        a = jnp.maximum(a, 0.0)
        a = (a * a).astype(bf16)
        o_ref[r:r + 2 * RB, :] += _nn(a, w2_ref[...])

    @pl.when(f == pl.num_programs(1) - 1)
    def _():
        for r in range(0, tm, RB):
            y = x_ref[r:r + RB, :] + g2_ref[...] * o_ref[r:r + RB, :]
            if final:
                ms = jnp.mean(y * y, axis=-1, keepdims=True)
                y = y * lax.rsqrt(ms + EPS) * fn_ref[...]
            o_ref[r:r + RB, :] = y


def _mlp(x, norm_g, mod, final_norm, w1_b, w2_b, layer, L, per_seq, final):
    T = x.shape[0]
    tm, tf = 1024, 512
    return pl.pallas_call(
        functools.partial(_mlp_kernel, final=final),
        grid=(T // tm, D_FF // tf),
        in_specs=[pl.BlockSpec((tm, D), lambda i, f: (i, 0), pipeline_mode=pl.Buffered(1)),
                  pl.BlockSpec((1, D), lambda i, f: (0, 0)),
                  _mod_spec(4, tm, L, per_seq),
                  _mod_spec(3, tm, L, per_seq),
                  _mod_spec(5, tm, L, per_seq),
                  pl.BlockSpec((1, D), lambda i, f: (0, 0)),
                  pl.BlockSpec((None, D, tf), lambda i, f: (layer, 0, f)),
                  pl.BlockSpec((None, tf, D), lambda i, f: (layer, f, 0))],
        out_specs=pl.BlockSpec((tm, D), lambda i, f: (i, 0)),
        out_shape=jax.ShapeDtypeStruct((T, D), f32),
        scratch_shapes=[pltpu.VMEM((tm, D), bf16)],
        compiler_params=_cparams(("arbitrary", "arbitrary")),
        name="mlp",
    )(x, norm_g.reshape(1, D), mod, mod, mod, final_norm.reshape(1, D), w1_b, w2_b)


def _conv_kernel(u_ref, w_ref, b_ref, lg_ref, lb_ref, o_ref, vpad_ref, ph_ref, *, L):
    RC = 64
    pad = 16
    SUB = 8
    off0 = pad - CONV_K // 2
    nph = RC + SUB * ((off0 + CONV_K - 1) // SUB)
    vpad_ref[0:pad, :] = jnp.zeros((pad, GW), f32)
    vpad_ref[pad + L:2 * pad + L, :] = jnp.zeros((pad, GW), f32)

    def fill(i, carry):
        r0 = pl.multiple_of(i * RC, RC)
        a = u_ref[pl.ds(r0, RC), 0:GW]
        g = u_ref[pl.ds(r0, RC), GW:2 * GW]
        vpad_ref[pl.ds(pad + r0, RC), :] = a * _sigmoid(g)
        return carry

    lax.fori_loop(0, L // RC, fill, 0)

    def body(i, carry):
        r0 = pl.multiple_of(i * RC, RC)
        acc = jnp.broadcast_to(b_ref[...], (RC, GW))
        win = vpad_ref.at[pl.ds(r0, RC + 2 * pad), :]
        for ph in range(SUB):
            ph_ref[ph] = win[ph:ph + nph, :]
        for k in range(CONV_K):
            a, ph = divmod(off0 + k, SUB)
            acc = acc + ph_ref[ph, SUB * a:SUB * a + RC, :] * w_ref[k:k + 1, :]
        mu = jnp.mean(acc, axis=-1, keepdims=True)
        xc = acc - mu
        var = jnp.mean(xc * xc, axis=-1, keepdims=True)
        y = xc * lax.rsqrt(var + EPS) * lg_ref[...] + lb_ref[...]
        o_ref[pl.ds(r0, RC), :] = _silu(y).astype(o_ref.dtype)
        return carry

    lax.fori_loop(0, L // RC, body, 0)


def _conv_module(u, conv_w, conv_b, ln_g, ln_b, nb, L):
    return pl.pallas_call(
        functools.partial(_conv_kernel, L=L),
        grid=(nb,),
        in_specs=[pl.BlockSpec((L, 2 * GW), lambda b: (b, C_CONV // (2 * GW))),
                  pl.BlockSpec((CONV_K, GW), lambda b: (0, 0)),
                  pl.BlockSpec((1, GW), lambda b: (0, 0)),
                  pl.BlockSpec((1, GW), lambda b: (0, 0)),
                  pl.BlockSpec((1, GW), lambda b: (0, 0))],
        out_specs=pl.BlockSpec((L, GW), lambda b: (b, 0)),
        out_shape=jax.ShapeDtypeStruct((nb * L, GW), bf16),
        scratch_shapes=[pltpu.VMEM((L + 32, GW), f32), pltpu.VMEM((8, 64 + 24, GW), f32)],
        compiler_params=_cparams(("arbitrary",)),
        name="conv_module",
    )(u, conv_w, conv_b.reshape(1, GW), ln_g.reshape(1, GW), ln_b.reshape(1, GW))


def _ret_kernel(*refs, L, has_h0, want_state):
    ld_ref, q_ref, k_ref, v_ref, g_ref, gn_ref = refs[:6]
    pos = 6
    h0_ref = None
    if has_h0:
        h0_ref = refs[pos]
        pos += 1
    o_ref = refs[pos]
    pos += 1
    hout_ref = None
    if want_state:
        hout_ref = refs[pos]
        pos += 1
    y_scr, st_scr, dec_scr = refs[pos:pos + 3]

    nc = L // CHUNK
    row = lax.broadcasted_iota(jnp.int32, (CHUNK, CHUNK), 0)
    col = lax.broadcasted_iota(jnp.int32, (CHUNK, CHUNK), 1)
    rowf = row.astype(f32)
    diff = (row - col).astype(f32)
    kscale = RET_D ** -0.5

    for d in range(2):
        for h in range(RET_HEADS):
            if has_h0:
                st_scr[d, h] = h0_ref[d, h]
            else:
                st_scr[d, h] = jnp.zeros((RET_D, RET_D), f32)
    for h in range(RET_HEADS):
        laf = ld_ref[h]
        lab = ld_ref[RET_HEADS + h]
        dec_scr[h] = jnp.where(col < row, jnp.exp(diff * laf),
                               jnp.where(col > row, jnp.exp(-diff * lab), 2.0))

    def load(r0, h):
        sl = slice(h * RET_D, (h + 1) * RET_D)
        qb = q_ref[pl.ds(r0, CHUNK), sl].astype(bf16)
        kb = (k_ref[pl.ds(r0, CHUNK), sl] * kscale).astype(bf16)
        vh = v_ref[pl.ds(r0, CHUNK), sl]
        return sl, qb, kb, vh

    def fwd(c, carry):
        r0 = pl.multiple_of(c * CHUNK, CHUNK)
        for h in range(RET_HEADS):
            sl, qb, kb, vh = load(r0, h)
            laf = ld_ref[h]
            m = (_nt(qb, kb) * dec_scr[h]).astype(bf16)
            y = _nn(m, vh.astype(bf16))
            st = st_scr[0, h]
            y = y + jnp.exp((rowf + 1.0) * laf) * _nt(qb, st.astype(bf16))
            wend = jnp.exp((CHUNK - 1.0 - rowf) * laf)
            cs = _tn((vh * wend).astype(bf16), kb)
            st_scr[0, h] = st * jnp.exp(jnp.full((RET_D, RET_D), CHUNK * laf, f32)) + cs
            y_scr[pl.ds(r0, CHUNK), sl] = y
        return carry

    lax.fori_loop(0, nc, fwd, 0)

    def bwd(i, carry):
        r0 = pl.multiple_of((nc - 1 - i) * CHUNK, CHUNK)
        for h in range(RET_HEADS):
            sl, qb, kb, vh = load(r0, h)
            lab = ld_ref[RET_HEADS + h]
            st = st_scr[1, h]
            y = y_scr[pl.ds(r0, CHUNK), sl] + jnp.exp((CHUNK - rowf) * lab) * _nt(qb, st.astype(bf16))
            wend = jnp.exp(rowf * lab)
            cs = _tn((vh * wend).astype(bf16), kb)
            st_scr[1, h] = st * jnp.exp(jnp.full((RET_D, RET_D), CHUNK * lab, f32)) + cs
            mu = jnp.mean(y, axis=-1, keepdims=True)
            yc = y - mu
            var = jnp.mean(yc * yc, axis=-1, keepdims=True)
            yn = yc * lax.rsqrt(var + EPS) * gn_ref[:, sl]
            g = g_ref[pl.ds(r0, CHUNK), sl]
            o_ref[pl.ds(r0, CHUNK), sl] = (_silu(g) * yn).astype(o_ref.dtype)
        return carry

    lax.fori_loop(0, nc, bwd, 0)

    if want_state:
        for d in range(2):
            for h in range(RET_HEADS):
                hout_ref[d, h] = st_scr[d, h]


def _retention(u, log_decay, gn_g, h0, layer, nb, L, want_state):
    has_h0 = h0 is not None
    cblk = lambda c: pl.BlockSpec((L, GW), lambda b, c=c: (b, c // GW))
    in_specs = [pl.BlockSpec(memory_space=pltpu.SMEM),
                cblk(C_RQ), cblk(C_RK), cblk(C_RV), cblk(C_RG),
                pl.BlockSpec((1, GW), lambda b: (0, 0))]
    args = [log_decay.reshape(2 * RET_HEADS), u, u, u, u, gn_g.reshape(1, GW)]
    if has_h0:
        in_specs.append(pl.BlockSpec((None, None, 2, RET_HEADS, RET_D, RET_D),
                                     lambda b: (b, layer, 0, 0, 0, 0)))
        args.append(h0)
    out_specs = [pl.BlockSpec((L, GW), lambda b: (b, 0))]
    out_shape = [jax.ShapeDtypeStruct((nb * L, GW), bf16)]
    if want_state:
        out_specs.append(pl.BlockSpec((None, 2, RET_HEADS, RET_D, RET_D), lambda b: (b, 0, 0, 0, 0)))
        out_shape.append(jax.ShapeDtypeStruct((nb, 2, RET_HEADS, RET_D, RET_D), f32))
    res = pl.pallas_call(
        functools.partial(_ret_kernel, L=L, has_h0=has_h0, want_state=want_state),
        grid=(nb,),
        in_specs=in_specs,
        out_specs=out_specs,
        out_shape=out_shape,
        scratch_shapes=[pltpu.VMEM((L, GW), f32),
                        pltpu.VMEM((2, RET_HEADS, RET_D, RET_D), f32),
                        pltpu.VMEM((RET_HEADS, CHUNK, CHUNK), f32)],
        compiler_params=_cparams(("arbitrary",)),
        name="retention",
    )(*args)
    return res if want_state else (res[0], None)


def _softplus(x):
    return jnp.maximum(x, 0.0) + jnp.log(1.0 + jnp.exp(-jnp.abs(x)))


def _ssm_kernel(*refs, L, has_h0, want_state):
    (xbc_ref, z_ref, dtc_ref, dtr_ref, cw_ref, cb_ref, alr_ref, alc_ref, dbr_ref, dbc_ref,
     dsk_ref, ng_ref) = refs[:12]
    pos = 12
    h0_ref = None
    if has_h0:
        h0_ref = refs[pos]
        pos += 1
    o_ref = refs[pos]
    pos += 1
    hout_ref = None
    if want_state:
        hout_ref = refs[pos]
        pos += 1
    xpad_ref, xc_scr, y_scr, st_scr = refs[pos:pos + 4]

    nc = L // CHUNK
    H, P, N = SSM_HEADS, SSM_P, SSM_N
    pad = 8
    row = lax.broadcasted_iota(jnp.int32, (CHUNK, CHUNK), 0)
    col = lax.broadcasted_iota(jnp.int32, (CHUNK, CHUNK), 1)
    lt01 = (row >= col).astype(bf16)
    ut01 = (row <= col).astype(bf16)
    sub16 = lax.broadcasted_iota(jnp.int32, (2 * H, CHUNK), 0)

    xpad_ref[0:pad, :] = jnp.zeros((pad, 2 * GW), f32)
    xpad_ref[pad + L:2 * pad + L, :] = jnp.zeros((pad, 2 * GW), f32)

    def fill(i, carry):
        r0 = pl.multiple_of(i * CHUNK, CHUNK)
        xpad_ref[pl.ds(pad + r0, CHUNK), :] = xbc_ref[pl.ds(r0, CHUNK), :]
        return carry

    lax.fori_loop(0, nc, fill, 0)

    for d in range(2):
        for h in range(H):
            if has_h0:
                st_scr[d, h] = h0_ref[d, h]
            else:
                st_scr[d, h] = jnp.zeros((P, N), f32)

    a_neg_r = -jnp.exp(alr_ref[...])
    a_neg_c = -jnp.exp(alc_ref[...])

    def decays(r0):
        dt_c = _softplus(dtc_ref[pl.ds(r0, CHUNK), :] + dbr_ref[...])
        dt_r = _softplus(dtr_ref[:, pl.ds(r0, CHUNK)] + dbc_ref[...])
        la_c = dt_c * a_neg_r
        la_r = dt_r * a_neg_c
        a_c = jnp.where(col < H, _exact_dot_left(lt01, la_c), _exact_dot_left(ut01, la_c))
        a_r = jnp.where(sub16 < H, _exact_dot_right(la_r, ut01), _exact_dot_right(la_r, lt01))
        return dt_c, dt_r, a_c, a_r

    def fwd(c, carry):
        r0 = pl.multiple_of(c * CHUNK, CHUNK)
        acc = jnp.broadcast_to(cb_ref[...], (CHUNK, 2 * GW))
        win = xpad_ref.at[pl.ds(r0, CHUNK + 2 * pad), :]
        off = pad - SSM_K // 2
        for k in range(SSM_K):
            acc = acc + win[off + k:off + k + CHUNK, :] * cw_ref[k:k + 1, :]
        xc = _silu(acc)
        xc_scr[pl.ds(r0, CHUNK), :] = xc
        dt_c, dt_r, a_c, a_r = decays(r0)
        tot_f = a_c[CHUNK - 1:CHUNK, :]
        ks_f = dt_c * jnp.exp(tot_f - a_c)
        es_f = jnp.exp(a_c)
        for g in range(SSM_GROUPS):
            bm = xc[:, GW + g * N:GW + (g + 1) * N]
            cm = xc[:, GW + SSM_GROUPS * N + g * N:GW + SSM_GROUPS * N + (g + 1) * N]
            bmb = bm.astype(bf16)
            cmb = cm.astype(bf16)
            gmat = _nt(cmb, bmb)
            for hh in range(H // SSM_GROUPS):
                h = g * (H // SSM_GROUPS) + hh
                xs = xc[:, h * P:(h + 1) * P]
                df = jnp.exp(jnp.minimum(a_c[:, h:h + 1] - a_r[h:h + 1, :], 0.0)) * dt_r[h:h + 1, :]
                db = jnp.exp(jnp.minimum(a_c[:, H + h:H + h + 1] - a_r[H + h:H + h + 1, :], 0.0)) \
                    * dt_r[H + h:H + h + 1, :]
                dm = jnp.where(col <= row, df, 0.0) + jnp.where(col >= row, db, 0.0)
                m = (gmat * dm).astype(bf16)
                y = _nn(m, xs.astype(bf16))
                y = y + (dsk_ref[0:1, h * P:(h + 1) * P] + dsk_ref[1:2, h * P:(h + 1) * P]) * xs
                st = st_scr[0, h]
                y = y + es_f[:, h:h + 1] * _nt(cmb, st.astype(bf16))
                cs = _tn((xs * ks_f[:, h:h + 1]).astype(bf16), bmb)
                st_scr[0, h] = st * jnp.exp(tot_f[:, h:h + 1]) + cs
                y_scr[pl.ds(r0, CHUNK), h * P:(h + 1) * P] = y
        return carry

    lax.fori_loop(0, nc, fwd, 0)

    def bwd(i, carry):
        r0 = pl.multiple_of((nc - 1 - i) * CHUNK, CHUNK)
        xc = xc_scr[pl.ds(r0, CHUNK), :]
        dt_c, dt_r, a_c, a_r = decays(r0)
        tot_b = a_c[0:1, :]
        ks_b = dt_c * jnp.exp(tot_b - a_c)
        es_b = jnp.exp(a_c)
        for g in range(SSM_GROUPS):
            bmb = xc[:, GW + g * N:GW + (g + 1) * N].astype(bf16)
            cmb = xc[:, GW + SSM_GROUPS * N + g * N:GW + SSM_GROUPS * N + (g + 1) * N].astype(bf16)
            for hh in range(H // SSM_GROUPS):
                h = g * (H // SSM_GROUPS) + hh
                xs = xc[:, h * P:(h + 1) * P]
                st = st_scr[1, h]
                y = y_scr[pl.ds(r0, CHUNK), h * P:(h + 1) * P] \
                    + es_b[:, H + h:H + h + 1] * _nt(cmb, st.astype(bf16))
                cs = _tn((xs * ks_b[:, H + h:H + h + 1]).astype(bf16), bmb)
                st_scr[1, h] = st * jnp.exp(tot_b[:, H + h:H + h + 1]) + cs
                y_scr[pl.ds(r0, CHUNK), h * P:(h + 1) * P] = y
        yz = y_scr[pl.ds(r0, CHUNK), :] * _silu(z_ref[pl.ds(r0, CHUNK), :])
        ms = jnp.mean(yz * yz, axis=-1, keepdims=True)
        o_ref[pl.ds(r0, CHUNK), :] = (yz * lax.rsqrt(ms + EPS) * ng_ref[...]).astype(o_ref.dtype)
        return carry

    lax.fori_loop(0, nc, bwd, 0)

    if want_state:
        for d in range(2):
            for h in range(H):
                hout_ref[d, h] = st_scr[d, h]


def _pad_lanes(v, n=128):
    return jnp.pad(v, ((0, 0), (0, n - v.shape[1])))


def _ssm(u, dt_t, p, h0, layer, nb, L, want_state):
    has_h0 = h0 is not None
    H = SSM_HEADS
    a_log = p['ssm_a_log'].reshape(1, 2 * H)
    dt_bias = p['ssm_dt_bias'].reshape(1, 2 * H)
    small = lambda shape: pl.BlockSpec(shape, lambda b: (0, 0))
    in_specs = [pl.BlockSpec((L, 2 * GW), lambda b: (b, C_XBC // (2 * GW))),
                pl.BlockSpec((L, GW), lambda b: (b, C_Z // GW)),
                pl.BlockSpec((L, 128), lambda b: (b, C_DT // 128)),
                pl.BlockSpec((2 * H, L), lambda b: (0, b)),
                small((SSM_K, 2 * GW)), small((1, 2 * GW)),
                small((1, 128)), small((2 * H, 128)), small((1, 128)), small((2 * H, 128)),
                small((2, GW)), small((1, GW))]
    args = [u, u, u, dt_t, p['ssm_conv_w'], p['ssm_conv_b'].reshape(1, 2 * GW),
            _pad_lanes(a_log), jnp.broadcast_to(a_log.reshape(2 * H, 1), (2 * H, 128)),
            _pad_lanes(dt_bias), jnp.broadcast_to(dt_bias.reshape(2 * H, 1), (2 * H, 128)),
            jnp.repeat(p['ssm_d'], SSM_P, axis=1), p['ssm_norm'].reshape(1, GW)]
    if has_h0:
        in_specs.append(pl.BlockSpec((None, None, 2, H, SSM_P, SSM_N), lambda b: (b, layer, 0, 0, 0, 0)))
        args.append(h0)
    out_specs = [pl.BlockSpec((L, GW), lambda b: (b, 0))]
    out_shape = [jax.ShapeDtypeStruct((nb * L, GW), bf16)]
    if want_state:
        out_specs.append(pl.BlockSpec((None, 2, H, SSM_P, SSM_N), lambda b: (b, 0, 0, 0, 0)))
        out_shape.append(jax.ShapeDtypeStruct((nb, 2, H, SSM_P, SSM_N), f32))
    res = pl.pallas_call(
        functools.partial(_ssm_kernel, L=L, has_h0=has_h0, want_state=want_state),
        grid=(nb,),
        in_specs=in_specs,
        out_specs=out_specs,
        out_shape=out_shape,
        scratch_shapes=[pltpu.VMEM((L + 16, 2 * GW), f32),
                        pltpu.VMEM((L, 2 * GW), f32),
                        pltpu.VMEM((L, GW), f32),
                        pltpu.VMEM((2, H, SSM_P, SSM_N), f32)],
        compiler_params=_cparams(("arbitrary",)),
        name="ssd_mixer",
    )(*args)
    return res if want_state else (res[0], None)


def _ctx_att_kernel(sink_ref, q_ref, k_ref, v_ref, o_ref, *, L):
    G = ATT_HEADS // ATT_KV
    scale = ATT_D ** -0.5
    for j in range(ATT_KV):
        kb = k_ref[:, j * ATT_D:(j + 1) * ATT_D].astype(bf16)
        vb = v_ref[:, j * ATT_D:(j + 1) * ATT_D].astype(bf16)
        for gg in range(G):
            h = j * G + gg
            qb = q_ref[:, h * ATT_D:(h + 1) * ATT_D].astype(bf16)
            s = _nt(qb, kb) * scale
            sink = sink_ref[h]
            m = jnp.maximum(jnp.max(s, axis=-1, keepdims=True), sink)
            p = jnp.exp(s - m)
            den = jnp.sum(p, axis=-1, keepdims=True) + jnp.exp(sink - m)
            o = _nn(p.astype(bf16), vb) / den
            o_ref[:, h * ATT_D:(h + 1) * ATT_D] = o.astype(o_ref.dtype)


def _ctx_attention(u, sink, nb, L):
    return pl.pallas_call(
        functools.partial(_ctx_att_kernel, L=L),
        grid=(nb,),
        in_specs=[pl.BlockSpec(memory_space=pltpu.SMEM),
                  pl.BlockSpec((L, GW), lambda b: (b, C_AQ // GW)),
                  pl.BlockSpec((L, 128), lambda b: (b, C_AK // 128)),
                  pl.BlockSpec((L, 128), lambda b: (b, C_AV // 128))],
        out_specs=pl.BlockSpec((L, GW), lambda b: (b, 0)),
        out_shape=jax.ShapeDtypeStruct((nb * L, GW), bf16),
        compiler_params=_cparams(("arbitrary",)),
        name="ctx_attention",
    )(sink, u, u, u)


def _rope(x, cos, sin):
    w = x.shape[1]
    lane = lax.broadcasted_iota(jnp.int32, x.shape, 1)
    first = (lane % 32) < 16
    rot = jnp.where(first, -pltpu.roll(x, w - 16, 1), pltpu.roll(x, 16, 1))
    return x * cos + rot * sin


def _half_variants(x, keep_fill):
    lane = lax.broadcasted_iota(jnp.int32, x.shape, 1)
    lo = lane < ATT_D
    a0 = jnp.where(lo, x, keep_fill)
    b1 = jnp.where(lo, keep_fill, x)
    xr = pltpu.roll(x, ATT_D, 1)
    a1 = jnp.where(lo, xr, keep_fill)
    b0 = jnp.where(lo, keep_fill, xr)
    return ((a0, b0), (a1, b1))


def _lat_att_kernel(sink_ref, q_ref, k_ref, v_ref, kc_ref, vc_ref, cos_ref, sin_ref, o_ref,
                    qr_scr, kv_scr, vv_scr, kcv_scr, vcv_scr, bias_scr, *, L):
    G = ATT_HEADS // ATT_KV
    B = CHUNK
    nb = L // B
    scale = ATT_D ** -0.5
    zeros = jnp.zeros((B, 128), bf16)
    for j in range(ATT_KV):
        for hf in range(2):
            kv_scr[j, hf, 0:B, :] = zeros
            kv_scr[j, hf, B + L:2 * B + L, :] = zeros
            vv_scr[j, hf, 0:B, :] = zeros
            vv_scr[j, hf, B + L:2 * B + L, :] = zeros
    kcv = _half_variants(kc_ref[...], 0.0)
    vcv = _half_variants(vc_ref[...], 1.0)
    for j in range(ATT_KV):
        for hf in range(2):
            kcv_scr[j, hf] = kcv[j][hf].astype(bf16)
            vcv_scr[j, hf] = vcv[j][hf].astype(bf16)

    rowi = lax.broadcasted_iota(jnp.int32, (2 * B, 3 * B), 0) % B
    coli = lax.broadcasted_iota(jnp.int32, (2 * B, 3 * B), 1)
    inwin = jnp.abs(rowi - (coli - B)) <= ATT_WIN
    ninf = jnp.float32(-jnp.inf)
    bias_scr[0] = jnp.where(inwin & (coli >= B), 0.0, ninf)
    bias_scr[1] = jnp.where(inwin, 0.0, ninf)
    bias_scr[2] = jnp.where(inwin & (coli < 2 * B), 0.0, ninf)

    def prep(n, carry):
        r0 = pl.multiple_of(n * B, B)
        cos = cos_ref[pl.ds(r0, B), :]
        sin = sin_ref[pl.ds(r0, B), :]
        cos4 = jnp.concatenate([cos] * 4, axis=1)
        sin4 = jnp.concatenate([sin] * 4, axis=1)
        qr_scr[pl.ds(r0, B), :] = (_rope(q_ref[pl.ds(r0, B), :], cos4, sin4) * scale).astype(bf16)
        kvar = _half_variants(_rope(k_ref[pl.ds(r0, B), :], cos, sin), 0.0)
        vvar = _half_variants(v_ref[pl.ds(r0, B), :], 1.0)
        for j in range(ATT_KV):
            for hf in range(2):
                kv_scr[j, hf, pl.ds(B + r0, B), :] = kvar[j][hf].astype(bf16)
                vv_scr[j, hf, pl.ds(B + r0, B), :] = vvar[j][hf].astype(bf16)
        return carry

    lax.fori_loop(0, nb, prep, 0)

    lane_lo = lax.broadcasted_iota(jnp.int32, (2 * B, 128), 1) < ATT_D

    def blk(n, carry):
        r0 = pl.multiple_of(n * B, B)
        bias = bias_scr[jnp.where(n == 0, 0, jnp.where(n == nb - 1, 2, 1))]
        for j in range(ATT_KV):
            qs = jnp.concatenate([qr_scr[pl.ds(r0, B), (2 * j) * 128:(2 * j + 1) * 128],
                                  qr_scr[pl.ds(r0, B), (2 * j + 1) * 128:(2 * j + 2) * 128]], axis=0)
            outs = []
            for hf in range(2):
                s_c = _nt(qs, kcv_scr[j, hf])
                s_b = _nt(qs, kv_scr[j, hf, pl.ds(r0, 3 * B), :]) + bias
                sink = jnp.concatenate([jnp.full((B, 1), sink_ref[G * j + hf], f32),
                                        jnp.full((B, 1), sink_ref[G * j + 2 + hf], f32)], axis=0)
                m = jnp.maximum(jnp.maximum(jnp.max(s_c, axis=-1, keepdims=True),
                                            jnp.max(s_b, axis=-1, keepdims=True)), sink)
                p_c = jnp.exp(s_c - m).astype(bf16)
                p_b = jnp.exp(s_b - m).astype(bf16)
                oe = _nn(p_c, vcv_scr[j, hf]) + _nn(p_b, vv_scr[j, hf, pl.ds(r0, 3 * B), :])
                den = oe[:, (1 - hf) * ATT_D:(1 - hf) * ATT_D + 1] + jnp.exp(sink - m)
                outs.append(oe / den)
            o = jnp.where(lane_lo, outs[0], outs[1])
            o_ref[pl.ds(r0, B), (2 * j) * 128:(2 * j + 1) * 128] = o[0:B].astype(o_ref.dtype)
            o_ref[pl.ds(r0, B), (2 * j + 1) * 128:(2 * j + 2) * 128] = o[B:2 * B].astype(o_ref.dtype)
        return carry

    lax.fori_loop(0, nb, blk, 0, unroll=2)


def _rope_tables(L):
    pos = jnp.arange(L)
    rows = (pos // GRID_W).astype(f32)
    cols = (pos % GRID_W).astype(f32)
    half = ATT_D // 4
    freqs = ROPE_THETA ** (-jnp.arange(half, dtype=f32) / half)
    ang_r = rows[:, None] * freqs
    ang_c = cols[:, None] * freqs
    ang = jnp.concatenate([ang_r, ang_r, ang_c, ang_c], axis=1)
    ang = jnp.concatenate([ang, ang], axis=1)
    return jnp.cos(ang), jnp.sin(ang)


def _lat_attention(u, k_cache, v_cache, sink, layer, nb, L):
    Lc = k_cache.shape[2]
    cos, sin = _rope_tables(L)
    kc = k_cache.reshape(nb, DEPTH, Lc, ATT_KV * ATT_D)
    vc = v_cache.reshape(nb, DEPTH, Lc, ATT_KV * ATT_D)
    cspec = pl.BlockSpec((None, None, Lc, 128), lambda b: (b, layer, 0, 0))
    return pl.pallas_call(
        functools.partial(_lat_att_kernel, L=L),
        grid=(nb,),
        in_specs=[pl.BlockSpec(memory_space=pltpu.SMEM),
                  pl.BlockSpec((L, GW), lambda b: (b, C_AQ // GW)),
                  pl.BlockSpec((L, 128), lambda b: (b, C_AK // 128)),
                  pl.BlockSpec((L, 128), lambda b: (b, C_AV // 128)),
                  cspec, cspec,
                  pl.BlockSpec((L, 128), lambda b: (0, 0)),
                  pl.BlockSpec((L, 128), lambda b: (0, 0))],
        out_specs=pl.BlockSpec((L, GW), lambda b: (b, 0)),
        out_shape=jax.ShapeDtypeStruct((nb * L, GW), bf16),
        scratch_shapes=[pltpu.VMEM((L, GW), bf16),
                        pltpu.VMEM((ATT_KV, 2, L + 2 * CHUNK, 128), bf16),
                        pltpu.VMEM((ATT_KV, 2, L + 2 * CHUNK, 128), bf16),
                        pltpu.VMEM((ATT_KV, 2, Lc, 128), bf16),
                        pltpu.VMEM((ATT_KV, 2, Lc, 128), bf16),
                        pltpu.VMEM((3, 2 * CHUNK, 3 * CHUNK), f32)],
        compiler_params=_cparams(("arbitrary",)),
        name="lat_attention",
    )(sink, u, u, u, kc, vc, cos, sin)


def _permute_w_in(w):
    z = jnp.zeros((DEPTH, D, 128 - 2 * SSM_HEADS), bf16)
    wb = w.astype(bf16)
    return jnp.concatenate([wb[:, :, 0:1024], wb[:, :, 1536:2560], wb[:, :, 1024:1536],
                            wb[:, :, 2576:5392], wb[:, :, 2560:2576], z], axis=2)


def _layer(x, mod, p, ctx, layer, nb, L, final):
    per_seq = ctx is not None
    u = _inproj(x, p['norm_mix'], mod, p['w_in_p'], layer, L, per_seq)
    dt_t = u[:, C_DT:C_DT + 2 * SSM_HEADS].T
    want_state = ctx is None
    o_conv = _conv_module(u, p['conv_w'], p['conv_b'], p['conv_ln_g'], p['conv_ln_b'], nb, L)
    if ctx is None:
        o_ssm, h_ssm = _ssm(u, dt_t, p, None, layer, nb, L, True)
        o_ret, h_ret = _retention(u, p['ret_log_decay'], p['ret_gn_g'], None, layer, nb, L, True)
        o_att = _ctx_attention(u, p['att_sink'], nb, L)
    else:
        k_c, v_c, s_ssm, s_ret = ctx
        o_ssm, h_ssm = _ssm(u, dt_t, p, s_ssm, layer, nb, L, False)
        o_ret, h_ret = _retention(u, p['ret_log_decay'], p['ret_gn_g'], s_ret, layer, nb, L, False)
        o_att = _lat_attention(u, k_c, v_c, p['att_sink'], layer, nb, L)
    x1 = _outproj(o_conv, o_ssm, o_ret, o_att, p['w_out_b'], layer, x, mod, L, per_seq)
    x2 = _mlp(x1, p['norm_mlp'], mod, p['final_norm'], p['w1_b'], p['w2_b'], layer, L, per_seq, final)
    states = None
    if want_state:
        ak = u[:, C_AK:C_AK + 128].reshape(nb, L, ATT_KV, ATT_D)
        av = u[:, C_AV:C_AV + 128].reshape(nb, L, ATT_KV, ATT_D)
        states = (ak, av, h_ssm, h_ret)
    return x2, states


def kernel(x_prompt, x_sample, cache_attn_k, cache_attn_v, state_ssm, state_ret, c, c_ctx, ada_w, ada_b, norm_mix, norm_mlp, w_in, conv_w, conv_b, conv_ln_g, conv_ln_b, ssm_conv_w, ssm_conv_b, ssm_a_log, ssm_dt_bias, ssm_d, ssm_norm, ret_log_decay, ret_gn_g, att_sink, w_out, w1, w2, final_norm):
    nbp, Lp, _ = x_prompt.shape
    nbs, Ls, _ = x_sample.shape
    cvec = jnp.concatenate([c_ctx[None, :], c, jnp.zeros((16 - 1 - nbs, D), f32)], axis=0)
    mod = _ada_mod(cvec, ada_w, ada_b)
    y_p = x_prompt.reshape(nbp * Lp, D)
    y_s = x_sample.reshape(nbs * Ls, D)
    new_k, new_v, new_ssm, new_ret = [], [], [], []
    w_in_p = _permute_w_in(w_in)
    w_out_b = w_out.astype(bf16)
    w1_b = w1.astype(bf16)
    w2_b = w2.astype(bf16)
    for l in range(DEPTH):
        p = dict(norm_mix=norm_mix[l], norm_mlp=norm_mlp[l], w_in_p=w_in_p,
                 conv_w=conv_w[l], conv_b=conv_b[l], conv_ln_g=conv_ln_g[l], conv_ln_b=conv_ln_b[l],
                 ssm_conv_w=ssm_conv_w[l], ssm_conv_b=ssm_conv_b[l], ssm_a_log=ssm_a_log[l],
                 ssm_dt_bias=ssm_dt_bias[l], ssm_d=ssm_d[l], ssm_norm=ssm_norm[l],
                 ret_log_decay=ret_log_decay[l], ret_gn_g=ret_gn_g[l], att_sink=att_sink[l],
                 w_out_b=w_out_b, w1_b=w1_b, w2_b=w2_b, final_norm=final_norm)
        mod_l = mod[l].reshape(16, 6, 1, D)
        final = l == DEPTH - 1
        y_p, (k_l, v_l, hs_l, hr_l) = _layer(y_p, mod_l[0:1], p, None, l, nbp, Lp, final)
        new_k.append(k_l)
        new_v.append(v_l)
        new_ssm.append(hs_l)
        new_ret.append(hr_l)
        ctx = (cache_attn_k, cache_attn_v, state_ssm, state_ret)
        y_s, _ = _layer(y_s, mod_l[1:1 + nbs], p, ctx, l, nbs, Ls, final)
    return (y_p.reshape(nbp, Lp, D), y_s.reshape(nbs, Ls, D),
            jnp.stack(new_k, axis=1), jnp.stack(new_v, axis=1),
            jnp.stack(new_ssm, axis=1), jnp.stack(new_ret, axis=1))
```

```python
import functools

import jax
import jax.numpy as jnp
from jax import lax
from jax.experimental import pallas as pl
from jax.experimental.pallas import tpu as pltpu

f32 = jnp.float32
bf16 = jnp.bfloat16

D = 2048
DEPTH = 2
GW = 512
CONV_K = 31
SSM_HEADS, SSM_P, SSM_N, SSM_GROUPS, SSM_K = 8, 64, 128, 2, 5
RET_HEADS, RET_D = 4, 128
ATT_HEADS, ATT_KV, ATT_D, ATT_WIN = 8, 2, 64, 128
GRID_W = 64
ROPE_THETA = 10000.0
CHUNK = 128
D_FF = 4 * D
EPS = 1e-6

C_CONV, C_XBC, C_Z, C_RQ, C_RK, C_RV, C_RG, C_AQ, C_AK, C_AV, C_DT = (
    0, 1024, 2048, 2560, 3072, 3584, 4096, 4608, 5120, 5248, 5376)
UW = 5504
N_IN = 5392
N_CHUNK = 512

VMEM_LIMIT = 56 * 1024 * 1024


def _cparams(sem):
    return pltpu.CompilerParams(dimension_semantics=sem, vmem_limit_bytes=VMEM_LIMIT)


def _sigmoid(x):
    return 1.0 / (1.0 + jnp.exp(-x))


def _silu(x):
    return x * _sigmoid(x)


def _nt(a, b):
    return lax.dot_general(a, b, (((1,), (1,)), ((), ())), preferred_element_type=f32)


def _tn(a, b):
    return lax.dot_general(a, b, (((0,), (0,)), ((), ())), preferred_element_type=f32)


def _nn(a, b):
    return jnp.dot(a, b, preferred_element_type=f32)


def _split3(x):
    hi = x.astype(bf16)
    r1 = x - hi.astype(f32)
    mid = r1.astype(bf16)
    lo = (r1 - mid.astype(f32)).astype(bf16)
    return hi, mid, lo


def _ada_kernel(c_ref, w_ref, b_ref, o_ref):
    c = c_ref[...]
    s = _silu(c).astype(bf16)
    o_ref[...] = _nn(s, w_ref[...].astype(bf16)) + b_ref[...]


def _ada_mod(cvec, ada_w, ada_b):
    tn = 1024
    return pl.pallas_call(
        _ada_kernel,
        grid=(DEPTH, 6 * D // tn),
        in_specs=[pl.BlockSpec((16, D), lambda l, j: (0, 0)),
                  pl.BlockSpec((None, D, tn), lambda l, j: (l, 0, j)),
                  pl.BlockSpec((None, 1, tn), lambda l, j: (l, 0, j))],
        out_specs=pl.BlockSpec((None, 16, tn), lambda l, j: (l, 0, j)),
        out_shape=jax.ShapeDtypeStruct((DEPTH, 16, 6 * D), f32),
        compiler_params=_cparams(("arbitrary", "arbitrary")),
        name="ada_mod",
    )(cvec, ada_w, ada_b.reshape(DEPTH, 1, 6 * D))


def _mod_spec(idx, tm, L, per_seq):
    if per_seq:
        return pl.BlockSpec((None, None, 1, D), lambda i, *_: ((i * tm) // L, idx, 0, 0))
    return pl.BlockSpec((None, None, 1, D), lambda i, *_: (0, idx, 0, 0))


def _inproj_kernel(x_ref, g_ref, sc_ref, sh_ref, w_ref, u_ref):
    x = x_ref[...]
    ms = jnp.mean(x * x, axis=-1, keepdims=True)
    h = x * lax.rsqrt(ms + EPS) * g_ref[...] * (1.0 + sc_ref[...]) + sh_ref[...]
    hb = h.astype(bf16)
    for n0 in range(0, UW, N_CHUNK):
        n1 = min(n0 + N_CHUNK, UW)
        u_ref[:, n0:n1] = _nn(hb, w_ref[:, n0:n1])


def _inproj(x, norm_g, mod, w_in_p, layer, L, per_seq):
    T = x.shape[0]
    tm = 256
    return pl.pallas_call(
        _inproj_kernel,
        grid=(T // tm,),
        in_specs=[pl.BlockSpec((tm, D), lambda i: (i, 0)),
                  pl.BlockSpec((1, D), lambda i: (0, 0)),
                  _mod_spec(1, tm, L, per_seq),
                  _mod_spec(0, tm, L, per_seq),
                  pl.BlockSpec((None, D, UW), lambda i: (layer, 0, 0), pipeline_mode=pl.Buffered(1))],
        out_specs=pl.BlockSpec((tm, UW), lambda i: (i, 0)),
        out_shape=jax.ShapeDtypeStruct((T, UW), f32),
        compiler_params=_cparams(("arbitrary",)),
        name="in_proj",
    )(x, norm_g.reshape(1, D), mod, mod, w_in_p)


def _outproj_kernel(oc_ref, os_ref, or_ref, oa_ref, w_ref, x_ref, g_ref, o_ref):
    for n0 in range(0, D, N_CHUNK):
        n1 = n0 + N_CHUNK
        acc = _nn(oc_ref[...], w_ref[0:GW, n0:n1])
        acc = acc + _nn(os_ref[...], w_ref[GW:2 * GW, n0:n1])
        acc = acc + _nn(or_ref[...], w_ref[2 * GW:3 * GW, n0:n1])
        acc = acc + _nn(oa_ref[...], w_ref[3 * GW:4 * GW, n0:n1])
        o_ref[:, n0:n1] = x_ref[:, n0:n1] + g_ref[:, n0:n1] * acc


def _outproj(o_conv, o_ssm, o_ret, o_att, w_out_b, layer, x, mod, L, per_seq):
    T = x.shape[0]
    tm = 512
    ospec = pl.BlockSpec((tm, GW), lambda i: (i, 0))
    return pl.pallas_call(
        _outproj_kernel,
        grid=(T // tm,),
        in_specs=[ospec, ospec, ospec, ospec,
                  pl.BlockSpec((None, D, D), lambda i: (layer, 0, 0), pipeline_mode=pl.Buffered(1)),
                  pl.BlockSpec((tm, D), lambda i: (i, 0)),
                  _mod_spec(2, tm, L, per_seq)],
        out_specs=pl.BlockSpec((tm, D), lambda i: (i, 0)),
        out_shape=jax.ShapeDtypeStruct((T, D), f32),
        compiler_params=_cparams(("arbitrary",)),
        name="out_proj",
    )(o_conv, o_ssm, o_ret, o_att, w_out_b, x, mod)


def _mlp_kernel(x_ref, g_ref, sc_ref, sh_ref, g2_ref, fn_ref, w1_ref, w2_ref, o_ref, h_scr, *, final):
    f = pl.program_id(1)
    tm = x_ref.shape[0]
    RB = 256

    @pl.when(f == 0)
    def _():
        for r in range(0, tm, RB):
            x = x_ref[r:r + RB, :]
            ms = jnp.mean(x * x, axis=-1, keepdims=True)
            h = x * lax.rsqrt(ms + EPS) * g_ref[...] * (1.0 + sc_ref[...]) + sh_ref[...]
            h_scr[r:r + RB, :] = h.astype(bf16)
        o_ref[...] = jnp.zeros_like(o_ref)

    for r in range(0, tm, 2 * RB):
        a = _nn(h_scr[r:r + 2 * RB, :], w1_ref[...])
        a = jnp.maximum(a, 0.0)
        a = (a * a).astype(bf16)
        o_ref[r:r + 2 * RB, :] += _nn(a, w2_ref[...])

    @pl.when(f == pl.num_programs(1) - 1)
    def _():
        for r in range(0, tm, RB):
            y = x_ref[r:r + RB, :] + g2_ref[...] * o_ref[r:r + RB, :]
            if final:
                ms = jnp.mean(y * y, axis=-1, keepdims=True)
                y = y * lax.rsqrt(ms + EPS) * fn_ref[...]
            o_ref[r:r + RB, :] = y


def _mlp(x, norm_g, mod, final_norm, w1_b, w2_b, layer, L, per_seq, final):
    T = x.shape[0]
    tm, tf = 1024, 512
    return pl.pallas_call(
        functools.partial(_mlp_kernel, final=final),
        grid=(T // tm, D_FF // tf),
        in_specs=[pl.BlockSpec((tm, D), lambda i, f: (i, 0), pipeline_mode=pl.Buffered(1)),
                  pl.BlockSpec((1, D), lambda i, f: (0, 0)),
                  _mod_spec(4, tm, L, per_seq),
                  _mod_spec(3, tm, L, per_seq),
                  _mod_spec(5, tm, L, per_seq),
                  pl.BlockSpec((1, D), lambda i, f: (0, 0)),
                  pl.BlockSpec((None, D, tf), lambda i, f: (layer, 0, f)),
                  pl.BlockSpec((None, tf, D), lambda i, f: (layer, f, 0))],
        out_specs=pl.BlockSpec((tm, D), lambda i, f: (i, 0)),
        out_shape=jax.ShapeDtypeStruct((T, D), f32),
        scratch_shapes=[pltpu.VMEM((tm, D), bf16)],
        compiler_params=_cparams(("arbitrary", "arbitrary")),
        name="mlp",
    )(x, norm_g.reshape(1, D), mod, mod, mod, final_norm.reshape(1, D), w1_b, w2_b)


def _conv_kernel(u_ref, w_ref, b_ref, lg_ref, lb_ref, o_ref, vpad_ref, ph_ref, *, L):
    RC = 64
    pad = 16
    SUB = 8
    off0 = pad - CONV_K // 2
    nph = RC + SUB * ((off0 + CONV_K - 1) // SUB)
    vpad_ref[0:pad, :] = jnp.zeros((pad, GW), f32)
    vpad_ref[pad + L:2 * pad + L, :] = jnp.zeros((pad, GW), f32)

    def fill(i, carry):
        r0 = pl.multiple_of(i * RC, RC)
        a = u_ref[pl.ds(r0, RC), 0:GW]
        g = u_ref[pl.ds(r0, RC), GW:2 * GW]
        vpad_ref[pl.ds(pad + r0, RC), :] = a * _sigmoid(g)
        return carry

    lax.fori_loop(0, L // RC, fill, 0)

    def body(i, carry):
        r0 = pl.multiple_of(i * RC, RC)
        acc = jnp.broadcast_to(b_ref[...], (RC, GW))
        win = vpad_ref.at[pl.ds(r0, RC + 2 * pad), :]
        for ph in range(SUB):
            ph_ref[ph] = win[ph:ph + nph, :]
        for k in range(CONV_K):
            a, ph = divmod(off0 + k, SUB)
            acc = acc + ph_ref[ph, SUB * a:SUB * a + RC, :] * w_ref[k:k + 1, :]
        mu = jnp.mean(acc, axis=-1, keepdims=True)
        xc = acc - mu
        var = jnp.mean(xc * xc, axis=-1, keepdims=True)
        y = xc * lax.rsqrt(var + EPS) * lg_ref[...] + lb_ref[...]
        o_ref[pl.ds(r0, RC), :] = _silu(y).astype(o_ref.dtype)
        return carry

    lax.fori_loop(0, L // RC, body, 0)


def _conv_module(u, conv_w, conv_b, ln_g, ln_b, nb, L):
    return pl.pallas_call(
        functools.partial(_conv_kernel, L=L),
        grid=(nb,),
        in_specs=[pl.BlockSpec((L, 2 * GW), lambda b: (b, C_CONV // (2 * GW))),
                  pl.BlockSpec((CONV_K, GW), lambda b: (0, 0)),
                  pl.BlockSpec((1, GW), lambda b: (0, 0)),
                  pl.BlockSpec((1, GW), lambda b: (0, 0)),
                  pl.BlockSpec((1, GW), lambda b: (0, 0))],
        out_specs=pl.BlockSpec((L, GW), lambda b: (b, 0)),
        out_shape=jax.ShapeDtypeStruct((nb * L, GW), bf16),
        scratch_shapes=[pltpu.VMEM((L + 32, GW), f32), pltpu.VMEM((8, 64 + 24, GW), f32)],
        compiler_params=_cparams(("arbitrary",)),
        name="conv_module",
    )(u, conv_w, conv_b.reshape(1, GW), ln_g.reshape(1, GW), ln_b.reshape(1, GW))


def _ret_kernel(*refs, L, has_h0, want_state):
    ld_ref, q_ref, k_ref, v_ref, g_ref, gn_ref = refs[:6]
    pos = 6
    h0_ref = None
    if has_h0:
        h0_ref = refs[pos]
        pos += 1
    o_ref = refs[pos]
    pos += 1
    hout_ref = None
    if want_state:
        hout_ref = refs[pos]
        pos += 1
    y_scr, st_scr, dec_scr, cs_scr = refs[pos:pos + 4]

    nc = L // CHUNK
    H = RET_HEADS
    row = lax.broadcasted_iota(jnp.int32, (CHUNK, CHUNK), 0)
    col = lax.broadcasted_iota(jnp.int32, (CHUNK, CHUNK), 1)
    rowf = row.astype(f32)
    diff = (row - col).astype(f32)
    kscale = RET_D ** -0.5
    sls = [slice(h * RET_D, (h + 1) * RET_D) for h in range(H)]

    for d in range(2):
        for h in range(H):
            if has_h0:
                st_scr[d, h] = h0_ref[d, h]
            else:
                st_scr[d, h] = jnp.zeros((RET_D, RET_D), f32)
    for h in range(H):
        laf = ld_ref[h]
        lab = ld_ref[H + h]
        dec_scr[h] = jnp.where(col < row, jnp.exp(diff * laf),
                               jnp.where(col > row, jnp.exp(-diff * lab), 2.0))


    def intra(c, carry):
        r0 = pl.multiple_of(c * CHUNK, CHUNK)
        qb = [q_ref[pl.ds(r0, CHUNK), sls[h]].astype(bf16) for h in range(H)]
        kb = [(k_ref[pl.ds(r0, CHUNK), sls[h]] * kscale).astype(bf16) for h in range(H)]
        vh = [v_ref[pl.ds(r0, CHUNK), sls[h]] for h in range(H)]
        s = [_nt(qb[h], kb[h]) for h in range(H)]
        cf = [_tn((vh[h] * jnp.exp((CHUNK - 1.0 - rowf) * ld_ref[h])).astype(bf16), kb[h])
              for h in range(H)]
        cb = [_tn((vh[h] * jnp.exp(rowf * ld_ref[H + h])).astype(bf16), kb[h]) for h in range(H)]
        m = [(s[h] * dec_scr[h]).astype(bf16) for h in range(H)]
        y = [_nn(m[h], vh[h].astype(bf16)) for h in range(H)]
        for h in range(H):
            y_scr[pl.ds(r0, CHUNK), sls[h]] = y[h]
            cs_scr[0, c, h] = cf[h]
            cs_scr[1, c, h] = cb[h]
        return carry

    lax.fori_loop(0, nc, intra, 0, unroll=2)

    def inter(d, c):
        r0 = pl.multiple_of(c * CHUNK, CHUNK)
        st = [st_scr[d, h] for h in range(H)]
        yi = [_nt(q_ref[pl.ds(r0, CHUNK), sls[h]].astype(bf16), st[h].astype(bf16)) for h in range(H)]
        for h in range(H):
            la = ld_ref[d * H + h]
            st_scr[d, h] = st[h] * jnp.exp(jnp.full((RET_D, RET_D), CHUNK * la, f32)) + cs_scr[d, c, h]
        return r0, yi

    def fwd(c, carry):
        r0, yi = inter(0, c)
        for h in range(H):
            y_scr[pl.ds(r0, CHUNK), sls[h]] += jnp.exp((rowf + 1.0) * ld_ref[h]) * yi[h]
        return carry

    lax.fori_loop(0, nc, fwd, 0, unroll=2)

    def bwd(i, carry):
        r0, yi = inter(1, nc - 1 - i)
        for h in range(H):
            sl = sls[h]
            y = y_scr[pl.ds(r0, CHUNK), sl] + jnp.exp((CHUNK - rowf) * ld_ref[H + h]) * yi[h]
            mu = jnp.mean(y, axis=-1, keepdims=True)
            yc = y - mu
            var = jnp.mean(yc * yc, axis=-1, keepdims=True)
            yn = yc * lax.rsqrt(var + EPS) * gn_ref[:, sl]
            g = g_ref[pl.ds(r0, CHUNK), sl]
            o_ref[pl.ds(r0, CHUNK), sl] = (_silu(g) * yn).astype(o_ref.dtype)
        return carry

    lax.fori_loop(0, nc, bwd, 0, unroll=2)

    if want_state:
        for d in range(2):
            for h in range(RET_HEADS):
                hout_ref[d, h] = st_scr[d, h]


def _retention(u, log_decay, gn_g, h0, layer, nb, L, want_state):
    has_h0 = h0 is not None
    cblk = lambda c: pl.BlockSpec((L, GW), lambda b, c=c: (b, c // GW))
    in_specs = [pl.BlockSpec(memory_space=pltpu.SMEM),
                cblk(C_RQ), cblk(C_RK), cblk(C_RV), cblk(C_RG),
                pl.BlockSpec((1, GW), lambda b: (0, 0))]
    args = [log_decay.reshape(2 * RET_HEADS), u, u, u, u, gn_g.reshape(1, GW)]
    if has_h0:
        in_specs.append(pl.BlockSpec((None, None, 2, RET_HEADS, RET_D, RET_D),
                                     lambda b: (b, layer, 0, 0, 0, 0)))
        args.append(h0)
    out_specs = [pl.BlockSpec((L, GW), lambda b: (b, 0))]
    out_shape = [jax.ShapeDtypeStruct((nb * L, GW), bf16)]
    if want_state:
        out_specs.append(pl.BlockSpec((None, 2, RET_HEADS, RET_D, RET_D), lambda b: (b, 0, 0, 0, 0)))
        out_shape.append(jax.ShapeDtypeStruct((nb, 2, RET_HEADS, RET_D, RET_D), f32))
    res = pl.pallas_call(
        functools.partial(_ret_kernel, L=L, has_h0=has_h0, want_state=want_state),
        grid=(nb,),
        in_specs=in_specs,
        out_specs=out_specs,
        out_shape=out_shape,
        scratch_shapes=[pltpu.VMEM((L, GW), f32),
                        pltpu.VMEM((2, RET_HEADS, RET_D, RET_D), f32),
                        pltpu.VMEM((RET_HEADS, CHUNK, CHUNK), f32),
                        pltpu.VMEM((2, L // CHUNK, RET_HEADS, RET_D, RET_D), f32)],
        compiler_params=_cparams(("arbitrary",)),
        name="retention",
    )(*args)
    return res if want_state else (res[0], None)


def _softplus(x):
    return jnp.maximum(x, 0.0) + jnp.log(1.0 + jnp.exp(-jnp.abs(x)))


def _ssm_kernel(*refs, L, has_h0, want_state):
    (xbc_ref, z_ref, dtc_ref, dtr_ref, cw_ref, cb_ref, alr_ref, alc_ref, dbr_ref, dbc_ref,
     dsk_ref, ng_ref) = refs[:12]
    pos = 12
    h0_ref = None
    if has_h0:
        h0_ref = refs[pos]
        pos += 1
    o_ref = refs[pos]
    pos += 1
    hout_ref = None
    if want_state:
        hout_ref = refs[pos]
        pos += 1
    xpad_ref, xc_scr, y_scr, st_scr, es_scr, cs_scr, tot_scr = refs[pos:pos + 7]

    nc = L // CHUNK
    H, P, N = SSM_HEADS, SSM_P, SSM_N
    HG = H // SSM_GROUPS
    pad = 8
    row = lax.broadcasted_iota(jnp.int32, (CHUNK, CHUNK), 0)
    col = lax.broadcasted_iota(jnp.int32, (CHUNK, CHUNK), 1)
    lt01 = (row >= col).astype(bf16)
    ut01 = (row <= col).astype(bf16)
    sub16 = lax.broadcasted_iota(jnp.int32, (2 * H, CHUNK), 0)
    lane_lo = col < P

    xpad_ref[0:pad, :] = jnp.zeros((pad, 2 * GW), f32)
    xpad_ref[pad + L:2 * pad + L, :] = jnp.zeros((pad, 2 * GW), f32)

    def fill(i, carry):
        r0 = pl.multiple_of(i * CHUNK, CHUNK)
        xpad_ref[pl.ds(pad + r0, CHUNK), :] = xbc_ref[pl.ds(r0, CHUNK), :]
        return carry

    lax.fori_loop(0, nc, fill, 0)

    for d in range(2):
        for h in range(H):
            g, hh = divmod(h, HG)
            if has_h0:
                st_scr[d, g, hh * P:(hh + 1) * P, :] = h0_ref[d, h]
            else:
                st_scr[d, g, hh * P:(hh + 1) * P, :] = jnp.zeros((P, N), f32)

    a_neg_r = -jnp.exp(alr_ref[...])
    a_neg_c = -jnp.exp(alc_ref[...])

    def per_head_lanes(v, base):
        cols = []
        for c2 in range(H // 2):
            a = jnp.broadcast_to(v[:, base + 2 * c2:base + 2 * c2 + 1], (CHUNK, 2 * P))
            b = jnp.broadcast_to(v[:, base + 2 * c2 + 1:base + 2 * c2 + 2], (CHUNK, 2 * P))
            cols.append(jnp.where(lane_lo, a, b))
        return jnp.concatenate(cols, axis=1)

    def grp(x, g, base):
        return x[:, base + g * N:base + (g + 1) * N]


    def intra(c, carry):
        r0 = pl.multiple_of(c * CHUNK, CHUNK)
        acc = jnp.broadcast_to(cb_ref[...], (CHUNK, 2 * GW))
        win = xpad_ref.at[pl.ds(r0, CHUNK + 2 * pad), :]
        off = pad - SSM_K // 2
        for k in range(SSM_K):
            acc = acc + win[off + k:off + k + CHUNK, :] * cw_ref[k:k + 1, :]
        xc = _silu(acc)
        xc_scr[pl.ds(r0, CHUNK), :] = xc
        xs = xc[:, 0:GW]
        bmb = [grp(xc, g, GW).astype(bf16) for g in range(SSM_GROUPS)]
        cmb = [grp(xc, g, GW + SSM_GROUPS * N).astype(bf16) for g in range(SSM_GROUPS)]
        gmat = [_nt(cmb[g], bmb[g]) for g in range(SSM_GROUPS)]

        dt_c = _softplus(dtc_ref[pl.ds(r0, CHUNK), :] + dbr_ref[...])
        dt_r = _softplus(dtr_ref[:, pl.ds(r0, CHUNK)] + dbc_ref[...])
        lc3 = _split3(dt_c * a_neg_r)
        lr3 = _split3(dt_r * a_neg_c)
        pc = [_nn(lt01, t) for t in lc3]
        sc = [_nn(ut01, t) for t in lc3]
        pr = [_nn(t, ut01) for t in lr3]
        sr = [_nn(t, lt01) for t in lr3]
        a_c = jnp.where(col < H, pc[0] + pc[1] + pc[2], sc[0] + sc[1] + sc[2])
        a_r = jnp.where(sub16 < H, pr[0] + pr[1] + pr[2], sr[0] + sr[1] + sr[2])
        tot = jnp.where(col[0:1, :] < H, a_c[CHUNK - 1:CHUNK, :], a_c[0:1, :])
        tot_scr[c] = jnp.broadcast_to(tot, (8, 128))
        ks = dt_c * jnp.exp(tot - a_c)
        es = jnp.exp(a_c)
        es_scr[0, pl.ds(r0, CHUNK), :] = per_head_lanes(es, 0)
        es_scr[1, pl.ds(r0, CHUNK), :] = per_head_lanes(es, H)
        xwf = (xs * per_head_lanes(ks, 0)).astype(bf16)
        xwb = (xs * per_head_lanes(ks, H)).astype(bf16)
        csf = [_tn(xwf[:, g * HG * P:(g + 1) * HG * P], bmb[g]) for g in range(SSM_GROUPS)]
        csb = [_tn(xwb[:, g * HG * P:(g + 1) * HG * P], bmb[g]) for g in range(SSM_GROUPS)]
        for g in range(SSM_GROUPS):
            cs_scr[0, c, g] = csf[g]
            cs_scr[1, c, g] = csb[g]

        ms = []
        for h in range(H):
            df = jnp.exp(jnp.minimum(a_c[:, h:h + 1] - a_r[h:h + 1, :], 0.0)) * dt_r[h:h + 1, :]
            db = jnp.exp(jnp.minimum(a_c[:, H + h:H + h + 1] - a_r[H + h:H + h + 1, :], 0.0)) \
                * dt_r[H + h:H + h + 1, :]
            dm = jnp.where(col <= row, df, 0.0) + jnp.where(col >= row, db, 0.0)
            ms.append((gmat[h // HG] * dm).astype(bf16))
        ys = []
        for c2 in range(H // 2):
            xcol = xs[:, c2 * 2 * P:(c2 + 1) * 2 * P]
            x_lo = jnp.where(lane_lo, xcol, 0.0).astype(bf16)
            x_hi = jnp.where(lane_lo, 0.0, xcol).astype(bf16)
            ys.append(_nn(ms[2 * c2], x_lo) + _nn(ms[2 * c2 + 1], x_hi))
        y = jnp.concatenate(ys, axis=1)
        y_scr[pl.ds(r0, CHUNK), :] = y + (dsk_ref[0:1, :] + dsk_ref[1:2, :]) * xs
        return carry

    lax.fori_loop(0, nc, intra, 0, unroll=2)

    def inter(d, c):
        r0 = pl.multiple_of(c * CHUNK, CHUNK)
        st = [st_scr[d, g] for g in range(SSM_GROUPS)]
        yi = [_nt(grp(xc_scr[pl.ds(r0, CHUNK), :], g, GW + SSM_GROUPS * N).astype(bf16),
                  st[g].astype(bf16)) for g in range(SSM_GROUPS)]
        tot = tot_scr[c]
        for g in range(SSM_GROUPS):
            dec = jnp.concatenate(
                [jnp.broadcast_to(jnp.exp(tot[0:1, d * H + g * HG + hh:d * H + g * HG + hh + 1]), (P, N))
                 for hh in range(HG)], axis=0)
            st_scr[d, g] = st[g] * dec + cs_scr[d, c, g]
        return r0, jnp.concatenate(yi, axis=1) * es_scr[d, pl.ds(r0, CHUNK), :]

    def fwd(c, carry):
        r0, yi = inter(0, c)
        y_scr[pl.ds(r0, CHUNK), :] += yi
        return carry

    lax.fori_loop(0, nc, fwd, 0)

    def bwd(i, carry):
        r0, yi = inter(1, nc - 1 - i)
        yz = (y_scr[pl.ds(r0, CHUNK), :] + yi) * _silu(z_ref[pl.ds(r0, CHUNK), :])
        ms = jnp.mean(yz * yz, axis=-1, keepdims=True)
        o_ref[pl.ds(r0, CHUNK), :] = (yz * lax.rsqrt(ms + EPS) * ng_ref[...]).astype(o_ref.dtype)
        return carry

    lax.fori_loop(0, nc, bwd, 0)

    if want_state:
        for d in range(2):
            for h in range(H):
                g, hh = divmod(h, HG)
                hout_ref[d, h] = st_scr[d, g, hh * P:(hh + 1) * P, :]


def _pad_lanes(v, n=128):
    return jnp.pad(v, ((0, 0), (0, n - v.shape[1])))


def _ssm(u, dt_t, p, h0, layer, nb, L, want_state):
    has_h0 = h0 is not None
    H = SSM_HEADS
    a_log = p['ssm_a_log'].reshape(1, 2 * H)
    dt_bias = p['ssm_dt_bias'].reshape(1, 2 * H)
    small = lambda shape: pl.BlockSpec(shape, lambda b: (0, 0))
    in_specs = [pl.BlockSpec((L, 2 * GW), lambda b: (b, C_XBC // (2 * GW))),
                pl.BlockSpec((L, GW), lambda b: (b, C_Z // GW)),
                pl.BlockSpec((L, 128), lambda b: (b, C_DT // 128)),
                pl.BlockSpec((2 * H, L), lambda b: (0, b)),
                small((SSM_K, 2 * GW)), small((1, 2 * GW)),
                small((1, 128)), small((2 * H, 128)), small((1, 128)), small((2 * H, 128)),
                small((2, GW)), small((1, GW))]
    args = [u, u, u, dt_t, p['ssm_conv_w'], p['ssm_conv_b'].reshape(1, 2 * GW),
            _pad_lanes(a_log), jnp.broadcast_to(a_log.reshape(2 * H, 1), (2 * H, 128)),
            _pad_lanes(dt_bias), jnp.broadcast_to(dt_bias.reshape(2 * H, 1), (2 * H, 128)),
            jnp.repeat(p['ssm_d'], SSM_P, axis=1), p['ssm_norm'].reshape(1, GW)]
    if has_h0:
        in_specs.append(pl.BlockSpec((None, None, 2, H, SSM_P, SSM_N), lambda b: (b, layer, 0, 0, 0, 0)))
        args.append(h0)
    out_specs = [pl.BlockSpec((L, GW), lambda b: (b, 0))]
    out_shape = [jax.ShapeDtypeStruct((nb * L, GW), bf16)]
    if want_state:
        out_specs.append(pl.BlockSpec((None, 2, H, SSM_P, SSM_N), lambda b: (b, 0, 0, 0, 0)))
        out_shape.append(jax.ShapeDtypeStruct((nb, 2, H, SSM_P, SSM_N), f32))
    res = pl.pallas_call(
        functools.partial(_ssm_kernel, L=L, has_h0=has_h0, want_state=want_state),
        grid=(nb,),
        in_specs=in_specs,
        out_specs=out_specs,
        out_shape=out_shape,
        scratch_shapes=[pltpu.VMEM((L + 16, 2 * GW), f32),
                        pltpu.VMEM((L, 2 * GW), f32),
                        pltpu.VMEM((L, GW), f32),
                        pltpu.VMEM((2, SSM_GROUPS, H // SSM_GROUPS * SSM_P, SSM_N), f32),
                        pltpu.VMEM((2, L, GW), f32),
                        pltpu.VMEM((2, L // CHUNK, SSM_GROUPS, H // SSM_GROUPS * SSM_P, SSM_N), f32),
                        pltpu.VMEM((L // CHUNK, 8, 128), f32)],
        compiler_params=_cparams(("arbitrary",)),
        name="ssd_mixer",
    )(*args)
    return res if want_state else (res[0], None)


def _ctx_att_kernel(sink_ref, q_ref, k_ref, v_ref, o_ref, *, L):
    G = ATT_HEADS // ATT_KV
    scale = ATT_D ** -0.5
    kb = [k_ref[:, j * ATT_D:(j + 1) * ATT_D].astype(bf16) for j in range(ATT_KV)]
    vb = [v_ref[:, j * ATT_D:(j + 1) * ATT_D].astype(bf16) for j in range(ATT_KV)]
    s = [_nt((q_ref[:, h * ATT_D:(h + 1) * ATT_D] * scale).astype(bf16), kb[h // G])
         for h in range(ATT_HEADS)]
    p, den = [], []
    for h in range(ATT_HEADS):
        sink = sink_ref[h]
        m = jnp.maximum(jnp.max(s[h], axis=-1, keepdims=True), sink)
        e = jnp.exp(s[h] - m)
        den.append(jnp.sum(e, axis=-1, keepdims=True) + jnp.exp(sink - m))
        p.append(e.astype(bf16))
    o = [_nn(p[h], vb[h // G]) for h in range(ATT_HEADS)]
    for h in range(ATT_HEADS):
        o_ref[:, h * ATT_D:(h + 1) * ATT_D] = (o[h] / den[h]).astype(o_ref.dtype)


def _ctx_attention(u, sink, nb, L):
    return pl.pallas_call(
        functools.partial(_ctx_att_kernel, L=L),
        grid=(nb,),
        in_specs=[pl.BlockSpec(memory_space=pltpu.SMEM),
                  pl.BlockSpec((L, GW), lambda b: (b, C_AQ // GW)),
                  pl.BlockSpec((L, 128), lambda b: (b, C_AK // 128)),
                  pl.BlockSpec((L, 128), lambda b: (b, C_AV // 128))],
        out_specs=pl.BlockSpec((L, GW), lambda b: (b, 0)),
        out_shape=jax.ShapeDtypeStruct((nb * L, GW), bf16),
        compiler_params=_cparams(("arbitrary",)),
        name="ctx_attention",
    )(sink, u, u, u)


def _rope(x, cos, sin):
    w = x.shape[1]
    lane = lax.broadcasted_iota(jnp.int32, x.shape, 1)
    first = (lane % 32) < 16
    rot = jnp.where(first, -pltpu.roll(x, w - 16, 1), pltpu.roll(x, 16, 1))
    return x * cos + rot * sin


def _half_variants(x, keep_fill):
    lane = lax.broadcasted_iota(jnp.int32, x.shape, 1)
    lo = lane < ATT_D
    a0 = jnp.where(lo, x, keep_fill)
    b1 = jnp.where(lo, keep_fill, x)
    xr = pltpu.roll(x, ATT_D, 1)
    a1 = jnp.where(lo, xr, keep_fill)
    b0 = jnp.where(lo, keep_fill, xr)
    return ((a0, b0), (a1, b1))


def _half_variants_t(xt):
    sub = lax.broadcasted_iota(jnp.int32, xt.shape, 0)
    lo = sub < ATT_D
    a0 = jnp.where(lo, xt, 0.0)
    b1 = jnp.where(lo, 0.0, xt)
    xr = pltpu.roll(xt, ATT_D, 0)
    a1 = jnp.where(lo, xr, 0.0)
    b0 = jnp.where(lo, 0.0, xr)
    return ((a0, b0), (a1, b1))


def _lat_att_kernel(sink_ref, q_ref, k_ref, v_ref, kc_ref, vc_ref, cos_ref, sin_ref, o_ref,
                    qr_scr, kt_scr, vv_scr, kct_scr, vcv_scr, bias_scr, *, L):
    G = ATT_HEADS // ATT_KV
    B = CHUNK
    nb = L // B
    scale = ATT_D ** -0.5
    zeros = jnp.zeros((B, 128), bf16)
    for j in range(ATT_KV):
        for hf in range(2):
            kt_scr[j, hf, 0] = zeros
            kt_scr[j, hf, nb + 1] = zeros
            vv_scr[j, hf, 0:B, :] = zeros
            vv_scr[j, hf, B + L:2 * B + L, :] = zeros
    kct = _half_variants_t(kc_ref[...].T)
    vcv = _half_variants(vc_ref[...], 1.0)
    for j in range(ATT_KV):
        for hf in range(2):
            kct_scr[j, hf] = kct[j][hf].astype(bf16)
            vcv_scr[j, hf] = vcv[j][hf].astype(bf16)

    rowi = lax.broadcasted_iota(jnp.int32, (2 * B, 3 * B), 0) % B
    coli = lax.broadcasted_iota(jnp.int32, (2 * B, 3 * B), 1)
    inwin = jnp.abs(rowi - (coli - B)) <= ATT_WIN
    ninf = jnp.float32(-jnp.inf)
    bias_scr[0] = jnp.where(inwin & (coli >= B), 0.0, ninf)
    bias_scr[1] = jnp.where(inwin, 0.0, ninf)
    bias_scr[2] = jnp.where(inwin & (coli < 2 * B), 0.0, ninf)

    def prep(n, carry):
        r0 = pl.multiple_of(n * B, B)
        cos = cos_ref[pl.ds(r0, B), :]
        sin = sin_ref[pl.ds(r0, B), :]
        cos4 = jnp.concatenate([cos] * 4, axis=1)
        sin4 = jnp.concatenate([sin] * 4, axis=1)
        qr_scr[pl.ds(r0, B), :] = (_rope(q_ref[pl.ds(r0, B), :], cos4, sin4) * scale).astype(bf16)
        kvar = _half_variants_t(_rope(k_ref[pl.ds(r0, B), :], cos, sin).T)
        vvar = _half_variants(v_ref[pl.ds(r0, B), :], 1.0)
        for j in range(ATT_KV):
            for hf in range(2):
                kt_scr[j, hf, n + 1] = kvar[j][hf].astype(bf16)
                vv_scr[j, hf, pl.ds(B + r0, B), :] = vvar[j][hf].astype(bf16)
        return carry

    lax.fori_loop(0, nb, prep, 0)

    lane_lo = lax.broadcasted_iota(jnp.int32, (2 * B, 128), 1) < ATT_D

    def blk(n, carry):
        r0 = pl.multiple_of(n * B, B)
        bias = bias_scr[jnp.where(n == 0, 0, jnp.where(n == nb - 1, 2, 1))]
        combos = [(j, hf) for j in range(ATT_KV) for hf in range(2)]
        qs = [jnp.concatenate([qr_scr[pl.ds(r0, B), (2 * j) * 128:(2 * j + 1) * 128],
                               qr_scr[pl.ds(r0, B), (2 * j + 1) * 128:(2 * j + 2) * 128]], axis=0)
              for j in range(ATT_KV)]
        s_c = [_nn(qs[j], kct_scr[j, hf]) for j, hf in combos]
        s_b = [jnp.concatenate([_nn(qs[j], kt_scr[j, hf, n + t]) for t in range(3)], axis=1)
               for j, hf in combos]
        p_c, p_b, esink = [], [], []
        for i, (j, hf) in enumerate(combos):
            sb = s_b[i] + bias
            sink = jnp.concatenate([jnp.full((B, 1), sink_ref[G * j + hf], f32),
                                    jnp.full((B, 1), sink_ref[G * j + 2 + hf], f32)], axis=0)
            m = jnp.maximum(jnp.maximum(jnp.max(s_c[i], axis=-1, keepdims=True),
                                        jnp.max(sb, axis=-1, keepdims=True)), sink)
            p_c.append(jnp.exp(s_c[i] - m).astype(bf16))
            p_b.append(jnp.exp(sb - m).astype(bf16))
            esink.append(jnp.exp(sink - m))
        oe = [_nn(p_c[i], vcv_scr[j, hf]) + _nn(p_b[i], vv_scr[j, hf, pl.ds(r0, 3 * B), :])
              for i, (j, hf) in enumerate(combos)]
        outs = []
        for i, (j, hf) in enumerate(combos):
            den = oe[i][:, (1 - hf) * ATT_D:(1 - hf) * ATT_D + 1] + esink[i]
            outs.append(oe[i] / den)
        for j in range(ATT_KV):
            o = jnp.where(lane_lo, outs[2 * j], outs[2 * j + 1])
            o_ref[pl.ds(r0, B), (2 * j) * 128:(2 * j + 1) * 128] = o[0:B].astype(o_ref.dtype)
            o_ref[pl.ds(r0, B), (2 * j + 1) * 128:(2 * j + 2) * 128] = o[B:2 * B].astype(o_ref.dtype)
        return carry

    lax.fori_loop(0, nb, blk, 0, unroll=2)


def _rope_tables(L):
    pos = jnp.arange(L)
    rows = (pos // GRID_W).astype(f32)
    cols = (pos % GRID_W).astype(f32)
    half = ATT_D // 4
    freqs = ROPE_THETA ** (-jnp.arange(half, dtype=f32) / half)
    ang_r = rows[:, None] * freqs
    ang_c = cols[:, None] * freqs
    ang = jnp.concatenate([ang_r, ang_r, ang_c, ang_c], axis=1)
    ang = jnp.concatenate([ang, ang], axis=1)
    return jnp.cos(ang), jnp.sin(ang)


def _lat_attention(u, k_cache, v_cache, sink, layer, nb, L):
    Lc = k_cache.shape[2]
    cos, sin = _rope_tables(L)
    kc = k_cache.reshape(nb, DEPTH, Lc, ATT_KV * ATT_D)
    vc = v_cache.reshape(nb, DEPTH, Lc, ATT_KV * ATT_D)
    cspec = pl.BlockSpec((None, None, Lc, 128), lambda b: (b, layer, 0, 0))
    return pl.pallas_call(
        functools.partial(_lat_att_kernel, L=L),
        grid=(nb,),
        in_specs=[pl.BlockSpec(memory_space=pltpu.SMEM),
                  pl.BlockSpec((L, GW), lambda b: (b, C_AQ // GW)),
                  pl.BlockSpec((L, 128), lambda b: (b, C_AK // 128)),
                  pl.BlockSpec((L, 128), lambda b: (b, C_AV // 128)),
                  cspec, cspec,
                  pl.BlockSpec((L, 128), lambda b: (0, 0)),
                  pl.BlockSpec((L, 128), lambda b: (0, 0))],
        out_specs=pl.BlockSpec((L, GW), lambda b: (b, 0)),
        out_shape=jax.ShapeDtypeStruct((nb * L, GW), bf16),
        scratch_shapes=[pltpu.VMEM((L, GW), bf16),
                        pltpu.VMEM((ATT_KV, 2, L // CHUNK + 2, 128, CHUNK), bf16),
                        pltpu.VMEM((ATT_KV, 2, L + 2 * CHUNK, 128), bf16),
                        pltpu.VMEM((ATT_KV, 2, 128, Lc), bf16),
                        pltpu.VMEM((ATT_KV, 2, Lc, 128), bf16),
                        pltpu.VMEM((3, 2 * CHUNK, 3 * CHUNK), f32)],
        compiler_params=_cparams(("arbitrary",)),
        name="lat_attention",
    )(sink, u, u, u, kc, vc, cos, sin)


def _permute_w_in_kernel(w_ref, o_ref):
    ndt = 2 * SSM_HEADS
    o_ref[:, C_CONV:C_CONV + 1024] = w_ref[:, 0:1024].astype(bf16)
    o_ref[:, C_XBC:C_XBC + 1024] = w_ref[:, 1536:2560].astype(bf16)
    o_ref[:, C_Z:C_Z + GW] = w_ref[:, 1024:1536].astype(bf16)
    o_ref[:, C_RQ:C_DT] = w_ref[:, 2560 + ndt:N_IN].astype(bf16)
    tail = w_ref[:, 2560:2560 + 128]
    lane = lax.broadcasted_iota(jnp.int32, tail.shape, 1)
    o_ref[:, C_DT:UW] = jnp.where(lane < ndt, tail, 0.0).astype(bf16)


def _permute_w_in(w):
    tr = 256
    return pl.pallas_call(
        _permute_w_in_kernel,
        grid=(DEPTH, D // tr),
        in_specs=[pl.BlockSpec((None, tr, N_IN), lambda l, i: (l, i, 0))],
        out_specs=pl.BlockSpec((None, tr, UW), lambda l, i: (l, i, 0)),
        out_shape=jax.ShapeDtypeStruct((DEPTH, D, UW), bf16),
        compiler_params=_cparams(("arbitrary", "arbitrary")),
        name="permute_w_in",
    )(w)


def _layer(x, mod, p, ctx, layer, nb, L, final):
    per_seq = ctx is not None
    u = _inproj(x, p['norm_mix'], mod, p['w_in_p'], layer, L, per_seq)
    dt_t = u[:, C_DT:C_DT + 2 * SSM_HEADS].T
    want_state = ctx is None
    o_conv = _conv_module(u, p['conv_w'], p['conv_b'], p['conv_ln_g'], p['conv_ln_b'], nb, L)
    if ctx is None:
        o_ssm, h_ssm = _ssm(u, dt_t, p, None, layer, nb, L, True)
        o_ret, h_ret = _retention(u, p['ret_log_decay'], p['ret_gn_g'], None, layer, nb, L, True)
        o_att = _ctx_attention(u, p['att_sink'], nb, L)
    else:
        k_c, v_c, s_ssm, s_ret = ctx
        o_ssm, h_ssm = _ssm(u, dt_t, p, s_ssm, layer, nb, L, False)
        o_ret, h_ret = _retention(u, p['ret_log_decay'], p['ret_gn_g'], s_ret, layer, nb, L, False)
        o_att = _lat_attention(u, k_c, v_c, p['att_sink'], layer, nb, L)
    x1 = _outproj(o_conv, o_ssm, o_ret, o_att, p['w_out_b'], layer, x, mod, L, per_seq)
    x2 = _mlp(x1, p['norm_mlp'], mod, p['final_norm'], p['w1_b'], p['w2_b'], layer, L, per_seq, final)
    states = None
    if want_state:
        ak = u[:, C_AK:C_AK + 128].reshape(nb, L, ATT_KV, ATT_D)
        av = u[:, C_AV:C_AV + 128].reshape(nb, L, ATT_KV, ATT_D)
        states = (ak, av, h_ssm, h_ret)
    return x2, states


def kernel(x_prompt, x_sample, cache_attn_k, cache_attn_v, state_ssm, state_ret, c, c_ctx, ada_w, ada_b, norm_mix, norm_mlp, w_in, conv_w, conv_b, conv_ln_g, conv_ln_b, ssm_conv_w, ssm_conv_b, ssm_a_log, ssm_dt_bias, ssm_d, ssm_norm, ret_log_decay, ret_gn_g, att_sink, w_out, w1, w2, final_norm):
    nbp, Lp, _ = x_prompt.shape
    nbs, Ls, _ = x_sample.shape
    cvec = jnp.concatenate([c_ctx[None, :], c, jnp.zeros((16 - 1 - nbs, D), f32)], axis=0)
    mod = _ada_mod(cvec, ada_w, ada_b)
    y_p = x_prompt.reshape(nbp * Lp, D)
    y_s = x_sample.reshape(nbs * Ls, D)
    new_k, new_v, new_ssm, new_ret = [], [], [], []
    w_in_p = _permute_w_in(w_in)
    w_out_b = w_out.astype(bf16)
    w1_b = w1.astype(bf16)
    w2_b = w2.astype(bf16)
    for l in range(DEPTH):
        p = dict(norm_mix=norm_mix[l], norm_mlp=norm_mlp[l], w_in_p=w_in_p,
                 conv_w=conv_w[l], conv_b=conv_b[l], conv_ln_g=conv_ln_g[l], conv_ln_b=conv_ln_b[l],
                 ssm_conv_w=ssm_conv_w[l], ssm_conv_b=ssm_conv_b[l], ssm_a_log=ssm_a_log[l],
                 ssm_dt_bias=ssm_dt_bias[l], ssm_d=ssm_d[l], ssm_norm=ssm_norm[l],
                 ret_log_decay=ret_log_decay[l], ret_gn_g=ret_gn_g[l], att_sink=att_sink[l],
                 w_out_b=w_out_b, w1_b=w1_b, w2_b=w2_b, final_norm=final_norm)
        mod_l = mod[l].reshape(16, 6, 1, D)
        final = l == DEPTH - 1
        y_p, (k_l, v_l, hs_l, hr_l) = _layer(y_p, mod_l[0:1], p, None, l, nbp, Lp, final)
        new_k.append(k_l)
        new_v.append(v_l)
        new_ssm.append(hs_l)
        new_ret.append(hr_l)
        ctx = (cache_attn_k, cache_attn_v, state_ssm, state_ret)
        y_s, _ = _layer(y_s, mod_l[1:1 + nbs], p, ctx, l, nbs, Ls, final)
    return (y_p.reshape(nbp, Lp, D), y_s.reshape(nbs, Ls, D),
            jnp.stack(new_k, axis=1), jnp.stack(new_v, axis=1),
            jnp.stack(new_ssm, axis=1), jnp.stack(new_ret, axis=1))
```

```python
import functools

import jax
import jax.numpy as jnp
from jax import lax
from jax.experimental import pallas as pl
from jax.experimental.pallas import tpu as pltpu

f32 = jnp.float32
bf16 = jnp.bfloat16

D = 2048
DEPTH = 2
GW = 512
CONV_K = 31
SSM_HEADS, SSM_P, SSM_N, SSM_GROUPS, SSM_K = 8, 64, 128, 2, 5
RET_HEADS, RET_D = 4, 128
ATT_HEADS, ATT_KV, ATT_D, ATT_WIN = 8, 2, 64, 128
GRID_W = 64
ROPE_THETA = 10000.0
CHUNK = 128
D_FF = 4 * D
EPS = 1e-6

C_CONV, C_XBC, C_Z, C_RQ, C_RK, C_RV, C_RG, C_AQ, C_AK, C_AV, C_DT = (
    0, 1024, 2048, 2560, 3072, 3584, 4096, 4608, 5120, 5248, 5376)
UW = 5504
N_IN = 5392
N_CHUNK = 512

VMEM_LIMIT = 56 * 1024 * 1024


def _cparams(sem):
    return pltpu.CompilerParams(dimension_semantics=sem, vmem_limit_bytes=VMEM_LIMIT)


def _sigmoid(x):
    return 1.0 / (1.0 + jnp.exp(-x))


def _silu(x):
    return x * _sigmoid(x)


def _nt(a, b):
    return lax.dot_general(a, b, (((1,), (1,)), ((), ())), preferred_element_type=f32)


def _tn(a, b):
    return lax.dot_general(a, b, (((0,), (0,)), ((), ())), preferred_element_type=f32)


def _nn(a, b):
    return jnp.dot(a, b, preferred_element_type=f32)


def _split3(x):
    hi = x.astype(bf16)
    r1 = x - hi.astype(f32)
    mid = r1.astype(bf16)
    lo = (r1 - mid.astype(f32)).astype(bf16)
    return hi, mid, lo


def _ada_kernel(c_ref, w_ref, b_ref, o_ref):
    c = c_ref[...]
    s = _silu(c).astype(bf16)
    o_ref[...] = _nn(s, w_ref[...].astype(bf16)) + b_ref[...]


def _ada_mod(cvec, ada_w, ada_b):
    tn = 1024
    return pl.pallas_call(
        _ada_kernel,
        grid=(DEPTH, 6 * D // tn),
        in_specs=[pl.BlockSpec((16, D), lambda l, j: (0, 0)),
                  pl.BlockSpec((None, D, tn), lambda l, j: (l, 0, j)),
                  pl.BlockSpec((None, 1, tn), lambda l, j: (l, 0, j))],
        out_specs=pl.BlockSpec((None, 16, tn), lambda l, j: (l, 0, j)),
        out_shape=jax.ShapeDtypeStruct((DEPTH, 16, 6 * D), f32),
        compiler_params=_cparams(("arbitrary", "arbitrary")),
        name="ada_mod",
    )(cvec, ada_w, ada_b.reshape(DEPTH, 1, 6 * D))


def _mod_spec(idx, tm, L, per_seq):
    if per_seq:
        return pl.BlockSpec((None, None, 1, D), lambda i, *_: ((i * tm) // L, idx, 0, 0))
    return pl.BlockSpec((None, None, 1, D), lambda i, *_: (0, idx, 0, 0))


def _inproj_kernel(x_ref, g_ref, sc_ref, sh_ref, w_ref, u_ref):
    x = x_ref[...]
    ms = jnp.mean(x * x, axis=-1, keepdims=True)
    h = x * lax.rsqrt(ms + EPS) * g_ref[...] * (1.0 + sc_ref[...]) + sh_ref[...]
    hb = h.astype(bf16)
    for n0 in range(0, UW, N_CHUNK):
        n1 = min(n0 + N_CHUNK, UW)
        u_ref[:, n0:n1] = _nn(hb, w_ref[:, n0:n1])


def _inproj(x, norm_g, mod, w_in_p, layer, L, per_seq):
    T = x.shape[0]
    tm = 256
    return pl.pallas_call(
        _inproj_kernel,
        grid=(T // tm,),
        in_specs=[pl.BlockSpec((tm, D), lambda i: (i, 0)),
                  pl.BlockSpec((1, D), lambda i: (0, 0)),
                  _mod_spec(1, tm, L, per_seq),
                  _mod_spec(0, tm, L, per_seq),
                  pl.BlockSpec((None, D, UW), lambda i: (layer, 0, 0), pipeline_mode=pl.Buffered(1))],
        out_specs=pl.BlockSpec((tm, UW), lambda i: (i, 0)),
        out_shape=jax.ShapeDtypeStruct((T, UW), f32),
        compiler_params=_cparams(("arbitrary",)),
        name="in_proj",
    )(x, norm_g.reshape(1, D), mod, mod, w_in_p)


def _outproj_kernel(oc_ref, os_ref, or_ref, oa_ref, w_ref, x_ref, g_ref, o_ref):
    for n0 in range(0, D, N_CHUNK):
        n1 = n0 + N_CHUNK
        acc = _nn(oc_ref[...], w_ref[0:GW, n0:n1])
        acc = acc + _nn(os_ref[...], w_ref[GW:2 * GW, n0:n1])
        acc = acc + _nn(or_ref[...], w_ref[2 * GW:3 * GW, n0:n1])
        acc = acc + _nn(oa_ref[...], w_ref[3 * GW:4 * GW, n0:n1])
        o_ref[:, n0:n1] = x_ref[:, n0:n1] + g_ref[:, n0:n1] * acc


def _outproj(o_conv, o_ssm, o_ret, o_att, w_out_b, layer, x, mod, L, per_seq):
    T = x.shape[0]
    tm = 512
    ospec = pl.BlockSpec((tm, GW), lambda i: (i, 0))
    return pl.pallas_call(
        _outproj_kernel,
        grid=(T // tm,),
        in_specs=[ospec, ospec, ospec, ospec,
                  pl.BlockSpec((None, D, D), lambda i: (layer, 0, 0), pipeline_mode=pl.Buffered(1)),
                  pl.BlockSpec((tm, D), lambda i: (i, 0)),
                  _mod_spec(2, tm, L, per_seq)],
        out_specs=pl.BlockSpec((tm, D), lambda i: (i, 0)),
        out_shape=jax.ShapeDtypeStruct((T, D), f32),
        compiler_params=_cparams(("arbitrary",)),
        name="out_proj",
    )(o_conv, o_ssm, o_ret, o_att, w_out_b, x, mod)


def _mlp_kernel(x_ref, g_ref, sc_ref, sh_ref, g2_ref, fn_ref, w1_ref, w2_ref, o_ref, h_scr, *, final):
    f = pl.program_id(1)
    tm = x_ref.shape[0]
    RB = 256

    @pl.when(f == 0)
    def _():
        for r in range(0, tm, RB):
            x = x_ref[r:r + RB, :]
            ms = jnp.mean(x * x, axis=-1, keepdims=True)
            h = x * lax.rsqrt(ms + EPS) * g_ref[...] * (1.0 + sc_ref[...]) + sh_ref[...]
            h_scr[r:r + RB, :] = h.astype(bf16)
        o_ref[...] = jnp.zeros_like(o_ref)

    for r in range(0, tm, 2 * RB):
        a = _nn(h_scr[r:r + 2 * RB, :], w1_ref[...])
        a = jnp.maximum(a, 0.0)
        a = (a * a).astype(bf16)
        o_ref[r:r + 2 * RB, :] += _nn(a, w2_ref[...])

    @pl.when(f == pl.num_programs(1) - 1)
    def _():
        for r in range(0, tm, RB):
            y = x_ref[r:r + RB, :] + g2_ref[...] * o_ref[r:r + RB, :]
            if final:
                ms = jnp.mean(y * y, axis=-1, keepdims=True)
                y = y * lax.rsqrt(ms + EPS) * fn_ref[...]
            o_ref[r:r + RB, :] = y


def _mlp(x, norm_g, mod, final_norm, w1_b, w2_b, layer, L, per_seq, final):
    T = x.shape[0]
    tm, tf = 1024, 512
    return pl.pallas_call(
        functools.partial(_mlp_kernel, final=final),
        grid=(T // tm, D_FF // tf),
        in_specs=[pl.BlockSpec((tm, D), lambda i, f: (i, 0)),
                  pl.BlockSpec((1, D), lambda i, f: (0, 0)),
                  _mod_spec(4, tm, L, per_seq),
                  _mod_spec(3, tm, L, per_seq),
                  _mod_spec(5, tm, L, per_seq),
                  pl.BlockSpec((1, D), lambda i, f: (0, 0)),
                  pl.BlockSpec((None, D, tf), lambda i, f: (layer, 0, f)),
                  pl.BlockSpec((None, tf, D), lambda i, f: (layer, f, 0))],
        out_specs=pl.BlockSpec((tm, D), lambda i, f: (i, 0)),
        out_shape=jax.ShapeDtypeStruct((T, D), f32),
        scratch_shapes=[pltpu.VMEM((tm, D), bf16)],
        compiler_params=_cparams(("arbitrary", "arbitrary")),
        name="mlp",
    )(x, norm_g.reshape(1, D), mod, mod, mod, final_norm.reshape(1, D), w1_b, w2_b)


def _conv_kernel(u_ref, w_ref, b_ref, lg_ref, lb_ref, o_ref, vpad_ref, ph_ref, *, L):
    RC = 64
    pad = 16
    SUB = 8
    off0 = pad - CONV_K // 2
    nph = RC + SUB * ((off0 + CONV_K - 1) // SUB)
    vpad_ref[0:pad, :] = jnp.zeros((pad, GW), f32)
    vpad_ref[pad + L:2 * pad + L, :] = jnp.zeros((pad, GW), f32)

    def fill(i, carry):
        r0 = pl.multiple_of(i * RC, RC)
        a = u_ref[pl.ds(r0, RC), 0:GW]
        g = u_ref[pl.ds(r0, RC), GW:2 * GW]
        vpad_ref[pl.ds(pad + r0, RC), :] = a * _sigmoid(g)
        return carry

    lax.fori_loop(0, L // RC, fill, 0)

    def body(i, carry):
        r0 = pl.multiple_of(i * RC, RC)
        win = vpad_ref.at[pl.ds(r0, RC + 2 * pad), :]
        accs = []
        for c0 in range(0, GW, 128):
            ls = slice(c0, c0 + 128)
            for ph in range(SUB):
                ph_ref[ph, :, ls] = win[ph:ph + nph, ls]
            acc = jnp.broadcast_to(b_ref[:, ls], (RC, 128))
            for k in range(CONV_K):
                a, ph = divmod(off0 + k, SUB)
                acc = acc + ph_ref[ph, SUB * a:SUB * a + RC, ls] * w_ref[k:k + 1, ls]
            accs.append(acc)
        acc = jnp.concatenate(accs, axis=1)
        mu = jnp.mean(acc, axis=-1, keepdims=True)
        xc = acc - mu
        var = jnp.mean(xc * xc, axis=-1, keepdims=True)
        y = xc * lax.rsqrt(var + EPS) * lg_ref[...] + lb_ref[...]
        o_ref[pl.ds(r0, RC), :] = _silu(y).astype(o_ref.dtype)
        return carry

    lax.fori_loop(0, L // RC, body, 0)


def _conv_module(u, conv_w, conv_b, ln_g, ln_b, nb, L):
    return pl.pallas_call(
        functools.partial(_conv_kernel, L=L),
        grid=(nb,),
        in_specs=[pl.BlockSpec((L, 2 * GW), lambda b: (b, C_CONV // (2 * GW))),
                  pl.BlockSpec((CONV_K, GW), lambda b: (0, 0)),
                  pl.BlockSpec((1, GW), lambda b: (0, 0)),
                  pl.BlockSpec((1, GW), lambda b: (0, 0)),
                  pl.BlockSpec((1, GW), lambda b: (0, 0))],
        out_specs=pl.BlockSpec((L, GW), lambda b: (b, 0)),
        out_shape=jax.ShapeDtypeStruct((nb * L, GW), bf16),
        scratch_shapes=[pltpu.VMEM((L + 32, GW), f32), pltpu.VMEM((8, 64 + 24, GW), f32)],
        compiler_params=_cparams(("arbitrary",)),
        name="conv_module",
    )(u, conv_w, conv_b.reshape(1, GW), ln_g.reshape(1, GW), ln_b.reshape(1, GW))


def _ret_kernel(*refs, L, has_h0, want_state):
    ld_ref, q_ref, k_ref, v_ref, g_ref, gn_ref = refs[:6]
    pos = 6
    h0_ref = None
    if has_h0:
        h0_ref = refs[pos]
        pos += 1
    o_ref = refs[pos]
    pos += 1
    hout_ref = None
    if want_state:
        hout_ref = refs[pos]
        pos += 1
    y_scr, st_scr, dec_scr, cs_scr = refs[pos:pos + 4]

    nc = L // CHUNK
    H = RET_HEADS
    row = lax.broadcasted_iota(jnp.int32, (CHUNK, CHUNK), 0)
    col = lax.broadcasted_iota(jnp.int32, (CHUNK, CHUNK), 1)
    rowf = row.astype(f32)
    diff = (row - col).astype(f32)
    kscale = RET_D ** -0.5
    sls = [slice(h * RET_D, (h + 1) * RET_D) for h in range(H)]

    for d in range(2):
        for h in range(H):
            if has_h0:
                st_scr[d, h] = h0_ref[d, h]
            else:
                st_scr[d, h] = jnp.zeros((RET_D, RET_D), f32)
    for h in range(H):
        laf = ld_ref[h]
        lab = ld_ref[H + h]
        dec_scr[h] = jnp.where(col < row, jnp.exp(diff * laf),
                               jnp.where(col > row, jnp.exp(-diff * lab), 2.0))


    def intra(c, carry):
        r0 = pl.multiple_of(c * CHUNK, CHUNK)
        qb = [q_ref[pl.ds(r0, CHUNK), sls[h]].astype(bf16) for h in range(H)]
        kb = [(k_ref[pl.ds(r0, CHUNK), sls[h]] * kscale).astype(bf16) for h in range(H)]
        vh = [v_ref[pl.ds(r0, CHUNK), sls[h]] for h in range(H)]
        s = [_nt(qb[h], kb[h]) for h in range(H)]
        cf = [_tn((vh[h] * jnp.exp((CHUNK - 1.0 - rowf) * ld_ref[h])).astype(bf16), kb[h])
              for h in range(H)]
        cb = [_tn((vh[h] * jnp.exp(rowf * ld_ref[H + h])).astype(bf16), kb[h]) for h in range(H)]
        m = [(s[h] * dec_scr[h]).astype(bf16) for h in range(H)]
        y = [_nn(m[h], vh[h].astype(bf16)) for h in range(H)]
        for h in range(H):
            y_scr[pl.ds(r0, CHUNK), sls[h]] = y[h]
            cs_scr[0, c, h] = cf[h]
            cs_scr[1, c, h] = cb[h]
        return carry

    lax.fori_loop(0, nc, intra, 0, unroll=2)

    def inter(d, c):
        r0 = pl.multiple_of(c * CHUNK, CHUNK)
        st = [st_scr[d, h] for h in range(H)]
        yi = [_nt(q_ref[pl.ds(r0, CHUNK), sls[h]].astype(bf16), st[h].astype(bf16)) for h in range(H)]
        for h in range(H):
            la = ld_ref[d * H + h]
            st_scr[d, h] = st[h] * jnp.exp(jnp.full((RET_D, RET_D), CHUNK * la, f32)) + cs_scr[d, c, h]
        return r0, yi

    def fwd(c, carry):
        r0, yi = inter(0, c)
        for h in range(H):
            y_scr[pl.ds(r0, CHUNK), sls[h]] += jnp.exp((rowf + 1.0) * ld_ref[h]) * yi[h]
        return carry

    lax.fori_loop(0, nc, fwd, 0, unroll=2)

    def bwd(i, carry):
        r0, yi = inter(1, nc - 1 - i)
        for h in range(H):
            sl = sls[h]
            y = y_scr[pl.ds(r0, CHUNK), sl] + jnp.exp((CHUNK - rowf) * ld_ref[H + h]) * yi[h]
            mu = jnp.mean(y, axis=-1, keepdims=True)
            yc = y - mu
            var = jnp.mean(yc * yc, axis=-1, keepdims=True)
            yn = yc * lax.rsqrt(var + EPS) * gn_ref[:, sl]
            g = g_ref[pl.ds(r0, CHUNK), sl]
            o_ref[pl.ds(r0, CHUNK), sl] = (_silu(g) * yn).astype(o_ref.dtype)
        return carry

    lax.fori_loop(0, nc, bwd, 0, unroll=2)

    if want_state:
        for d in range(2):
            for h in range(RET_HEADS):
                hout_ref[d, h] = st_scr[d, h]


def _retention(u, log_decay, gn_g, h0, layer, nb, L, want_state):
    has_h0 = h0 is not None
    cblk = lambda c: pl.BlockSpec((L, GW), lambda b, c=c: (b, c // GW))
    in_specs = [pl.BlockSpec(memory_space=pltpu.SMEM),
                cblk(C_RQ), cblk(C_RK), cblk(C_RV), cblk(C_RG),
                pl.BlockSpec((1, GW), lambda b: (0, 0))]
    args = [log_decay.reshape(2 * RET_HEADS), u, u, u, u, gn_g.reshape(1, GW)]
    if has_h0:
        in_specs.append(pl.BlockSpec((None, None, 2, RET_HEADS, RET_D, RET_D),
                                     lambda b: (b, layer, 0, 0, 0, 0)))
        args.append(h0)
    out_specs = [pl.BlockSpec((L, GW), lambda b: (b, 0))]
    out_shape = [jax.ShapeDtypeStruct((nb * L, GW), bf16)]
    if want_state:
        out_specs.append(pl.BlockSpec((None, 2, RET_HEADS, RET_D, RET_D), lambda b: (b, 0, 0, 0, 0)))
        out_shape.append(jax.ShapeDtypeStruct((nb, 2, RET_HEADS, RET_D, RET_D), f32))
    res = pl.pallas_call(
        functools.partial(_ret_kernel, L=L, has_h0=has_h0, want_state=want_state),
        grid=(nb,),
        in_specs=in_specs,
        out_specs=out_specs,
        out_shape=out_shape,
        scratch_shapes=[pltpu.VMEM((L, GW), f32),
                        pltpu.VMEM((2, RET_HEADS, RET_D, RET_D), f32),
                        pltpu.VMEM((RET_HEADS, CHUNK, CHUNK), f32),
                        pltpu.VMEM((2, L // CHUNK, RET_HEADS, RET_D, RET_D), f32)],
        compiler_params=_cparams(("arbitrary",)),
        name="retention",
    )(*args)
    return res if want_state else (res[0], None)


def _softplus(x):
    return jnp.maximum(x, 0.0) + jnp.log(1.0 + jnp.exp(-jnp.abs(x)))


def _ssm_kernel(*refs, L, has_h0, want_state):
    (xbc_ref, z_ref, dtc_ref, dtr_ref, cw_ref, cb_ref, alr_ref, alc_ref, dbr_ref, dbc_ref,
     dsk_ref, ng_ref) = refs[:12]
    pos = 12
    h0_ref = None
    if has_h0:
        h0_ref = refs[pos]
        pos += 1
    o_ref = refs[pos]
    pos += 1
    hout_ref = None
    if want_state:
        hout_ref = refs[pos]
        pos += 1
    xpad_ref, xc_scr, y_scr, st_scr, es_scr, cs_scr, tot_scr = refs[pos:pos + 7]

    nc = L // CHUNK
    H, P, N = SSM_HEADS, SSM_P, SSM_N
    HG = H // SSM_GROUPS
    pad = 8
    row = lax.broadcasted_iota(jnp.int32, (CHUNK, CHUNK), 0)
    col = lax.broadcasted_iota(jnp.int32, (CHUNK, CHUNK), 1)
    lt01 = (row >= col).astype(bf16)
    ut01 = (row <= col).astype(bf16)
    sub16 = lax.broadcasted_iota(jnp.int32, (2 * H, CHUNK), 0)
    lane_lo = col < P

    xpad_ref[0:pad, :] = jnp.zeros((pad, 2 * GW), f32)
    xpad_ref[pad + L:2 * pad + L, :] = jnp.zeros((pad, 2 * GW), f32)

    def fill(i, carry):
        r0 = pl.multiple_of(i * CHUNK, CHUNK)
        xpad_ref[pl.ds(pad + r0, CHUNK), :] = xbc_ref[pl.ds(r0, CHUNK), :]
        return carry

    lax.fori_loop(0, nc, fill, 0)

    for d in range(2):
        for h in range(H):
            g, hh = divmod(h, HG)
            if has_h0:
                st_scr[d, g, hh * P:(hh + 1) * P, :] = h0_ref[d, h]
            else:
                st_scr[d, g, hh * P:(hh + 1) * P, :] = jnp.zeros((P, N), f32)

    a_neg_r = -jnp.exp(alr_ref[...])
    a_neg_c = -jnp.exp(alc_ref[...])

    def per_head_lanes(v, base):
        cols = []
        for c2 in range(H // 2):
            a = jnp.broadcast_to(v[:, base + 2 * c2:base + 2 * c2 + 1], (CHUNK, 2 * P))
            b = jnp.broadcast_to(v[:, base + 2 * c2 + 1:base + 2 * c2 + 2], (CHUNK, 2 * P))
            cols.append(jnp.where(lane_lo, a, b))
        return jnp.concatenate(cols, axis=1)

    def grp(x, g, base):
        return x[:, base + g * N:base + (g + 1) * N]


    def intra(c, carry):
        r0 = pl.multiple_of(c * CHUNK, CHUNK)
        win = xpad_ref.at[pl.ds(r0, CHUNK + 2 * pad), :]
        off = pad - SSM_K // 2
        for c0 in range(0, 2 * GW, 128):
            ls = slice(c0, c0 + 128)
            acc = jnp.broadcast_to(cb_ref[:, ls], (CHUNK, 128))
            for k in range(SSM_K):
                acc = acc + win[off + k:off + k + CHUNK, ls] * cw_ref[k:k + 1, ls]
            xc_scr[pl.ds(r0, CHUNK), ls] = _silu(acc)
        xc = xc_scr[pl.ds(r0, CHUNK), :]
        xs = xc[:, 0:GW]
        bmb = [grp(xc, g, GW).astype(bf16) for g in range(SSM_GROUPS)]
        cmb = [grp(xc, g, GW + SSM_GROUPS * N).astype(bf16) for g in range(SSM_GROUPS)]
        gmat = [_nt(cmb[g], bmb[g]) for g in range(SSM_GROUPS)]

        dt_c = _softplus(dtc_ref[pl.ds(r0, CHUNK), :] + dbr_ref[...])
        dt_r = _softplus(dtr_ref[:, pl.ds(r0, CHUNK)] + dbc_ref[...])
        lc3 = _split3(dt_c * a_neg_r)
        lr3 = _split3(dt_r * a_neg_c)
        pc = [_nn(lt01, t) for t in lc3]
        sc = [_nn(ut01, t) for t in lc3]
        pr = [_nn(t, ut01) for t in lr3]
        sr = [_nn(t, lt01) for t in lr3]
        a_c = jnp.where(col < H, pc[0] + pc[1] + pc[2], sc[0] + sc[1] + sc[2])
        a_r = jnp.where(sub16 < H, pr[0] + pr[1] + pr[2], sr[0] + sr[1] + sr[2])
        tot = jnp.where(col[0:1, :] < H, a_c[CHUNK - 1:CHUNK, :], a_c[0:1, :])
        tot_scr[c] = jnp.broadcast_to(tot, (8, 128))
        ks = dt_c * jnp.exp(tot - a_c)
        es = jnp.exp(a_c)
        es_scr[0, pl.ds(r0, CHUNK), :] = per_head_lanes(es, 0)
        es_scr[1, pl.ds(r0, CHUNK), :] = per_head_lanes(es, H)
        xwf = (xs * per_head_lanes(ks, 0)).astype(bf16)
        xwb = (xs * per_head_lanes(ks, H)).astype(bf16)
        csf = [_tn(xwf[:, g * HG * P:(g + 1) * HG * P], bmb[g]) for g in range(SSM_GROUPS)]
        csb = [_tn(xwb[:, g * HG * P:(g + 1) * HG * P], bmb[g]) for g in range(SSM_GROUPS)]
        for g in range(SSM_GROUPS):
            cs_scr[0, c, g] = csf[g]
            cs_scr[1, c, g] = csb[g]

        ms = []
        for h in range(H):
            df = jnp.exp(jnp.minimum(a_c[:, h:h + 1] - a_r[h:h + 1, :], 0.0)) * dt_r[h:h + 1, :]
            db = jnp.exp(jnp.minimum(a_c[:, H + h:H + h + 1] - a_r[H + h:H + h + 1, :], 0.0)) \
                * dt_r[H + h:H + h + 1, :]
            dm = jnp.where(col <= row, df, 0.0) + jnp.where(col >= row, db, 0.0)
            ms.append((gmat[h // HG] * dm).astype(bf16))
        ys = []
        for c2 in range(H // 2):
            xcol = xs[:, c2 * 2 * P:(c2 + 1) * 2 * P]
            x_lo = jnp.where(lane_lo, xcol, 0.0).astype(bf16)
            x_hi = jnp.where(lane_lo, 0.0, xcol).astype(bf16)
            ys.append(_nn(ms[2 * c2], x_lo) + _nn(ms[2 * c2 + 1], x_hi))
        y = jnp.concatenate(ys, axis=1)
        y_scr[pl.ds(r0, CHUNK), :] = y + (dsk_ref[0:1, :] + dsk_ref[1:2, :]) * xs
        return carry

    lax.fori_loop(0, nc, intra, 0, unroll=2)

    def inter(d, c):
        r0 = pl.multiple_of(c * CHUNK, CHUNK)
        st = [st_scr[d, g] for g in range(SSM_GROUPS)]
        yi = [_nt(grp(xc_scr[pl.ds(r0, CHUNK), :], g, GW + SSM_GROUPS * N).astype(bf16),
                  st[g].astype(bf16)) for g in range(SSM_GROUPS)]
        tot = tot_scr[c]
        for g in range(SSM_GROUPS):
            dec = jnp.concatenate(
                [jnp.broadcast_to(jnp.exp(tot[0:1, d * H + g * HG + hh:d * H + g * HG + hh + 1]), (P, N))
                 for hh in range(HG)], axis=0)
            st_scr[d, g] = st[g] * dec + cs_scr[d, c, g]
        return r0, jnp.concatenate(yi, axis=1) * es_scr[d, pl.ds(r0, CHUNK), :]

    def fwd(c, carry):
        r0, yi = inter(0, c)
        y_scr[pl.ds(r0, CHUNK), :] += yi
        return carry

    lax.fori_loop(0, nc, fwd, 0)

    def bwd(i, carry):
        r0, yi = inter(1, nc - 1 - i)
        yz = (y_scr[pl.ds(r0, CHUNK), :] + yi) * _silu(z_ref[pl.ds(r0, CHUNK), :])
        ms = jnp.mean(yz * yz, axis=-1, keepdims=True)
        o_ref[pl.ds(r0, CHUNK), :] = (yz * lax.rsqrt(ms + EPS) * ng_ref[...]).astype(o_ref.dtype)
        return carry

    lax.fori_loop(0, nc, bwd, 0)

    if want_state:
        for d in range(2):
            for h in range(H):
                g, hh = divmod(h, HG)
                hout_ref[d, h] = st_scr[d, g, hh * P:(hh + 1) * P, :]


def _pad_lanes(v, n=128):
    return jnp.pad(v, ((0, 0), (0, n - v.shape[1])))


def _ssm(u, dt_t, p, h0, layer, nb, L, want_state):
    has_h0 = h0 is not None
    H = SSM_HEADS
    a_log = p['ssm_a_log'].reshape(1, 2 * H)
    dt_bias = p['ssm_dt_bias'].reshape(1, 2 * H)
    small = lambda shape: pl.BlockSpec(shape, lambda b: (0, 0))
    in_specs = [pl.BlockSpec((L, 2 * GW), lambda b: (b, C_XBC // (2 * GW))),
                pl.BlockSpec((L, GW), lambda b: (b, C_Z // GW)),
                pl.BlockSpec((L, 128), lambda b: (b, C_DT // 128)),
                pl.BlockSpec((2 * H, L), lambda b: (0, b)),
                small((SSM_K, 2 * GW)), small((1, 2 * GW)),
                small((1, 128)), small((2 * H, 128)), small((1, 128)), small((2 * H, 128)),
                small((2, GW)), small((1, GW))]
    args = [u, u, u, dt_t, p['ssm_conv_w'], p['ssm_conv_b'].reshape(1, 2 * GW),
            _pad_lanes(a_log), jnp.broadcast_to(a_log.reshape(2 * H, 1), (2 * H, 128)),
            _pad_lanes(dt_bias), jnp.broadcast_to(dt_bias.reshape(2 * H, 1), (2 * H, 128)),
            jnp.repeat(p['ssm_d'], SSM_P, axis=1), p['ssm_norm'].reshape(1, GW)]
    if has_h0:
        in_specs.append(pl.BlockSpec((None, None, 2, H, SSM_P, SSM_N), lambda b: (b, layer, 0, 0, 0, 0)))
        args.append(h0)
    out_specs = [pl.BlockSpec((L, GW), lambda b: (b, 0))]
    out_shape = [jax.ShapeDtypeStruct((nb * L, GW), bf16)]
    if want_state:
        out_specs.append(pl.BlockSpec((None, 2, H, SSM_P, SSM_N), lambda b: (b, 0, 0, 0, 0)))
        out_shape.append(jax.ShapeDtypeStruct((nb, 2, H, SSM_P, SSM_N), f32))
    res = pl.pallas_call(
        functools.partial(_ssm_kernel, L=L, has_h0=has_h0, want_state=want_state),
        grid=(nb,),
        in_specs=in_specs,
        out_specs=out_specs,
        out_shape=out_shape,
        scratch_shapes=[pltpu.VMEM((L + 16, 2 * GW), f32),
                        pltpu.VMEM((L, 2 * GW), f32),
                        pltpu.VMEM((L, GW), f32),
                        pltpu.VMEM((2, SSM_GROUPS, H // SSM_GROUPS * SSM_P, SSM_N), f32),
                        pltpu.VMEM((2, L, GW), f32),
                        pltpu.VMEM((2, L // CHUNK, SSM_GROUPS, H // SSM_GROUPS * SSM_P, SSM_N), f32),
                        pltpu.VMEM((L // CHUNK, 8, 128), f32)],
        compiler_params=_cparams(("arbitrary",)),
        name="ssd_mixer",
    )(*args)
    return res if want_state else (res[0], None)


def _ctx_att_kernel(sink_ref, q_ref, k_ref, v_ref, o_ref, *, L):
    G = ATT_HEADS // ATT_KV
    scale = ATT_D ** -0.5
    kb = [k_ref[:, j * ATT_D:(j + 1) * ATT_D].astype(bf16) for j in range(ATT_KV)]
    vb = [v_ref[:, j * ATT_D:(j + 1) * ATT_D].astype(bf16) for j in range(ATT_KV)]
    s = [_nt((q_ref[:, h * ATT_D:(h + 1) * ATT_D] * scale).astype(bf16), kb[h // G])
         for h in range(ATT_HEADS)]
    p, den = [], []
    for h in range(ATT_HEADS):
        sink = sink_ref[h]
        m = jnp.maximum(jnp.max(s[h], axis=-1, keepdims=True), sink)
        e = jnp.exp(s[h] - m)
        den.append(jnp.sum(e, axis=-1, keepdims=True) + jnp.exp(sink - m))
        p.append(e.astype(bf16))
    o = [_nn(p[h], vb[h // G]) for h in range(ATT_HEADS)]
    for h in range(ATT_HEADS):
        o_ref[:, h * ATT_D:(h + 1) * ATT_D] = (o[h] / den[h]).astype(o_ref.dtype)


def _ctx_attention(u, sink, nb, L):
    return pl.pallas_call(
        functools.partial(_ctx_att_kernel, L=L),
        grid=(nb,),
        in_specs=[pl.BlockSpec(memory_space=pltpu.SMEM),
                  pl.BlockSpec((L, GW), lambda b: (b, C_AQ // GW)),
                  pl.BlockSpec((L, 128), lambda b: (b, C_AK // 128)),
                  pl.BlockSpec((L, 128), lambda b: (b, C_AV // 128))],
        out_specs=pl.BlockSpec((L, GW), lambda b: (b, 0)),
        out_shape=jax.ShapeDtypeStruct((nb * L, GW), bf16),
        compiler_params=_cparams(("arbitrary",)),
        name="ctx_attention",
    )(sink, u, u, u)


def _rope(x, cos, sin):
    w = x.shape[1]
    lane = lax.broadcasted_iota(jnp.int32, x.shape, 1)
    first = (lane % 32) < 16
    rot = jnp.where(first, -pltpu.roll(x, w - 16, 1), pltpu.roll(x, 16, 1))
    return x * cos + rot * sin


def _half_variants(x, keep_fill):
    lane = lax.broadcasted_iota(jnp.int32, x.shape, 1)
    lo = lane < ATT_D
    a0 = jnp.where(lo, x, keep_fill)
    b1 = jnp.where(lo, keep_fill, x)
    xr = pltpu.roll(x, ATT_D, 1)
    a1 = jnp.where(lo, xr, keep_fill)
    b0 = jnp.where(lo, keep_fill, xr)
    return ((a0, b0), (a1, b1))


def _half_variants_t(xt):
    sub = lax.broadcasted_iota(jnp.int32, xt.shape, 0)
    lo = sub < ATT_D
    a0 = jnp.where(lo, xt, 0.0)
    b1 = jnp.where(lo, 0.0, xt)
    xr = pltpu.roll(xt, ATT_D, 0)
    a1 = jnp.where(lo, xr, 0.0)
    b0 = jnp.where(lo, 0.0, xr)
    return ((a0, b0), (a1, b1))


def _lat_att_kernel(sink_ref, q_ref, k_ref, v_ref, kc_ref, vc_ref, cos_ref, sin_ref, o_ref,
                    qr_scr, kt_scr, vv_scr, kct_scr, vcv_scr, bias_scr, *, L):
    G = ATT_HEADS // ATT_KV
    B = CHUNK
    nb = L // B
    scale = ATT_D ** -0.5
    zeros = jnp.zeros((B, 128), bf16)
    for j in range(ATT_KV):
        for hf in range(2):
            kt_scr[j, hf, 0] = zeros
            kt_scr[j, hf, nb + 1] = zeros
            vv_scr[j, hf, 0:B, :] = zeros
            vv_scr[j, hf, B + L:2 * B + L, :] = zeros
    kct = _half_variants_t(kc_ref[...].T)
    vcv = _half_variants(vc_ref[...], 1.0)
    for j in range(ATT_KV):
        for hf in range(2):
            kct_scr[j, hf] = kct[j][hf].astype(bf16)
            vcv_scr[j, hf] = vcv[j][hf].astype(bf16)

    rowi = lax.broadcasted_iota(jnp.int32, (2 * B, 3 * B), 0) % B
    coli = lax.broadcasted_iota(jnp.int32, (2 * B, 3 * B), 1)
    inwin = jnp.abs(rowi - (coli - B)) <= ATT_WIN
    ninf = jnp.float32(-jnp.inf)
    bias_scr[0] = jnp.where(inwin & (coli >= B), 0.0, ninf)
    bias_scr[1] = jnp.where(inwin, 0.0, ninf)
    bias_scr[2] = jnp.where(inwin & (coli < 2 * B), 0.0, ninf)

    def prep(n, carry):
        r0 = pl.multiple_of(n * B, B)
        cos = cos_ref[pl.ds(r0, B), :]
        sin = sin_ref[pl.ds(r0, B), :]
        cos4 = jnp.concatenate([cos] * 4, axis=1)
        sin4 = jnp.concatenate([sin] * 4, axis=1)
        qr_scr[pl.ds(r0, B), :] = (_rope(q_ref[pl.ds(r0, B), :], cos4, sin4) * scale).astype(bf16)
        kvar = _half_variants_t(_rope(k_ref[pl.ds(r0, B), :], cos, sin).T)
        vvar = _half_variants(v_ref[pl.ds(r0, B), :], 1.0)
        for j in range(ATT_KV):
            for hf in range(2):
                kt_scr[j, hf, n + 1] = kvar[j][hf].astype(bf16)
                vv_scr[j, hf, pl.ds(B + r0, B), :] = vvar[j][hf].astype(bf16)
        return carry

    lax.fori_loop(0, nb, prep, 0)

    lane_lo = lax.broadcasted_iota(jnp.int32, (2 * B, 128), 1) < ATT_D

    combos = [(j, hf) for j in range(ATT_KV) for hf in range(2)]

    def scores(n):
        r0 = pl.multiple_of(n * B, B)
        qs = [jnp.concatenate([qr_scr[pl.ds(r0, B), (2 * j) * 128:(2 * j + 1) * 128],
                               qr_scr[pl.ds(r0, B), (2 * j + 1) * 128:(2 * j + 2) * 128]], axis=0)
              for j in range(ATT_KV)]
        s_c = [_nn(qs[j], kct_scr[j, hf]) for j, hf in combos]
        s_b = [jnp.concatenate([_nn(qs[j], kt_scr[j, hf, n + t]) for t in range(3)], axis=1)
               for j, hf in combos]
        return s_c, s_b

    def finish(n, s_c, s_b):
        r0 = pl.multiple_of(n * B, B)
        bias = bias_scr[jnp.where(n == 0, 0, jnp.where(n == nb - 1, 2, 1))]
        p_c, p_b, esink = [], [], []
        for i, (j, hf) in enumerate(combos):
            sb = s_b[i] + bias
            sink = jnp.concatenate([jnp.full((B, 1), sink_ref[G * j + hf], f32),
                                    jnp.full((B, 1), sink_ref[G * j + 2 + hf], f32)], axis=0)
            m = jnp.maximum(jnp.maximum(jnp.max(s_c[i], axis=-1, keepdims=True),
                                        jnp.max(sb, axis=-1, keepdims=True)), sink)
            p_c.append(jnp.exp(s_c[i] - m).astype(bf16))
            p_b.append(jnp.exp(sb - m).astype(bf16))
            esink.append(jnp.exp(sink - m))
        oe = [_nn(p_c[i], vcv_scr[j, hf]) + _nn(p_b[i], vv_scr[j, hf, pl.ds(r0, 3 * B), :])
              for i, (j, hf) in enumerate(combos)]
        outs = []
        for i, (j, hf) in enumerate(combos):
            den = oe[i][:, (1 - hf) * ATT_D:(1 - hf) * ATT_D + 1] + esink[i]
            outs.append(oe[i] / den)
        for j in range(ATT_KV):
            o = jnp.where(lane_lo, outs[2 * j], outs[2 * j + 1])
            o_ref[pl.ds(r0, B), (2 * j) * 128:(2 * j + 1) * 128] = o[0:B].astype(o_ref.dtype)
            o_ref[pl.ds(r0, B), (2 * j + 1) * 128:(2 * j + 2) * 128] = o[B:2 * B].astype(o_ref.dtype)

    def blk_pair(i, carry):
        sa = scores(2 * i)
        sb = scores(2 * i + 1)
        finish(2 * i, *sa)
        finish(2 * i + 1, *sb)
        return carry

    lax.fori_loop(0, nb // 2, blk_pair, 0)


def _rope_tables(L):
    pos = jnp.arange(L)
    rows = (pos // GRID_W).astype(f32)
    cols = (pos % GRID_W).astype(f32)
    half = ATT_D // 4
    freqs = ROPE_THETA ** (-jnp.arange(half, dtype=f32) / half)
    ang_r = rows[:, None] * freqs
    ang_c = cols[:, None] * freqs
    ang = jnp.concatenate([ang_r, ang_r, ang_c, ang_c], axis=1)
    ang = jnp.concatenate([ang, ang], axis=1)
    return jnp.cos(ang), jnp.sin(ang)


def _lat_attention(u, k_cache, v_cache, sink, layer, nb, L):
    Lc = k_cache.shape[2]
    cos, sin = _rope_tables(L)
    kc = k_cache.reshape(nb, DEPTH, Lc, ATT_KV * ATT_D)
    vc = v_cache.reshape(nb, DEPTH, Lc, ATT_KV * ATT_D)
    cspec = pl.BlockSpec((None, None, Lc, 128), lambda b: (b, layer, 0, 0))
    return pl.pallas_call(
        functools.partial(_lat_att_kernel, L=L),
        grid=(nb,),
        in_specs=[pl.BlockSpec(memory_space=pltpu.SMEM),
                  pl.BlockSpec((L, GW), lambda b: (b, C_AQ // GW)),
                  pl.BlockSpec((L, 128), lambda b: (b, C_AK // 128)),
                  pl.BlockSpec((L, 128), lambda b: (b, C_AV // 128)),
                  cspec, cspec,
                  pl.BlockSpec((L, 128), lambda b: (0, 0)),
                  pl.BlockSpec((L, 128), lambda b: (0, 0))],
        out_specs=pl.BlockSpec((L, GW), lambda b: (b, 0)),
        out_shape=jax.ShapeDtypeStruct((nb * L, GW), bf16),
        scratch_shapes=[pltpu.VMEM((L, GW), bf16),
                        pltpu.VMEM((ATT_KV, 2, L // CHUNK + 2, 128, CHUNK), bf16),
                        pltpu.VMEM((ATT_KV, 2, L + 2 * CHUNK, 128), bf16),
                        pltpu.VMEM((ATT_KV, 2, 128, Lc), bf16),
                        pltpu.VMEM((ATT_KV, 2, Lc, 128), bf16),
                        pltpu.VMEM((3, 2 * CHUNK, 3 * CHUNK), f32)],
        compiler_params=_cparams(("arbitrary",)),
        name="lat_attention",
    )(sink, u, u, u, kc, vc, cos, sin)


def _permute_w_in_kernel(wt_ref, o_ref):
    ndt = 2 * SSM_HEADS
    moves = [(0, C_CONV, 1024), (1536, C_XBC, 1024), (1024, C_Z, GW), (2560 + ndt, C_RQ, C_DT - C_RQ)]
    for src, dst, n in moves:
        for r in range(0, n, 128):
            o_ref[:, dst + r:dst + r + 128] = wt_ref[src + r:src + r + 128, :].T.astype(bf16)
    tail = wt_ref[2560:2560 + 128, :].T
    lane = lax.broadcasted_iota(jnp.int32, tail.shape, 1)
    o_ref[:, C_DT:UW] = jnp.where(lane < ndt, tail, 0.0).astype(bf16)


def _permute_w_in(w):
    tr = 256
    return pl.pallas_call(
        _permute_w_in_kernel,
        grid=(DEPTH, D // tr),
        in_specs=[pl.BlockSpec((None, N_IN, tr), lambda l, i: (l, 0, i))],
        out_specs=pl.BlockSpec((None, tr, UW), lambda l, i: (l, i, 0)),
        out_shape=jax.ShapeDtypeStruct((DEPTH, D, UW), bf16),
        compiler_params=_cparams(("arbitrary", "arbitrary")),
        name="permute_w_in",
    )(jnp.swapaxes(w, 1, 2))


def _layer(x, mod, p, ctx, layer, nb, L, final):
    per_seq = ctx is not None
    u = _inproj(x, p['norm_mix'], mod, p['w_in_p'], layer, L, per_seq)
    dt_t = u[:, C_DT:C_DT + 2 * SSM_HEADS].T
    want_state = ctx is None
    o_conv = _conv_module(u, p['conv_w'], p['conv_b'], p['conv_ln_g'], p['conv_ln_b'], nb, L)
    if ctx is None:
        o_ssm, h_ssm = _ssm(u, dt_t, p, None, layer, nb, L, True)
        o_ret, h_ret = _retention(u, p['ret_log_decay'], p['ret_gn_g'], None, layer, nb, L, True)
        o_att = _ctx_attention(u, p['att_sink'], nb, L)
    else:
        k_c, v_c, s_ssm, s_ret = ctx
        o_ssm, h_ssm = _ssm(u, dt_t, p, s_ssm, layer, nb, L, False)
        o_ret, h_ret = _retention(u, p['ret_log_decay'], p['ret_gn_g'], s_ret, layer, nb, L, False)
        o_att = _lat_attention(u, k_c, v_c, p['att_sink'], layer, nb, L)
    x1 = _outproj(o_conv, o_ssm, o_ret, o_att, p['w_out_b'], layer, x, mod, L, per_seq)
    x2 = _mlp(x1, p['norm_mlp'], mod, p['final_norm'], p['w1_b'], p['w2_b'], layer, L, per_seq, final)
    states = None
    if want_state:
        ak = u[:, C_AK:C_AK + 128].reshape(nb, L, ATT_KV, ATT_D)
        av = u[:, C_AV:C_AV + 128].reshape(nb, L, ATT_KV, ATT_D)
        states = (ak, av, h_ssm, h_ret)
    return x2, states


def kernel(x_prompt, x_sample, cache_attn_k, cache_attn_v, state_ssm, state_ret, c, c_ctx, ada_w, ada_b, norm_mix, norm_mlp, w_in, conv_w, conv_b, conv_ln_g, conv_ln_b, ssm_conv_w, ssm_conv_b, ssm_a_log, ssm_dt_bias, ssm_d, ssm_norm, ret_log_decay, ret_gn_g, att_sink, w_out, w1, w2, final_norm):
    nbp, Lp, _ = x_prompt.shape
    nbs, Ls, _ = x_sample.shape
    cvec = jnp.concatenate([c_ctx[None, :], c, jnp.zeros((16 - 1 - nbs, D), f32)], axis=0)
    mod = _ada_mod(cvec, ada_w, ada_b)
    y_p = x_prompt.reshape(nbp * Lp, D)
    y_s = x_sample.reshape(nbs * Ls, D)
    new_k, new_v, new_ssm, new_ret = [], [], [], []
    w_in_p = _permute_w_in(w_in)
    w_out_b = w_out.astype(bf16)
    w1_b = w1.astype(bf16)
    w2_b = w2.astype(bf16)
    for l in range(DEPTH):
        p = dict(norm_mix=norm_mix[l], norm_mlp=norm_mlp[l], w_in_p=w_in_p,
                 conv_w=conv_w[l], conv_b=conv_b[l], conv_ln_g=conv_ln_g[l], conv_ln_b=conv_ln_b[l],
                 ssm_conv_w=ssm_conv_w[l], ssm_conv_b=ssm_conv_b[l], ssm_a_log=ssm_a_log[l],
                 ssm_dt_bias=ssm_dt_bias[l], ssm_d=ssm_d[l], ssm_norm=ssm_norm[l],
                 ret_log_decay=ret_log_decay[l], ret_gn_g=ret_gn_g[l], att_sink=att_sink[l],
                 w_out_b=w_out_b, w1_b=w1_b, w2_b=w2_b, final_norm=final_norm)
        mod_l = mod[l].reshape(16, 6, 1, D)
        final = l == DEPTH - 1
        y_p, (k_l, v_l, hs_l, hr_l) = _layer(y_p, mod_l[0:1], p, None, l, nbp, Lp, final)
        new_k.append(k_l)
        new_v.append(v_l)
        new_ssm.append(hs_l)
        new_ret.append(hr_l)
        ctx = (cache_attn_k, cache_attn_v, state_ssm, state_ret)
        y_s, _ = _layer(y_s, mod_l[1:1 + nbs], p, ctx, l, nbs, Ls, final)
    return (y_p.reshape(nbp, Lp, D), y_s.reshape(nbs, Ls, D),
            jnp.stack(new_k, axis=1), jnp.stack(new_v, axis=1),
            jnp.stack(new_ssm, axis=1), jnp.stack(new_ret, axis=1))
```

```python
import functools

import jax
import jax.numpy as jnp
from jax import lax
from jax.experimental import pallas as pl
from jax.experimental.pallas import tpu as pltpu

f32 = jnp.float32
bf16 = jnp.bfloat16

D = 2048
DEPTH = 2
GW = 512
CONV_K = 31
SSM_HEADS, SSM_P, SSM_N, SSM_GROUPS, SSM_K = 8, 64, 128, 2, 5
RET_HEADS, RET_D = 4, 128
ATT_HEADS, ATT_KV, ATT_D, ATT_WIN = 8, 2, 64, 128
GRID_W = 64
ROPE_THETA = 10000.0
CHUNK = 128
D_FF = 4 * D
EPS = 1e-6

C_CONV, C_XBC, C_Z, C_RQ, C_RK, C_RV, C_RG, C_AQ, C_AK, C_AV, C_DT = (
    0, 1024, 2048, 2560, 3072, 3584, 4096, 4608, 5120, 5248, 5376)
UW = 5504
N_IN = 5392
N_CHUNK = 512

VMEM_LIMIT = 56 * 1024 * 1024


def _cparams(sem):
    return pltpu.CompilerParams(dimension_semantics=sem, vmem_limit_bytes=VMEM_LIMIT)


def _sigmoid(x):
    return 1.0 / (1.0 + jnp.exp(-x))


def _silu(x):
    return x * _sigmoid(x)


def _nt(a, b):
    return lax.dot_general(a, b, (((1,), (1,)), ((), ())), preferred_element_type=f32)


def _tn(a, b):
    return lax.dot_general(a, b, (((0,), (0,)), ((), ())), preferred_element_type=f32)


def _nn(a, b):
    return jnp.dot(a, b, preferred_element_type=f32)


def _split3(x):
    hi = x.astype(bf16)
    r1 = x - hi.astype(f32)
    mid = r1.astype(bf16)
    lo = (r1 - mid.astype(f32)).astype(bf16)
    return hi, mid, lo


def _ada_kernel(c_ref, w_ref, b_ref, o_ref):
    c = c_ref[...]
    s = _silu(c).astype(bf16)
    o_ref[...] = _nn(s, w_ref[...].astype(bf16)) + b_ref[...]


def _ada_mod(cvec, ada_w, ada_b):
    tn = 1024
    return pl.pallas_call(
        _ada_kernel,
        grid=(DEPTH, 6 * D // tn),
        in_specs=[pl.BlockSpec((16, D), lambda l, j: (0, 0)),
                  pl.BlockSpec((None, D, tn), lambda l, j: (l, 0, j)),
                  pl.BlockSpec((None, 1, tn), lambda l, j: (l, 0, j))],
        out_specs=pl.BlockSpec((None, 16, tn), lambda l, j: (l, 0, j)),
        out_shape=jax.ShapeDtypeStruct((DEPTH, 16, 6 * D), f32),
        compiler_params=_cparams(("arbitrary", "arbitrary")),
        name="ada_mod",
    )(cvec, ada_w, ada_b.reshape(DEPTH, 1, 6 * D))


def _mod_spec(idx, tm, L, per_seq):
    if per_seq:
        return pl.BlockSpec((None, None, 1, D), lambda i, *_: ((i * tm) // L, idx, 0, 0))
    return pl.BlockSpec((None, None, 1, D), lambda i, *_: (0, idx, 0, 0))


def _inproj_kernel(x_ref, g_ref, sc_ref, sh_ref, w_ref, *refs, n_cast):
    cast_in, u_ref, cast_out = refs[:n_cast], refs[n_cast], refs[n_cast + 1:]
    x = x_ref[...]
    ms = jnp.mean(x * x, axis=-1, keepdims=True)
    h = x * lax.rsqrt(ms + EPS) * g_ref[...] * (1.0 + sc_ref[...]) + sh_ref[...]
    hb = h.astype(bf16)
    for n0 in range(0, UW, N_CHUNK):
        n1 = min(n0 + N_CHUNK, UW)
        u_ref[:, n0:n1] = _nn(hb, w_ref[:, n0:n1])
    for src, dst in zip(cast_in, cast_out):
        dst[...] = src[...].astype(bf16)


def _inproj(x, norm_g, mod, w_in_p, layer, L, per_seq, casts):
    T = x.shape[0]
    tm = 256
    n = T // tm
    cast_in_specs, cast_out_specs, cast_shapes = [], [], []
    for w in casts:
        _, R, C = w.shape
        cast_in_specs.append(pl.BlockSpec((None, R // n, C), lambda i: (layer, i, 0)))
        cast_out_specs.append(pl.BlockSpec((R // n, C), lambda i: (i, 0)))
        cast_shapes.append(jax.ShapeDtypeStruct((R, C), bf16))
    res = pl.pallas_call(
        functools.partial(_inproj_kernel, n_cast=len(casts)),
        grid=(n,),
        in_specs=[pl.BlockSpec((tm, D), lambda i: (i, 0)),
                  pl.BlockSpec((1, D), lambda i: (0, 0)),
                  _mod_spec(1, tm, L, per_seq),
                  _mod_spec(0, tm, L, per_seq),
                  pl.BlockSpec((None, D, UW), lambda i: (layer, 0, 0), pipeline_mode=pl.Buffered(1))]
        + cast_in_specs,
        out_specs=[pl.BlockSpec((tm, UW), lambda i: (i, 0))] + cast_out_specs,
        out_shape=[jax.ShapeDtypeStruct((T, UW), f32)] + cast_shapes,
        compiler_params=_cparams(("arbitrary",)),
        name="in_proj",
    )(x, norm_g.reshape(1, D), mod, mod, w_in_p, *casts)
    return res[0], res[1:]


def _outproj_kernel(oc_ref, os_ref, or_ref, oa_ref, w_ref, x_ref, g_ref, o_ref):
    for n0 in range(0, D, N_CHUNK):
        n1 = n0 + N_CHUNK
        acc = _nn(oc_ref[...], w_ref[0:GW, n0:n1])
        acc = acc + _nn(os_ref[...], w_ref[GW:2 * GW, n0:n1])
        acc = acc + _nn(or_ref[...], w_ref[2 * GW:3 * GW, n0:n1])
        acc = acc + _nn(oa_ref[...], w_ref[3 * GW:4 * GW, n0:n1])
        o_ref[:, n0:n1] = x_ref[:, n0:n1] + g_ref[:, n0:n1] * acc


def _outproj(o_conv, o_ssm, o_ret, o_att, w_out_b, x, mod, L, per_seq):
    T = x.shape[0]
    tm = 512
    ospec = pl.BlockSpec((tm, GW), lambda i: (i, 0))
    return pl.pallas_call(
        _outproj_kernel,
        grid=(T // tm,),
        in_specs=[ospec, ospec, ospec, ospec,
                  pl.BlockSpec((D, D), lambda i: (0, 0), pipeline_mode=pl.Buffered(1)),
                  pl.BlockSpec((tm, D), lambda i: (i, 0)),
                  _mod_spec(2, tm, L, per_seq)],
        out_specs=pl.BlockSpec((tm, D), lambda i: (i, 0)),
        out_shape=jax.ShapeDtypeStruct((T, D), f32),
        compiler_params=_cparams(("arbitrary",)),
        name="out_proj",
    )(o_conv, o_ssm, o_ret, o_att, w_out_b, x, mod)


def _mlp_kernel(x_ref, g_ref, sc_ref, sh_ref, g2_ref, fn_ref, w1_ref, w2_ref, o_ref, h_scr, *, final):
    f = pl.program_id(1)
    tm = x_ref.shape[0]
    RB = 256

    @pl.when(f == 0)
    def _():
        for r in range(0, tm, RB):
            x = x_ref[r:r + RB, :]
            ms = jnp.mean(x * x, axis=-1, keepdims=True)
            h = x * lax.rsqrt(ms + EPS) * g_ref[...] * (1.0 + sc_ref[...]) + sh_ref[...]
            h_scr[r:r + RB, :] = h.astype(bf16)
        o_ref[...] = jnp.zeros_like(o_ref)

    for r in range(0, tm, 2 * RB):
        a = _nn(h_scr[r:r + 2 * RB, :], w1_ref[...])
        a = jnp.maximum(a, 0.0)
        a = (a * a).astype(bf16)
        o_ref[r:r + 2 * RB, :] += _nn(a, w2_ref[...])

    @pl.when(f == pl.num_programs(1) - 1)
    def _():
        for r in range(0, tm, RB):
            y = x_ref[r:r + RB, :] + g2_ref[...] * o_ref[r:r + RB, :]
            if final:
                ms = jnp.mean(y * y, axis=-1, keepdims=True)
                y = y * lax.rsqrt(ms + EPS) * fn_ref[...]
            o_ref[r:r + RB, :] = y


def _mlp(x, norm_g, mod, final_norm, w1_b, w2_b, L, per_seq, final):
    T = x.shape[0]
    tm, tf = 1024, 512
    return pl.pallas_call(
        functools.partial(_mlp_kernel, final=final),
        grid=(T // tm, D_FF // tf),
        in_specs=[pl.BlockSpec((tm, D), lambda i, f: (i, 0)),
                  pl.BlockSpec((1, D), lambda i, f: (0, 0)),
                  _mod_spec(4, tm, L, per_seq),
                  _mod_spec(3, tm, L, per_seq),
                  _mod_spec(5, tm, L, per_seq),
                  pl.BlockSpec((1, D), lambda i, f: (0, 0)),
                  pl.BlockSpec((D, tf), lambda i, f: (0, f)),
                  pl.BlockSpec((tf, D), lambda i, f: (f, 0))],
        out_specs=pl.BlockSpec((tm, D), lambda i, f: (i, 0)),
        out_shape=jax.ShapeDtypeStruct((T, D), f32),
        scratch_shapes=[pltpu.VMEM((tm, D), bf16)],
        compiler_params=_cparams(("arbitrary", "arbitrary")),
        name="mlp",
    )(x, norm_g.reshape(1, D), mod, mod, mod, final_norm.reshape(1, D), w1_b, w2_b)


def _conv_kernel(u_ref, w_ref, b_ref, lg_ref, lb_ref, o_ref, vpad_ref, ph_ref, *, L):
    RC = 64
    pad = 16
    SUB = 8
    off0 = pad - CONV_K // 2
    nph = RC + SUB * ((off0 + CONV_K - 1) // SUB)
    vpad_ref[0:pad, :] = jnp.zeros((pad, GW), f32)
    vpad_ref[pad + L:2 * pad + L, :] = jnp.zeros((pad, GW), f32)

    def fill(i, carry):
        r0 = pl.multiple_of(i * RC, RC)
        a = u_ref[pl.ds(r0, RC), 0:GW]
        g = u_ref[pl.ds(r0, RC), GW:2 * GW]
        vpad_ref[pl.ds(pad + r0, RC), :] = a * _sigmoid(g)
        return carry

    lax.fori_loop(0, L // RC, fill, 0)

    def body(i, carry):
        r0 = pl.multiple_of(i * RC, RC)
        win = vpad_ref.at[pl.ds(r0, RC + 2 * pad), :]
        accs = []
        for c0 in range(0, GW, 128):
            ls = slice(c0, c0 + 128)
            for ph in range(SUB):
                ph_ref[ph, :, ls] = win[ph:ph + nph, ls]
            acc = jnp.broadcast_to(b_ref[:, ls], (RC, 128))
            for k in range(CONV_K):
                a, ph = divmod(off0 + k, SUB)
                acc = acc + ph_ref[ph, SUB * a:SUB * a + RC, ls] * w_ref[k:k + 1, ls]
            accs.append(acc)
        acc = jnp.concatenate(accs, axis=1)
        mu = jnp.mean(acc, axis=-1, keepdims=True)
        xc = acc - mu
        var = jnp.mean(xc * xc, axis=-1, keepdims=True)
        y = xc * lax.rsqrt(var + EPS) * lg_ref[...] + lb_ref[...]
        o_ref[pl.ds(r0, RC), :] = _silu(y).astype(o_ref.dtype)
        return carry

    lax.fori_loop(0, L // RC, body, 0)


def _conv_module(u, conv_w, conv_b, ln_g, ln_b, nb, L):
    return pl.pallas_call(
        functools.partial(_conv_kernel, L=L),
        grid=(nb,),
        in_specs=[pl.BlockSpec((L, 2 * GW), lambda b: (b, C_CONV // (2 * GW))),
                  pl.BlockSpec((CONV_K, GW), lambda b: (0, 0)),
                  pl.BlockSpec((1, GW), lambda b: (0, 0)),
                  pl.BlockSpec((1, GW), lambda b: (0, 0)),
                  pl.BlockSpec((1, GW), lambda b: (0, 0))],
        out_specs=pl.BlockSpec((L, GW), lambda b: (b, 0)),
        out_shape=jax.ShapeDtypeStruct((nb * L, GW), bf16),
        scratch_shapes=[pltpu.VMEM((L + 32, GW), f32), pltpu.VMEM((8, 64 + 24, GW), f32)],
        compiler_params=_cparams(("arbitrary",)),
        name="conv_module",
    )(u, conv_w, conv_b.reshape(1, GW), ln_g.reshape(1, GW), ln_b.reshape(1, GW))


def _ret_kernel(*refs, L, has_h0, want_state):
    ld_ref, q_ref, k_ref, v_ref, g_ref, gn_ref = refs[:6]
    pos = 6
    h0_ref = None
    if has_h0:
        h0_ref = refs[pos]
        pos += 1
    o_ref = refs[pos]
    pos += 1
    hout_ref = None
    if want_state:
        hout_ref = refs[pos]
        pos += 1
    y_scr, st_scr, dec_scr, cs_scr = refs[pos:pos + 4]

    nc = L // CHUNK
    H = RET_HEADS
    row = lax.broadcasted_iota(jnp.int32, (CHUNK, CHUNK), 0)
    col = lax.broadcasted_iota(jnp.int32, (CHUNK, CHUNK), 1)
    rowf = row.astype(f32)
    diff = (row - col).astype(f32)
    kscale = RET_D ** -0.5
    sls = [slice(h * RET_D, (h + 1) * RET_D) for h in range(H)]

    for d in range(2):
        for h in range(H):
            if has_h0:
                st_scr[d, h] = h0_ref[d, h]
            else:
                st_scr[d, h] = jnp.zeros((RET_D, RET_D), f32)
    for h in range(H):
        laf = ld_ref[h]
        lab = ld_ref[H + h]
        dec_scr[h] = jnp.where(col < row, jnp.exp(diff * laf),
                               jnp.where(col > row, jnp.exp(-diff * lab), 2.0))


    def intra(c, carry):
        r0 = pl.multiple_of(c * CHUNK, CHUNK)
        qb = [q_ref[pl.ds(r0, CHUNK), sls[h]].astype(bf16) for h in range(H)]
        kb = [(k_ref[pl.ds(r0, CHUNK), sls[h]] * kscale).astype(bf16) for h in range(H)]
        vh = [v_ref[pl.ds(r0, CHUNK), sls[h]] for h in range(H)]
        s = [_nt(qb[h], kb[h]) for h in range(H)]
        cf = [_tn((vh[h] * jnp.exp((CHUNK - 1.0 - rowf) * ld_ref[h])).astype(bf16), kb[h])
              for h in range(H)]
        cb = [_tn((vh[h] * jnp.exp(rowf * ld_ref[H + h])).astype(bf16), kb[h]) for h in range(H)]
        m = [(s[h] * dec_scr[h]).astype(bf16) for h in range(H)]
        y = [_nn(m[h], vh[h].astype(bf16)) for h in range(H)]
        for h in range(H):
            y_scr[pl.ds(r0, CHUNK), sls[h]] = y[h]
            cs_scr[0, c, h] = cf[h]
            cs_scr[1, c, h] = cb[h]
        return carry

    lax.fori_loop(0, nc, intra, 0, unroll=2)

    def inter(d, c):
        r0 = pl.multiple_of(c * CHUNK, CHUNK)
        st = [st_scr[d, h] for h in range(H)]
        yi = [_nt(q_ref[pl.ds(r0, CHUNK), sls[h]].astype(bf16), st[h].astype(bf16)) for h in range(H)]
        for h in range(H):
            la = ld_ref[d * H + h]
            st_scr[d, h] = st[h] * jnp.exp(jnp.full((RET_D, RET_D), CHUNK * la, f32)) + cs_scr[d, c, h]
        return r0, yi

    def fwd(c, carry):
        r0, yi = inter(0, c)
        for h in range(H):
            y_scr[pl.ds(r0, CHUNK), sls[h]] += jnp.exp((rowf + 1.0) * ld_ref[h]) * yi[h]
        return carry

    lax.fori_loop(0, nc, fwd, 0, unroll=2)

    def bwd(i, carry):
        r0, yi = inter(1, nc - 1 - i)
        for h in range(H):
            sl = sls[h]
            y = y_scr[pl.ds(r0, CHUNK), sl] + jnp.exp((CHUNK - rowf) * ld_ref[H + h]) * yi[h]
            mu = jnp.mean(y, axis=-1, keepdims=True)
            yc = y - mu
            var = jnp.mean(yc * yc, axis=-1, keepdims=True)
            yn = yc * lax.rsqrt(var + EPS) * gn_ref[:, sl]
            g = g_ref[pl.ds(r0, CHUNK), sl]
            o_ref[pl.ds(r0, CHUNK), sl] = (_silu(g) * yn).astype(o_ref.dtype)
        return carry

    lax.fori_loop(0, nc, bwd, 0, unroll=2)

    if want_state:
        for d in range(2):
            for h in range(RET_HEADS):
                hout_ref[d, h] = st_scr[d, h]


def _retention(u, log_decay, gn_g, h0, layer, nb, L, want_state):
    has_h0 = h0 is not None
    cblk = lambda c: pl.BlockSpec((L, GW), lambda b, c=c: (b, c // GW))
    in_specs = [pl.BlockSpec(memory_space=pltpu.SMEM),
                cblk(C_RQ), cblk(C_RK), cblk(C_RV), cblk(C_RG),
                pl.BlockSpec((1, GW), lambda b: (0, 0))]
    args = [log_decay.reshape(2 * RET_HEADS), u, u, u, u, gn_g.reshape(1, GW)]
    if has_h0:
        in_specs.append(pl.BlockSpec((None, None, 2, RET_HEADS, RET_D, RET_D),
                                     lambda b: (b, layer, 0, 0, 0, 0)))
        args.append(h0)
    out_specs = [pl.BlockSpec((L, GW), lambda b: (b, 0))]
    out_shape = [jax.ShapeDtypeStruct((nb * L, GW), bf16)]
    if want_state:
        out_specs.append(pl.BlockSpec((None, 2, RET_HEADS, RET_D, RET_D), lambda b: (b, 0, 0, 0, 0)))
        out_shape.append(jax.ShapeDtypeStruct((nb, 2, RET_HEADS, RET_D, RET_D), f32))
    res = pl.pallas_call(
        functools.partial(_ret_kernel, L=L, has_h0=has_h0, want_state=want_state),
        grid=(nb,),
        in_specs=in_specs,
        out_specs=out_specs,
        out_shape=out_shape,
        scratch_shapes=[pltpu.VMEM((L, GW), f32),
                        pltpu.VMEM((2, RET_HEADS, RET_D, RET_D), f32),
                        pltpu.VMEM((RET_HEADS, CHUNK, CHUNK), f32),
                        pltpu.VMEM((2, L // CHUNK, RET_HEADS, RET_D, RET_D), f32)],
        compiler_params=_cparams(("arbitrary",)),
        name="retention",
    )(*args)
    return res if want_state else (res[0], None)


def _softplus(x):
    return jnp.maximum(x, 0.0) + jnp.log(1.0 + jnp.exp(-jnp.abs(x)))


def _ssm_kernel(*refs, L, has_h0, want_state):
    (xbc_ref, z_ref, dtc_ref, dtr_ref, cw_ref, cb_ref, alr_ref, alc_ref, dbr_ref, dbc_ref,
     dsk_ref, ng_ref) = refs[:12]
    pos = 12
    h0_ref = None
    if has_h0:
        h0_ref = refs[pos]
        pos += 1
    o_ref = refs[pos]
    pos += 1
    hout_ref = None
    if want_state:
        hout_ref = refs[pos]
        pos += 1
    xpad_ref, xc_scr, y_scr, st_scr, es_scr, cs_scr, tot_scr = refs[pos:pos + 7]

    nc = L // CHUNK
    H, P, N = SSM_HEADS, SSM_P, SSM_N
    HG = H // SSM_GROUPS
    pad = 8
    row = lax.broadcasted_iota(jnp.int32, (CHUNK, CHUNK), 0)
    col = lax.broadcasted_iota(jnp.int32, (CHUNK, CHUNK), 1)
    lt01 = (row >= col).astype(bf16)
    ut01 = (row <= col).astype(bf16)
    sub16 = lax.broadcasted_iota(jnp.int32, (2 * H, CHUNK), 0)
    lane_lo = col < P

    xpad_ref[0:pad, :] = jnp.zeros((pad, 2 * GW), f32)
    xpad_ref[pad + L:2 * pad + L, :] = jnp.zeros((pad, 2 * GW), f32)

    def fill(i, carry):
        r0 = pl.multiple_of(i * CHUNK, CHUNK)
        xpad_ref[pl.ds(pad + r0, CHUNK), :] = xbc_ref[pl.ds(r0, CHUNK), :]
        return carry

    lax.fori_loop(0, nc, fill, 0)

    for d in range(2):
        for h in range(H):
            g, hh = divmod(h, HG)
            if has_h0:
                st_scr[d, g, hh * P:(hh + 1) * P, :] = h0_ref[d, h]
            else:
                st_scr[d, g, hh * P:(hh + 1) * P, :] = jnp.zeros((P, N), f32)

    a_neg_r = -jnp.exp(alr_ref[...])
    a_neg_c = -jnp.exp(alc_ref[...])

    def per_head_lanes(v, base):
        cols = []
        for c2 in range(H // 2):
            a = jnp.broadcast_to(v[:, base + 2 * c2:base + 2 * c2 + 1], (CHUNK, 2 * P))
            b = jnp.broadcast_to(v[:, base + 2 * c2 + 1:base + 2 * c2 + 2], (CHUNK, 2 * P))
            cols.append(jnp.where(lane_lo, a, b))
        return jnp.concatenate(cols, axis=1)

    def grp(x, g, base):
        return x[:, base + g * N:base + (g + 1) * N]


    def intra(c, carry):
        r0 = pl.multiple_of(c * CHUNK, CHUNK)
        win = xpad_ref.at[pl.ds(r0, CHUNK + 2 * pad), :]
        off = pad - SSM_K // 2
        for c0 in range(0, 2 * GW, 128):
            ls = slice(c0, c0 + 128)
            acc = jnp.broadcast_to(cb_ref[:, ls], (CHUNK, 128))
            for k in range(SSM_K):
                acc = acc + win[off + k:off + k + CHUNK, ls] * cw_ref[k:k + 1, ls]
            xc_scr[pl.ds(r0, CHUNK), ls] = _silu(acc)
        xc = xc_scr[pl.ds(r0, CHUNK), :]
        xs = xc[:, 0:GW]
        bmb = [grp(xc, g, GW).astype(bf16) for g in range(SSM_GROUPS)]
        cmb = [grp(xc, g, GW + SSM_GROUPS * N).astype(bf16) for g in range(SSM_GROUPS)]
        gmat = [_nt(cmb[g], bmb[g]) for g in range(SSM_GROUPS)]

        dt_c = _softplus(dtc_ref[pl.ds(r0, CHUNK), :] + dbr_ref[...])
        dt_r = _softplus(dtr_ref[:, pl.ds(r0, CHUNK)] + dbc_ref[...])
        lc3 = _split3(dt_c * a_neg_r)
        lr3 = _split3(dt_r * a_neg_c)
        pc = [_nn(lt01, t) for t in lc3]
        sc = [_nn(ut01, t) for t in lc3]
        pr = [_nn(t, ut01) for t in lr3]
        sr = [_nn(t, lt01) for t in lr3]
        a_c = jnp.where(col < H, pc[0] + pc[1] + pc[2], sc[0] + sc[1] + sc[2])
        a_r = jnp.where(sub16 < H, pr[0] + pr[1] + pr[2], sr[0] + sr[1] + sr[2])
        tot = jnp.where(col[0:1, :] < H, a_c[CHUNK - 1:CHUNK, :], a_c[0:1, :])
        tot_scr[c] = jnp.broadcast_to(tot, (8, 128))
        ks = dt_c * jnp.exp(tot - a_c)
        es = jnp.exp(a_c)
        es_scr[0, pl.ds(r0, CHUNK), :] = per_head_lanes(es, 0)
        es_scr[1, pl.ds(r0, CHUNK), :] = per_head_lanes(es, H)
        xwf = (xs * per_head_lanes(ks, 0)).astype(bf16)
        xwb = (xs * per_head_lanes(ks, H)).astype(bf16)
        csf = [_tn(xwf[:, g * HG * P:(g + 1) * HG * P], bmb[g]) for g in range(SSM_GROUPS)]
        csb = [_tn(xwb[:, g * HG * P:(g + 1) * HG * P], bmb[g]) for g in range(SSM_GROUPS)]
        for g in range(SSM_GROUPS):
            cs_scr[0, c, g] = csf[g]
            cs_scr[1, c, g] = csb[g]

        ms = []
        for h in range(H):
            df = jnp.exp(jnp.minimum(a_c[:, h:h + 1] - a_r[h:h + 1, :], 0.0)) * dt_r[h:h + 1, :]
            db = jnp.exp(jnp.minimum(a_c[:, H + h:H + h + 1] - a_r[H + h:H + h + 1, :], 0.0)) \
                * dt_r[H + h:H + h + 1, :]
            dm = jnp.where(col <= row, df, 0.0) + jnp.where(col >= row, db, 0.0)
            ms.append((gmat[h // HG] * dm).astype(bf16))
        ys = []
        for c2 in range(H // 2):
            xcol = xs[:, c2 * 2 * P:(c2 + 1) * 2 * P]
            x_lo = jnp.where(lane_lo, xcol, 0.0).astype(bf16)
            x_hi = jnp.where(lane_lo, 0.0, xcol).astype(bf16)
            ys.append(_nn(ms[2 * c2], x_lo) + _nn(ms[2 * c2 + 1], x_hi))
        y = jnp.concatenate(ys, axis=1)
        y_scr[pl.ds(r0, CHUNK), :] = y + (dsk_ref[0:1, :] + dsk_ref[1:2, :]) * xs
        return carry

    lax.fori_loop(0, nc, intra, 0, unroll=2)

    def inter(d, c):
        r0 = pl.multiple_of(c * CHUNK, CHUNK)
        st = [st_scr[d, g] for g in range(SSM_GROUPS)]
        yi = [_nt(grp(xc_scr[pl.ds(r0, CHUNK), :], g, GW + SSM_GROUPS * N).astype(bf16),
                  st[g].astype(bf16)) for g in range(SSM_GROUPS)]
        tot = tot_scr[c]
        for g in range(SSM_GROUPS):
            dec = jnp.concatenate(
                [jnp.broadcast_to(jnp.exp(tot[0:1, d * H + g * HG + hh:d * H + g * HG + hh + 1]), (P, N))
                 for hh in range(HG)], axis=0)
            st_scr[d, g] = st[g] * dec + cs_scr[d, c, g]
        return r0, jnp.concatenate(yi, axis=1) * es_scr[d, pl.ds(r0, CHUNK), :]

    def fwd(c, carry):
        r0, yi = inter(0, c)
        y_scr[pl.ds(r0, CHUNK), :] += yi
        return carry

    lax.fori_loop(0, nc, fwd, 0)

    def bwd(i, carry):
        r0, yi = inter(1, nc - 1 - i)
        yz = (y_scr[pl.ds(r0, CHUNK), :] + yi) * _silu(z_ref[pl.ds(r0, CHUNK), :])
        ms = jnp.mean(yz * yz, axis=-1, keepdims=True)
        o_ref[pl.ds(r0, CHUNK), :] = (yz * lax.rsqrt(ms + EPS) * ng_ref[...]).astype(o_ref.dtype)
        return carry

    lax.fori_loop(0, nc, bwd, 0)

    if want_state:
        for d in range(2):
            for h in range(H):
                g, hh = divmod(h, HG)
                hout_ref[d, h] = st_scr[d, g, hh * P:(hh + 1) * P, :]


def _pad_lanes(v, n=128):
    return jnp.pad(v, ((0, 0), (0, n - v.shape[1])))


def _ssm(u, dt_t, p, h0, layer, nb, L, want_state):
    has_h0 = h0 is not None
    H = SSM_HEADS
    a_log = p['ssm_a_log'].reshape(1, 2 * H)
    dt_bias = p['ssm_dt_bias'].reshape(1, 2 * H)
    small = lambda shape: pl.BlockSpec(shape, lambda b: (0, 0))
    in_specs = [pl.BlockSpec((L, 2 * GW), lambda b: (b, C_XBC // (2 * GW))),
                pl.BlockSpec((L, GW), lambda b: (b, C_Z // GW)),
                pl.BlockSpec((L, 128), lambda b: (b, C_DT // 128)),
                pl.BlockSpec((2 * H, L), lambda b: (0, b)),
                small((SSM_K, 2 * GW)), small((1, 2 * GW)),
                small((1, 128)), small((2 * H, 128)), small((1, 128)), small((2 * H, 128)),
                small((2, GW)), small((1, GW))]
    args = [u, u, u, dt_t, p['ssm_conv_w'], p['ssm_conv_b'].reshape(1, 2 * GW),
            _pad_lanes(a_log), jnp.broadcast_to(a_log.reshape(2 * H, 1), (2 * H, 128)),
            _pad_lanes(dt_bias), jnp.broadcast_to(dt_bias.reshape(2 * H, 1), (2 * H, 128)),
            jnp.repeat(p['ssm_d'], SSM_P, axis=1), p['ssm_norm'].reshape(1, GW)]
    if has_h0:
        in_specs.append(pl.BlockSpec((None, None, 2, H, SSM_P, SSM_N), lambda b: (b, layer, 0, 0, 0, 0)))
        args.append(h0)
    out_specs = [pl.BlockSpec((L, GW), lambda b: (b, 0))]
    out_shape = [jax.ShapeDtypeStruct((nb * L, GW), bf16)]
    if want_state:
        out_specs.append(pl.BlockSpec((None, 2, H, SSM_P, SSM_N), lambda b: (b, 0, 0, 0, 0)))
        out_shape.append(jax.ShapeDtypeStruct((nb, 2, H, SSM_P, SSM_N), f32))
    res = pl.pallas_call(
        functools.partial(_ssm_kernel, L=L, has_h0=has_h0, want_state=want_state),
        grid=(nb,),
        in_specs=in_specs,
        out_specs=out_specs,
        out_shape=out_shape,
        scratch_shapes=[pltpu.VMEM((L + 16, 2 * GW), f32),
                        pltpu.VMEM((L, 2 * GW), f32),
                        pltpu.VMEM((L, GW), f32),
                        pltpu.VMEM((2, SSM_GROUPS, H // SSM_GROUPS * SSM_P, SSM_N), f32),
                        pltpu.VMEM((2, L, GW), f32),
                        pltpu.VMEM((2, L // CHUNK, SSM_GROUPS, H // SSM_GROUPS * SSM_P, SSM_N), f32),
                        pltpu.VMEM((L // CHUNK, 8, 128), f32)],
        compiler_params=_cparams(("arbitrary",)),
        name="ssd_mixer",
    )(*args)
    return res if want_state else (res[0], None)


def _ctx_att_kernel(sink_ref, q_ref, k_ref, v_ref, o_ref, *, L):
    G = ATT_HEADS // ATT_KV
    scale = ATT_D ** -0.5
    kb = [k_ref[:, j * ATT_D:(j + 1) * ATT_D].astype(bf16) for j in range(ATT_KV)]
    vb = [v_ref[:, j * ATT_D:(j + 1) * ATT_D].astype(bf16) for j in range(ATT_KV)]
    s = [_nt((q_ref[:, h * ATT_D:(h + 1) * ATT_D] * scale).astype(bf16), kb[h // G])
         for h in range(ATT_HEADS)]
    p, den = [], []
    for h in range(ATT_HEADS):
        sink = sink_ref[h]
        m = jnp.maximum(jnp.max(s[h], axis=-1, keepdims=True), sink)
        e = jnp.exp(s[h] - m)
        den.append(jnp.sum(e, axis=-1, keepdims=True) + jnp.exp(sink - m))
        p.append(e.astype(bf16))
    o = [_nn(p[h], vb[h // G]) for h in range(ATT_HEADS)]
    for h in range(ATT_HEADS):
        o_ref[:, h * ATT_D:(h + 1) * ATT_D] = (o[h] / den[h]).astype(o_ref.dtype)


def _ctx_attention(u, sink, nb, L):
    return pl.pallas_call(
        functools.partial(_ctx_att_kernel, L=L),
        grid=(nb,),
        in_specs=[pl.BlockSpec(memory_space=pltpu.SMEM),
                  pl.BlockSpec((L, GW), lambda b: (b, C_AQ // GW)),
                  pl.BlockSpec((L, 128), lambda b: (b, C_AK // 128)),
                  pl.BlockSpec((L, 128), lambda b: (b, C_AV // 128))],
        out_specs=pl.BlockSpec((L, GW), lambda b: (b, 0)),
        out_shape=jax.ShapeDtypeStruct((nb * L, GW), bf16),
        compiler_params=_cparams(("arbitrary",)),
        name="ctx_attention",
    )(sink, u, u, u)


def _rope(x, cos, sin):
    w = x.shape[1]
    lane = lax.broadcasted_iota(jnp.int32, x.shape, 1)
    first = (lane % 32) < 16
    rot = jnp.where(first, -pltpu.roll(x, w - 16, 1), pltpu.roll(x, 16, 1))
    return x * cos + rot * sin


def _half_variants(x, keep_fill):
    lane = lax.broadcasted_iota(jnp.int32, x.shape, 1)
    lo = lane < ATT_D
    a0 = jnp.where(lo, x, keep_fill)
    b1 = jnp.where(lo, keep_fill, x)
    xr = pltpu.roll(x, ATT_D, 1)
    a1 = jnp.where(lo, xr, keep_fill)
    b0 = jnp.where(lo, keep_fill, xr)
    return ((a0, b0), (a1, b1))


def _half_variants_t(xt):
    sub = lax.broadcasted_iota(jnp.int32, xt.shape, 0)
    lo = sub < ATT_D
    a0 = jnp.where(lo, xt, 0.0)
    b1 = jnp.where(lo, 0.0, xt)
    xr = pltpu.roll(xt, ATT_D, 0)
    a1 = jnp.where(lo, xr, 0.0)
    b0 = jnp.where(lo, 0.0, xr)
    return ((a0, b0), (a1, b1))


def _lat_att_kernel(sink_ref, q_ref, k_ref, v_ref, kc_ref, vc_ref, cos_ref, sin_ref, o_ref,
                    qr_scr, kt_scr, vv_scr, kct_scr, vcv_scr, bias_scr, *, L):
    G = ATT_HEADS // ATT_KV
    B = CHUNK
    nb = L // B
    scale = ATT_D ** -0.5
    zeros = jnp.zeros((B, 128), bf16)
    for j in range(ATT_KV):
        for hf in range(2):
            kt_scr[j, hf, 0] = zeros
            kt_scr[j, hf, nb + 1] = zeros
            vv_scr[j, hf, 0:B, :] = zeros
            vv_scr[j, hf, B + L:2 * B + L, :] = zeros
    kct = _half_variants_t(kc_ref[...].T)
    vcv = _half_variants(vc_ref[...], 1.0)
    for j in range(ATT_KV):
        for hf in range(2):
            kct_scr[j, hf] = kct[j][hf].astype(bf16)
            vcv_scr[j, hf] = vcv[j][hf].astype(bf16)

    rowi = lax.broadcasted_iota(jnp.int32, (2 * B, 3 * B), 0) % B
    coli = lax.broadcasted_iota(jnp.int32, (2 * B, 3 * B), 1)
    inwin = jnp.abs(rowi - (coli - B)) <= ATT_WIN
    ninf = jnp.float32(-jnp.inf)
    bias_scr[0] = jnp.where(inwin & (coli >= B), 0.0, ninf)
    bias_scr[1] = jnp.where(inwin, 0.0, ninf)
    bias_scr[2] = jnp.where(inwin & (coli < 2 * B), 0.0, ninf)

    def prep(n, carry):
        r0 = pl.multiple_of(n * B, B)
        cos = cos_ref[pl.ds(r0, B), :]
        sin = sin_ref[pl.ds(r0, B), :]
        cos4 = jnp.concatenate([cos] * 4, axis=1)
        sin4 = jnp.concatenate([sin] * 4, axis=1)
        qr_scr[pl.ds(r0, B), :] = (_rope(q_ref[pl.ds(r0, B), :], cos4, sin4) * scale).astype(bf16)
        kvar = _half_variants_t(_rope(k_ref[pl.ds(r0, B), :], cos, sin).T)
        vvar = _half_variants(v_ref[pl.ds(r0, B), :], 1.0)
        for j in range(ATT_KV):
            for hf in range(2):
                kt_scr[j, hf, n + 1] = kvar[j][hf].astype(bf16)
                vv_scr[j, hf, pl.ds(B + r0, B), :] = vvar[j][hf].astype(bf16)
        return carry

    lax.fori_loop(0, nb, prep, 0)

    lane_lo = lax.broadcasted_iota(jnp.int32, (2 * B, 128), 1) < ATT_D

    combos = [(j, hf) for j in range(ATT_KV) for hf in range(2)]

    def scores(n):
        r0 = pl.multiple_of(n * B, B)
        qs = [jnp.concatenate([qr_scr[pl.ds(r0, B), (2 * j) * 128:(2 * j + 1) * 128],
                               qr_scr[pl.ds(r0, B), (2 * j + 1) * 128:(2 * j + 2) * 128]], axis=0)
              for j in range(ATT_KV)]
        s_c = [_nn(qs[j], kct_scr[j, hf]) for j, hf in combos]
        s_b = [jnp.concatenate([_nn(qs[j], kt_scr[j, hf, n + t]) for t in range(3)], axis=1)
               for j, hf in combos]
        return s_c, s_b

    def finish(n, s_c, s_b):
        r0 = pl.multiple_of(n * B, B)
        bias = bias_scr[jnp.where(n == 0, 0, jnp.where(n == nb - 1, 2, 1))]
        p_c, p_b, esink = [], [], []
        for i, (j, hf) in enumerate(combos):
            sb = s_b[i] + bias
            sink = jnp.concatenate([jnp.full((B, 1), sink_ref[G * j + hf], f32),
                                    jnp.full((B, 1), sink_ref[G * j + 2 + hf], f32)], axis=0)
            m = jnp.maximum(jnp.maximum(jnp.max(s_c[i], axis=-1, keepdims=True),
                                        jnp.max(sb, axis=-1, keepdims=True)), sink)
            p_c.append(jnp.exp(s_c[i] - m).astype(bf16))
            p_b.append(jnp.exp(sb - m).astype(bf16))
            esink.append(jnp.exp(sink - m))
        oe = [_nn(p_c[i], vcv_scr[j, hf]) + _nn(p_b[i], vv_scr[j, hf, pl.ds(r0, 3 * B), :])
              for i, (j, hf) in enumerate(combos)]
        outs = []
        for i, (j, hf) in enumerate(combos):
            den = oe[i][:, (1 - hf) * ATT_D:(1 - hf) * ATT_D + 1] + esink[i]
            outs.append(oe[i] / den)
        for j in range(ATT_KV):
            o = jnp.where(lane_lo, outs[2 * j], outs[2 * j + 1])
            o_ref[pl.ds(r0, B), (2 * j) * 128:(2 * j + 1) * 128] = o[0:B].astype(o_ref.dtype)
            o_ref[pl.ds(r0, B), (2 * j + 1) * 128:(2 * j + 2) * 128] = o[B:2 * B].astype(o_ref.dtype)

    def blk_pair(i, carry):
        sa = scores(2 * i)
        sb = scores(2 * i + 1)
        finish(2 * i, *sa)
        finish(2 * i + 1, *sb)
        return carry

    lax.fori_loop(0, nb // 2, blk_pair, 0)


def _rope_tables(L):
    pos = jnp.arange(L)
    rows = (pos // GRID_W).astype(f32)
    cols = (pos % GRID_W).astype(f32)
    half = ATT_D // 4
    freqs = ROPE_THETA ** (-jnp.arange(half, dtype=f32) / half)
    ang_r = rows[:, None] * freqs
    ang_c = cols[:, None] * freqs
    ang = jnp.concatenate([ang_r, ang_r, ang_c, ang_c], axis=1)
    ang = jnp.concatenate([ang, ang], axis=1)
    return jnp.cos(ang), jnp.sin(ang)


def _lat_attention(u, k_cache, v_cache, sink, layer, nb, L):
    Lc = k_cache.shape[2]
    cos, sin = _rope_tables(L)
    kc = k_cache.reshape(nb, DEPTH, Lc, ATT_KV * ATT_D)
    vc = v_cache.reshape(nb, DEPTH, Lc, ATT_KV * ATT_D)
    cspec = pl.BlockSpec((None, None, Lc, 128), lambda b: (b, layer, 0, 0))
    return pl.pallas_call(
        functools.partial(_lat_att_kernel, L=L),
        grid=(nb,),
        in_specs=[pl.BlockSpec(memory_space=pltpu.SMEM),
                  pl.BlockSpec((L, GW), lambda b: (b, C_AQ // GW)),
                  pl.BlockSpec((L, 128), lambda b: (b, C_AK // 128)),
                  pl.BlockSpec((L, 128), lambda b: (b, C_AV // 128)),
                  cspec, cspec,
                  pl.BlockSpec((L, 128), lambda b: (0, 0)),
                  pl.BlockSpec((L, 128), lambda b: (0, 0))],
        out_specs=pl.BlockSpec((L, GW), lambda b: (b, 0)),
        out_shape=jax.ShapeDtypeStruct((nb * L, GW), bf16),
        scratch_shapes=[pltpu.VMEM((L, GW), bf16),
                        pltpu.VMEM((ATT_KV, 2, L // CHUNK + 2, 128, CHUNK), bf16),
                        pltpu.VMEM((ATT_KV, 2, L + 2 * CHUNK, 128), bf16),
                        pltpu.VMEM((ATT_KV, 2, 128, Lc), bf16),
                        pltpu.VMEM((ATT_KV, 2, Lc, 128), bf16),
                        pltpu.VMEM((3, 2 * CHUNK, 3 * CHUNK), f32)],
        compiler_params=_cparams(("arbitrary",)),
        name="lat_attention",
    )(sink, u, u, u, kc, vc, cos, sin)


def _permute_w_in_kernel(wt_ref, o_ref):
    ndt = 2 * SSM_HEADS
    moves = [(0, C_CONV, 1024), (1536, C_XBC, 1024), (1024, C_Z, GW), (2560 + ndt, C_RQ, C_DT - C_RQ)]
    for src, dst, n in moves:
        for r in range(0, n, 128):
            o_ref[:, dst + r:dst + r + 128] = wt_ref[src + r:src + r + 128, :].T.astype(bf16)
    tail = wt_ref[2560:2560 + 128, :].T
    lane = lax.broadcasted_iota(jnp.int32, tail.shape, 1)
    o_ref[:, C_DT:UW] = jnp.where(lane < ndt, tail, 0.0).astype(bf16)


def _permute_w_in(w):
    tr = 256
    return pl.pallas_call(
        _permute_w_in_kernel,
        grid=(DEPTH, D // tr),
        in_specs=[pl.BlockSpec((None, N_IN, tr), lambda l, i: (l, 0, i))],
        out_specs=pl.BlockSpec((None, tr, UW), lambda l, i: (l, i, 0)),
        out_shape=jax.ShapeDtypeStruct((DEPTH, D, UW), bf16),
        compiler_params=_cparams(("arbitrary", "arbitrary")),
        name="permute_w_in",
    )(jnp.swapaxes(w, 1, 2))


def _mix_and_mlp(x, u, mod, p, ctx, layer, nb, L, final):
    per_seq = ctx is not None
    dt_t = u[:, C_DT:C_DT + 2 * SSM_HEADS].T
    want_state = ctx is None
    o_conv = _conv_module(u, p['conv_w'], p['conv_b'], p['conv_ln_g'], p['conv_ln_b'], nb, L)
    if ctx is None:
        o_ssm, h_ssm = _ssm(u, dt_t, p, None, layer, nb, L, True)
        o_ret, h_ret = _retention(u, p['ret_log_decay'], p['ret_gn_g'], None, layer, nb, L, True)
        o_att = _ctx_attention(u, p['att_sink'], nb, L)
    else:
        k_c, v_c, s_ssm, s_ret = ctx
        o_ssm, h_ssm = _ssm(u, dt_t, p, s_ssm, layer, nb, L, False)
        o_ret, h_ret = _retention(u, p['ret_log_decay'], p['ret_gn_g'], s_ret, layer, nb, L, False)
        o_att = _lat_attention(u, k_c, v_c, p['att_sink'], layer, nb, L)
    x1 = _outproj(o_conv, o_ssm, o_ret, o_att, p['w_out_b'], x, mod, L, per_seq)
    x2 = _mlp(x1, p['norm_mlp'], mod, p['final_norm'], p['w1_b'], p['w2_b'], L, per_seq, final)
    states = None
    if want_state:
        ak = u[:, C_AK:C_AK + 128].reshape(nb, L, ATT_KV, ATT_D)
        av = u[:, C_AV:C_AV + 128].reshape(nb, L, ATT_KV, ATT_D)
        states = (ak, av, h_ssm, h_ret)
    return x2, states


def kernel(x_prompt, x_sample, cache_attn_k, cache_attn_v, state_ssm, state_ret, c, c_ctx, ada_w, ada_b, norm_mix, norm_mlp, w_in, conv_w, conv_b, conv_ln_g, conv_ln_b, ssm_conv_w, ssm_conv_b, ssm_a_log, ssm_dt_bias, ssm_d, ssm_norm, ret_log_decay, ret_gn_g, att_sink, w_out, w1, w2, final_norm):
    nbp, Lp, _ = x_prompt.shape
    nbs, Ls, _ = x_sample.shape
    cvec = jnp.concatenate([c_ctx[None, :], c, jnp.zeros((16 - 1 - nbs, D), f32)], axis=0)
    mod = _ada_mod(cvec, ada_w, ada_b)
    y_p = x_prompt.reshape(nbp * Lp, D)
    y_s = x_sample.reshape(nbs * Ls, D)
    new_k, new_v, new_ssm, new_ret = [], [], [], []
    w_in_p = _permute_w_in(w_in)
    ctx = (cache_attn_k, cache_attn_v, state_ssm, state_ret)
    for l in range(DEPTH):
        mod_l = mod[l].reshape(16, 6, 1, D)
        mod_p, mod_s = mod_l[0:1], mod_l[1:1 + nbs]
        u_p, (w_out_b,) = _inproj(y_p, norm_mix[l], mod_p, w_in_p, l, Lp, False, [w_out])
        u_s, (w1_b, w2_b) = _inproj(y_s, norm_mix[l], mod_s, w_in_p, l, Ls, True, [w1, w2])
        p = dict(norm_mlp=norm_mlp[l],
                 conv_w=conv_w[l], conv_b=conv_b[l], conv_ln_g=conv_ln_g[l], conv_ln_b=conv_ln_b[l],
                 ssm_conv_w=ssm_conv_w[l], ssm_conv_b=ssm_conv_b[l], ssm_a_log=ssm_a_log[l],
                 ssm_dt_bias=ssm_dt_bias[l], ssm_d=ssm_d[l], ssm_norm=ssm_norm[l],
                 ret_log_decay=ret_log_decay[l], ret_gn_g=ret_gn_g[l], att_sink=att_sink[l],
                 w_out_b=w_out_b, w1_b=w1_b, w2_b=w2_b, final_norm=final_norm)
        final = l == DEPTH - 1
        y_p, (k_l, v_l, hs_l, hr_l) = _mix_and_mlp(y_p, u_p, mod_p, p, None, l, nbp, Lp, final)
        new_k.append(k_l)
        new_v.append(v_l)
        new_ssm.append(hs_l)
        new_ret.append(hr_l)
        y_s, _ = _mix_and_mlp(y_s, u_s, mod_s, p, ctx, l, nbs, Ls, final)
    return (y_p.reshape(nbp, Lp, D), y_s.reshape(nbs, Ls, D),
            jnp.stack(new_k, axis=1), jnp.stack(new_v, axis=1),
            jnp.stack(new_ssm, axis=1), jnp.stack(new_ret, axis=1))
```

```python
import functools

import jax
import jax.numpy as jnp
from jax import lax
from jax.experimental import pallas as pl
from jax.experimental.pallas import tpu as pltpu

f32 = jnp.float32
bf16 = jnp.bfloat16

D = 2048
DEPTH = 2
GW = 512
CONV_K = 31
SSM_HEADS, SSM_P, SSM_N, SSM_GROUPS, SSM_K = 8, 64, 128, 2, 5
RET_HEADS, RET_D = 4, 128
ATT_HEADS, ATT_KV, ATT_D, ATT_WIN = 8, 2, 64, 128
GRID_W = 64
ROPE_THETA = 10000.0
CHUNK = 128
D_FF = 4 * D
EPS = 1e-6

C_CONV, C_XBC, C_Z, C_RQ, C_RK, C_RV, C_RG, C_AQ, C_AK, C_AV, C_DT = (
    0, 1024, 2048, 2560, 3072, 3584, 4096, 4608, 5120, 5248, 5376)
UW = 5504
N_IN = 5392
N_CHUNK = 512

VMEM_LIMIT = 56 * 1024 * 1024


def _cparams(sem):
    return pltpu.CompilerParams(dimension_semantics=sem, vmem_limit_bytes=VMEM_LIMIT)


def _sigmoid(x):
    return 1.0 / (1.0 + jnp.exp(-x))


def _silu(x):
    return x * _sigmoid(x)


def _nt(a, b):
    return lax.dot_general(a, b, (((1,), (1,)), ((), ())), preferred_element_type=f32)


def _tn(a, b):
    return lax.dot_general(a, b, (((0,), (0,)), ((), ())), preferred_element_type=f32)


def _nn(a, b):
    return jnp.dot(a, b, preferred_element_type=f32)


def _split3(x):
    hi = x.astype(bf16)
    r1 = x - hi.astype(f32)
    mid = r1.astype(bf16)
    lo = (r1 - mid.astype(f32)).astype(bf16)
    return hi, mid, lo


def _ada_kernel(c_ref, w_ref, b_ref, o_ref):
    c = c_ref[...]
    s = _silu(c).astype(bf16)
    o_ref[...] = _nn(s, w_ref[...].astype(bf16)) + b_ref[...]


def _ada_mod(cvec, ada_w, ada_b):
    tn = 1024
    return pl.pallas_call(
        _ada_kernel,
        grid=(DEPTH, 6 * D // tn),
        in_specs=[pl.BlockSpec((16, D), lambda l, j: (0, 0)),
                  pl.BlockSpec((None, D, tn), lambda l, j: (l, 0, j)),
                  pl.BlockSpec((None, 1, tn), lambda l, j: (l, 0, j))],
        out_specs=pl.BlockSpec((None, 16, tn), lambda l, j: (l, 0, j)),
        out_shape=jax.ShapeDtypeStruct((DEPTH, 16, 6 * D), f32),
        compiler_params=_cparams(("arbitrary", "arbitrary")),
        name="ada_mod",
    )(cvec, ada_w, ada_b.reshape(DEPTH, 1, 6 * D))


def _mod_spec(idx, tm, L, per_seq):
    if per_seq:
        return pl.BlockSpec((None, None, 1, D), lambda i, *_: ((i * tm) // L, idx, 0, 0))
    return pl.BlockSpec((None, None, 1, D), lambda i, *_: (0, idx, 0, 0))


def _inproj_kernel(x_ref, g_ref, sc_ref, sh_ref, w_ref, *refs, n_cast):
    cast_in, u_ref, cast_out = refs[:n_cast], refs[n_cast], refs[n_cast + 1:]
    x = x_ref[...]
    ms = jnp.mean(x * x, axis=-1, keepdims=True)
    h = x * lax.rsqrt(ms + EPS) * (g_ref[...] * (1.0 + sc_ref[...])) + sh_ref[...]
    hb = h.astype(bf16)
    for n0 in range(0, UW, N_CHUNK):
        n1 = min(n0 + N_CHUNK, UW)
        u_ref[:, n0:n1] = _nn(hb, w_ref[:, n0:n1])
    for src, dst in zip(cast_in, cast_out):
        dst[...] = src[...].astype(bf16)


def _inproj(x, norm_g, mod, w_in_p, layer, L, per_seq, casts):
    T = x.shape[0]
    tm = 256
    n = T // tm
    cast_in_specs, cast_out_specs, cast_shapes = [], [], []
    for w in casts:
        _, R, C = w.shape
        cast_in_specs.append(pl.BlockSpec((None, R // n, C), lambda i: (layer, i, 0)))
        cast_out_specs.append(pl.BlockSpec((R // n, C), lambda i: (i, 0)))
        cast_shapes.append(jax.ShapeDtypeStruct((R, C), bf16))
    res = pl.pallas_call(
        functools.partial(_inproj_kernel, n_cast=len(casts)),
        grid=(n,),
        in_specs=[pl.BlockSpec((tm, D), lambda i: (i, 0)),
                  pl.BlockSpec((1, D), lambda i: (0, 0)),
                  _mod_spec(1, tm, L, per_seq),
                  _mod_spec(0, tm, L, per_seq),
                  pl.BlockSpec((None, D, UW), lambda i: (layer, 0, 0), pipeline_mode=pl.Buffered(1))]
        + cast_in_specs,
        out_specs=[pl.BlockSpec((tm, UW), lambda i: (i, 0))] + cast_out_specs,
        out_shape=[jax.ShapeDtypeStruct((T, UW), f32)] + cast_shapes,
        compiler_params=_cparams(("arbitrary",)),
        name="in_proj",
    )(x, norm_g.reshape(1, D), mod, mod, w_in_p, *casts)
    return res[0], res[1:]


def _outproj_kernel(oc_ref, os_ref, or_ref, oa_ref, w_ref, x_ref, g_ref, o_ref):
    for n0 in range(0, D, N_CHUNK):
        n1 = n0 + N_CHUNK
        acc = _nn(oc_ref[...], w_ref[0:GW, n0:n1])
        acc = acc + _nn(os_ref[...], w_ref[GW:2 * GW, n0:n1])
        acc = acc + _nn(or_ref[...], w_ref[2 * GW:3 * GW, n0:n1])
        acc = acc + _nn(oa_ref[...], w_ref[3 * GW:4 * GW, n0:n1])
        o_ref[:, n0:n1] = x_ref[:, n0:n1] + g_ref[:, n0:n1] * acc


def _outproj(o_conv, o_ssm, o_ret, o_att, w_out_b, x, mod, L, per_seq):
    T = x.shape[0]
    tm = 512
    ospec = pl.BlockSpec((tm, GW), lambda i: (i, 0))
    return pl.pallas_call(
        _outproj_kernel,
        grid=(T // tm,),
        in_specs=[ospec, ospec, ospec, ospec,
                  pl.BlockSpec((D, D), lambda i: (0, 0), pipeline_mode=pl.Buffered(1)),
                  pl.BlockSpec((tm, D), lambda i: (i, 0)),
                  _mod_spec(2, tm, L, per_seq)],
        out_specs=pl.BlockSpec((tm, D), lambda i: (i, 0)),
        out_shape=jax.ShapeDtypeStruct((T, D), f32),
        compiler_params=_cparams(("arbitrary",)),
        name="out_proj",
    )(o_conv, o_ssm, o_ret, o_att, w_out_b, x, mod)


def _mlp_kernel(x_ref, g_ref, sc_ref, sh_ref, g2_ref, fn_ref, w1_ref, w2_ref, o_ref, h_scr, *, final):
    f = pl.program_id(1)
    tm = x_ref.shape[0]
    RB = 256

    def ffn_tile(r):
        a = _nn(h_scr[r:r + 2 * RB, :], w1_ref[...])
        a = jnp.maximum(a, 0.0)
        a = (a * a).astype(bf16)
        return _nn(a, w2_ref[...])

    @pl.when(f == 0)
    def _():
        gs = g_ref[...] * (1.0 + sc_ref[...])
        for r in range(0, tm, RB):
            x = x_ref[r:r + RB, :]
            ms = jnp.mean(x * x, axis=-1, keepdims=True)
            h_scr[r:r + RB, :] = (x * lax.rsqrt(ms + EPS) * gs + sh_ref[...]).astype(bf16)
        for r in range(0, tm, 2 * RB):
            o_ref[r:r + 2 * RB, :] = ffn_tile(r)

    last = pl.num_programs(1) - 1

    @pl.when(jnp.logical_and(f > 0, f < last))
    def _():
        for r in range(0, tm, 2 * RB):
            o_ref[r:r + 2 * RB, :] += ffn_tile(r)

    @pl.when(f == last)
    def _():
        for r in range(0, tm, 2 * RB):
            acc = o_ref[r:r + 2 * RB, :] + ffn_tile(r)
            y = x_ref[r:r + 2 * RB, :] + g2_ref[...] * acc
            if final:
                ms = jnp.mean(y * y, axis=-1, keepdims=True)
                y = y * lax.rsqrt(ms + EPS) * fn_ref[...]
            o_ref[r:r + 2 * RB, :] = y


def _mlp(x, norm_g, mod, final_norm, w1_b, w2_b, L, per_seq, final):
    T = x.shape[0]
    tm, tf = 1024, 512
    return pl.pallas_call(
        functools.partial(_mlp_kernel, final=final),
        grid=(T // tm, D_FF // tf),
        in_specs=[pl.BlockSpec((tm, D), lambda i, f: (i, 0)),
                  pl.BlockSpec((1, D), lambda i, f: (0, 0)),
                  _mod_spec(4, tm, L, per_seq),
                  _mod_spec(3, tm, L, per_seq),
                  _mod_spec(5, tm, L, per_seq),
                  pl.BlockSpec((1, D), lambda i, f: (0, 0)),
                  pl.BlockSpec((D, tf), lambda i, f: (0, f)),
                  pl.BlockSpec((tf, D), lambda i, f: (f, 0))],
        out_specs=pl.BlockSpec((tm, D), lambda i, f: (i, 0)),
        out_shape=jax.ShapeDtypeStruct((T, D), f32),
        scratch_shapes=[pltpu.VMEM((tm, D), bf16)],
        compiler_params=_cparams(("arbitrary", "arbitrary")),
        name="mlp",
    )(x, norm_g.reshape(1, D), mod, mod, mod, final_norm.reshape(1, D), w1_b, w2_b)


def _conv_kernel(u_ref, w_ref, b_ref, lg_ref, lb_ref, o_ref, vpad_ref, ph_ref, *, L):
    RC = 64
    pad = 16
    SUB = 8
    off0 = pad - CONV_K // 2
    nph = RC + SUB * ((off0 + CONV_K - 1) // SUB)
    vpad_ref[0:pad, :] = jnp.zeros((pad, GW), f32)
    vpad_ref[pad + L:2 * pad + L, :] = jnp.zeros((pad, GW), f32)

    def fill(i, carry):
        r0 = pl.multiple_of(i * RC, RC)
        a = u_ref[pl.ds(r0, RC), 0:GW]
        g = u_ref[pl.ds(r0, RC), GW:2 * GW]
        vpad_ref[pl.ds(pad + r0, RC), :] = a * _sigmoid(g)
        return carry

    lax.fori_loop(0, L // RC, fill, 0)

    def body(i, carry):
        r0 = pl.multiple_of(i * RC, RC)
        win = vpad_ref.at[pl.ds(r0, RC + 2 * pad), :]
        accs = []
        for c0 in range(0, GW, 128):
            ls = slice(c0, c0 + 128)
            for ph in range(SUB):
                ph_ref[ph, :, ls] = win[ph:ph + nph, ls]
            acc = jnp.broadcast_to(b_ref[:, ls], (RC, 128))
            for k in range(CONV_K):
                a, ph = divmod(off0 + k, SUB)
                acc = acc + ph_ref[ph, SUB * a:SUB * a + RC, ls] * w_ref[k:k + 1, ls]
            accs.append(acc)
        acc = jnp.concatenate(accs, axis=1)
        mu = jnp.mean(acc, axis=-1, keepdims=True)
        xc = acc - mu
        var = jnp.mean(xc * xc, axis=-1, keepdims=True)
        y = xc * lax.rsqrt(var + EPS) * lg_ref[...] + lb_ref[...]
        o_ref[pl.ds(r0, RC), :] = _silu(y).astype(o_ref.dtype)
        return carry

    lax.fori_loop(0, L // RC, body, 0)


def _conv_module(u, conv_w, conv_b, ln_g, ln_b, nb, L):
    return pl.pallas_call(
        functools.partial(_conv_kernel, L=L),
        grid=(nb,),
        in_specs=[pl.BlockSpec((L, 2 * GW), lambda b: (b, C_CONV // (2 * GW))),
                  pl.BlockSpec((CONV_K, GW), lambda b: (0, 0)),
                  pl.BlockSpec((1, GW), lambda b: (0, 0)),
                  pl.BlockSpec((1, GW), lambda b: (0, 0)),
                  pl.BlockSpec((1, GW), lambda b: (0, 0))],
        out_specs=pl.BlockSpec((L, GW), lambda b: (b, 0)),
        out_shape=jax.ShapeDtypeStruct((nb * L, GW), bf16),
        scratch_shapes=[pltpu.VMEM((L + 32, GW), f32), pltpu.VMEM((8, 64 + 24, GW), f32)],
        compiler_params=_cparams(("arbitrary",)),
        name="conv_module",
    )(u, conv_w, conv_b.reshape(1, GW), ln_g.reshape(1, GW), ln_b.reshape(1, GW))


def _ret_kernel(*refs, L, has_h0, want_state):
    ld_ref, q_ref, k_ref, v_ref, g_ref, gn_ref = refs[:6]
    pos = 6
    h0_ref = None
    if has_h0:
        h0_ref = refs[pos]
        pos += 1
    o_ref = refs[pos]
    pos += 1
    hout_ref = None
    if want_state:
        hout_ref = refs[pos]
        pos += 1
    y_scr, st_scr, dec_scr, cs_scr = refs[pos:pos + 4]

    nc = L // CHUNK
    H = RET_HEADS
    row = lax.broadcasted_iota(jnp.int32, (CHUNK, CHUNK), 0)
    col = lax.broadcasted_iota(jnp.int32, (CHUNK, CHUNK), 1)
    rowf = row.astype(f32)
    diff = (row - col).astype(f32)
    kscale = RET_D ** -0.5
    sls = [slice(h * RET_D, (h + 1) * RET_D) for h in range(H)]

    for d in range(2):
        for h in range(H):
            if has_h0:
                st_scr[d, h] = h0_ref[d, h]
            else:
                st_scr[d, h] = jnp.zeros((RET_D, RET_D), f32)
    for h in range(H):
        laf = ld_ref[h]
        lab = ld_ref[H + h]
        dec_scr[h] = jnp.where(col < row, jnp.exp(diff * laf),
                               jnp.where(col > row, jnp.exp(-diff * lab), 2.0))


    def intra(c, carry):
        r0 = pl.multiple_of(c * CHUNK, CHUNK)
        qb = [q_ref[pl.ds(r0, CHUNK), sls[h]].astype(bf16) for h in range(H)]
        kb = [(k_ref[pl.ds(r0, CHUNK), sls[h]] * kscale).astype(bf16) for h in range(H)]
        vh = [v_ref[pl.ds(r0, CHUNK), sls[h]] for h in range(H)]
        s = [_nt(qb[h], kb[h]) for h in range(H)]
        cf = [_tn((vh[h] * jnp.exp((CHUNK - 1.0 - rowf) * ld_ref[h])).astype(bf16), kb[h])
              for h in range(H)]
        cb = [_tn((vh[h] * jnp.exp(rowf * ld_ref[H + h])).astype(bf16), kb[h]) for h in range(H)]
        m = [(s[h] * dec_scr[h]).astype(bf16) for h in range(H)]
        y = [_nn(m[h], vh[h].astype(bf16)) for h in range(H)]
        for h in range(H):
            y_scr[pl.ds(r0, CHUNK), sls[h]] = y[h]
            cs_scr[0, c, h] = cf[h]
            cs_scr[1, c, h] = cb[h]
        return carry

    lax.fori_loop(0, nc, intra, 0, unroll=2)

    def inter(d, c):
        r0 = pl.multiple_of(c * CHUNK, CHUNK)
        st = [st_scr[d, h] for h in range(H)]
        yi = [_nt(q_ref[pl.ds(r0, CHUNK), sls[h]].astype(bf16), st[h].astype(bf16)) for h in range(H)]
        for h in range(H):
            la = ld_ref[d * H + h]
            st_scr[d, h] = st[h] * jnp.exp(jnp.full((RET_D, RET_D), CHUNK * la, f32)) + cs_scr[d, c, h]
        return r0, yi

    def fwd(c, carry):
        r0, yi = inter(0, c)
        for h in range(H):
            y_scr[pl.ds(r0, CHUNK), sls[h]] += jnp.exp((rowf + 1.0) * ld_ref[h]) * yi[h]
        return carry

    lax.fori_loop(0, nc, fwd, 0, unroll=2)

    def bwd(i, carry):
        r0, yi = inter(1, nc - 1 - i)
        for h in range(H):
            sl = sls[h]
            y = y_scr[pl.ds(r0, CHUNK), sl] + jnp.exp((CHUNK - rowf) * ld_ref[H + h]) * yi[h]
            mu = jnp.mean(y, axis=-1, keepdims=True)
            yc = y - mu
            var = jnp.mean(yc * yc, axis=-1, keepdims=True)
            yn = yc * lax.rsqrt(var + EPS) * gn_ref[:, sl]
            g = g_ref[pl.ds(r0, CHUNK), sl]
            o_ref[pl.ds(r0, CHUNK), sl] = (_silu(g) * yn).astype(o_ref.dtype)
        return carry

    lax.fori_loop(0, nc, bwd, 0, unroll=2)

    if want_state:
        for d in range(2):
            for h in range(RET_HEADS):
                hout_ref[d, h] = st_scr[d, h]


def _retention(u, log_decay, gn_g, h0, layer, nb, L, want_state):
    has_h0 = h0 is not None
    cblk = lambda c: pl.BlockSpec((L, GW), lambda b, c=c: (b, c // GW))
    in_specs = [pl.BlockSpec(memory_space=pltpu.SMEM),
                cblk(C_RQ), cblk(C_RK), cblk(C_RV), cblk(C_RG),
                pl.BlockSpec((1, GW), lambda b: (0, 0))]
    args = [log_decay.reshape(2 * RET_HEADS), u, u, u, u, gn_g.reshape(1, GW)]
    if has_h0:
        in_specs.append(pl.BlockSpec((None, None, 2, RET_HEADS, RET_D, RET_D),
                                     lambda b: (b, layer, 0, 0, 0, 0)))
        args.append(h0)
    out_specs = [pl.BlockSpec((L, GW), lambda b: (b, 0))]
    out_shape = [jax.ShapeDtypeStruct((nb * L, GW), bf16)]
    if want_state:
        out_specs.append(pl.BlockSpec((None, 2, RET_HEADS, RET_D, RET_D), lambda b: (b, 0, 0, 0, 0)))
        out_shape.append(jax.ShapeDtypeStruct((nb, 2, RET_HEADS, RET_D, RET_D), f32))
    res = pl.pallas_call(
        functools.partial(_ret_kernel, L=L, has_h0=has_h0, want_state=want_state),
        grid=(nb,),
        in_specs=in_specs,
        out_specs=out_specs,
        out_shape=out_shape,
        scratch_shapes=[pltpu.VMEM((L, GW), f32),
                        pltpu.VMEM((2, RET_HEADS, RET_D, RET_D), f32),
                        pltpu.VMEM((RET_HEADS, CHUNK, CHUNK), f32),
                        pltpu.VMEM((2, L // CHUNK, RET_HEADS, RET_D, RET_D), f32)],
        compiler_params=_cparams(("arbitrary",)),
        name="retention",
    )(*args)
    return res if want_state else (res[0], None)


def _softplus(x):
    return jnp.maximum(x, 0.0) + jnp.log(1.0 + jnp.exp(-jnp.abs(x)))


def _ssm_kernel(*refs, L, has_h0, want_state):
    (xbc_ref, z_ref, dtc_ref, dtr_ref, cw_ref, cb_ref, alr_ref, alc_ref, dbr_ref, dbc_ref,
     dsk_ref, ng_ref) = refs[:12]
    pos = 12
    h0_ref = None
    if has_h0:
        h0_ref = refs[pos]
        pos += 1
    o_ref = refs[pos]
    pos += 1
    hout_ref = None
    if want_state:
        hout_ref = refs[pos]
        pos += 1
    xpad_ref, xc_scr, y_scr, st_scr, es_scr, cs_scr, tot_scr = refs[pos:pos + 7]

    nc = L // CHUNK
    H, P, N = SSM_HEADS, SSM_P, SSM_N
    HG = H // SSM_GROUPS
    pad = 8
    row = lax.broadcasted_iota(jnp.int32, (CHUNK, CHUNK), 0)
    col = lax.broadcasted_iota(jnp.int32, (CHUNK, CHUNK), 1)
    lt01 = (row >= col).astype(bf16)
    ut01 = (row <= col).astype(bf16)
    sub16 = lax.broadcasted_iota(jnp.int32, (2 * H, CHUNK), 0)
    lane_lo = col < P

    xpad_ref[0:pad, :] = jnp.zeros((pad, 2 * GW), f32)
    xpad_ref[pad + L:2 * pad + L, :] = jnp.zeros((pad, 2 * GW), f32)

    def fill(i, carry):
        r0 = pl.multiple_of(i * CHUNK, CHUNK)
        xpad_ref[pl.ds(pad + r0, CHUNK), :] = xbc_ref[pl.ds(r0, CHUNK), :]
        return carry

    lax.fori_loop(0, nc, fill, 0)

    for d in range(2):
        for h in range(H):
            g, hh = divmod(h, HG)
            if has_h0:
                st_scr[d, g, hh * P:(hh + 1) * P, :] = h0_ref[d, h]
            else:
                st_scr[d, g, hh * P:(hh + 1) * P, :] = jnp.zeros((P, N), f32)

    a_neg_r = -jnp.exp(alr_ref[...])
    a_neg_c = -jnp.exp(alc_ref[...])

    def per_head_lanes(v, base):
        cols = []
        for c2 in range(H // 2):
            a = jnp.broadcast_to(v[:, base + 2 * c2:base + 2 * c2 + 1], (CHUNK, 2 * P))
            b = jnp.broadcast_to(v[:, base + 2 * c2 + 1:base + 2 * c2 + 2], (CHUNK, 2 * P))
            cols.append(jnp.where(lane_lo, a, b))
        return jnp.concatenate(cols, axis=1)

    def grp(x, g, base):
        return x[:, base + g * N:base + (g + 1) * N]


    def intra(c, carry):
        r0 = pl.multiple_of(c * CHUNK, CHUNK)
        win = xpad_ref.at[pl.ds(r0, CHUNK + 2 * pad), :]
        off = pad - SSM_K // 2
        for c0 in range(0, 2 * GW, 128):
            ls = slice(c0, c0 + 128)
            acc = jnp.broadcast_to(cb_ref[:, ls], (CHUNK, 128))
            for k in range(SSM_K):
                acc = acc + win[off + k:off + k + CHUNK, ls] * cw_ref[k:k + 1, ls]
            xc_scr[pl.ds(r0, CHUNK), ls] = _silu(acc)
        xc = xc_scr[pl.ds(r0, CHUNK), :]
        xs = xc[:, 0:GW]
        bmb = [grp(xc, g, GW).astype(bf16) for g in range(SSM_GROUPS)]
        cmb = [grp(xc, g, GW + SSM_GROUPS * N).astype(bf16) for g in range(SSM_GROUPS)]
        gmat = [_nt(cmb[g], bmb[g]) for g in range(SSM_GROUPS)]

        dt_c = _softplus(dtc_ref[pl.ds(r0, CHUNK), :] + dbr_ref[...])
        dt_r = _softplus(dtr_ref[:, pl.ds(r0, CHUNK)] + dbc_ref[...])
        lc3 = _split3(dt_c * a_neg_r)
        lr3 = _split3(dt_r * a_neg_c)
        pc = [_nn(lt01, t) for t in lc3]
        sc = [_nn(ut01, t) for t in lc3]
        pr = [_nn(t, ut01) for t in lr3]
        sr = [_nn(t, lt01) for t in lr3]
        a_c = jnp.where(col < H, pc[0] + pc[1] + pc[2], sc[0] + sc[1] + sc[2])
        a_r = jnp.where(sub16 < H, pr[0] + pr[1] + pr[2], sr[0] + sr[1] + sr[2])
        tot = jnp.where(col[0:1, :] < H, a_c[CHUNK - 1:CHUNK, :], a_c[0:1, :])
        tot_scr[c] = jnp.broadcast_to(tot, (8, 128))
        ks = dt_c * jnp.exp(tot - a_c)
        es = jnp.exp(a_c)
        es_scr[0, pl.ds(r0, CHUNK), :] = per_head_lanes(es, 0)
        es_scr[1, pl.ds(r0, CHUNK), :] = per_head_lanes(es, H)
        xwf = (xs * per_head_lanes(ks, 0)).astype(bf16)
        xwb = (xs * per_head_lanes(ks, H)).astype(bf16)
        csf = [_tn(xwf[:, g * HG * P:(g + 1) * HG * P], bmb[g]) for g in range(SSM_GROUPS)]
        csb = [_tn(xwb[:, g * HG * P:(g + 1) * HG * P], bmb[g]) for g in range(SSM_GROUPS)]
        for g in range(SSM_GROUPS):
            cs_scr[0, c, g] = csf[g]
            cs_scr[1, c, g] = csb[g]

        ms = []
        for h in range(H):
            df = jnp.exp(jnp.minimum(a_c[:, h:h + 1] - a_r[h:h + 1, :], 0.0)) * dt_r[h:h + 1, :]
            db = jnp.exp(jnp.minimum(a_c[:, H + h:H + h + 1] - a_r[H + h:H + h + 1, :], 0.0)) \
                * dt_r[H + h:H + h + 1, :]
            dm = jnp.where(col <= row, df, 0.0) + jnp.where(col >= row, db, 0.0)
            ms.append((gmat[h // HG] * dm).astype(bf16))
        ys = []
        for c2 in range(H // 2):
            xcol = xs[:, c2 * 2 * P:(c2 + 1) * 2 * P]
            x_lo = jnp.where(lane_lo, xcol, 0.0).astype(bf16)
            x_hi = jnp.where(lane_lo, 0.0, xcol).astype(bf16)
            ys.append(_nn(ms[2 * c2], x_lo) + _nn(ms[2 * c2 + 1], x_hi))
        y = jnp.concatenate(ys, axis=1)
        y_scr[pl.ds(r0, CHUNK), :] = y + (dsk_ref[0:1, :] + dsk_ref[1:2, :]) * xs
        return carry

    lax.fori_loop(0, nc, intra, 0, unroll=2)

    def inter(d, c):
        r0 = pl.multiple_of(c * CHUNK, CHUNK)
        st = [st_scr[d, g] for g in range(SSM_GROUPS)]
        yi = [_nt(grp(xc_scr[pl.ds(r0, CHUNK), :], g, GW + SSM_GROUPS * N).astype(bf16),
                  st[g].astype(bf16)) for g in range(SSM_GROUPS)]
        tot = tot_scr[c]
        for g in range(SSM_GROUPS):
            dec = jnp.concatenate(
                [jnp.broadcast_to(jnp.exp(tot[0:1, d * H + g * HG + hh:d * H + g * HG + hh + 1]), (P, N))
                 for hh in range(HG)], axis=0)
            st_scr[d, g] = st[g] * dec + cs_scr[d, c, g]
        return r0, jnp.concatenate(yi, axis=1) * es_scr[d, pl.ds(r0, CHUNK), :]

    def fwd(c, carry):
        r0, yi = inter(0, c)
        y_scr[pl.ds(r0, CHUNK), :] += yi
        return carry

    lax.fori_loop(0, nc, fwd, 0, unroll=2)

    def bwd(i, carry):
        r0, yi = inter(1, nc - 1 - i)
        yz = (y_scr[pl.ds(r0, CHUNK), :] + yi) * _silu(z_ref[pl.ds(r0, CHUNK), :])
        ms = jnp.mean(yz * yz, axis=-1, keepdims=True)
        o_ref[pl.ds(r0, CHUNK), :] = (yz * lax.rsqrt(ms + EPS) * ng_ref[...]).astype(o_ref.dtype)
        return carry

    lax.fori_loop(0, nc, bwd, 0, unroll=2)

    if want_state:
        for d in range(2):
            for h in range(H):
                g, hh = divmod(h, HG)
                hout_ref[d, h] = st_scr[d, g, hh * P:(hh + 1) * P, :]


def _pad_lanes(v, n=128):
    return jnp.pad(v, ((0, 0), (0, n - v.shape[1])))


def _ssm(u, dt_t, p, h0, layer, nb, L, want_state):
    has_h0 = h0 is not None
    H = SSM_HEADS
    a_log = p['ssm_a_log'].reshape(1, 2 * H)
    dt_bias = p['ssm_dt_bias'].reshape(1, 2 * H)
    small = lambda shape: pl.BlockSpec(shape, lambda b: (0, 0))
    in_specs = [pl.BlockSpec((L, 2 * GW), lambda b: (b, C_XBC // (2 * GW))),
                pl.BlockSpec((L, GW), lambda b: (b, C_Z // GW)),
                pl.BlockSpec((L, 128), lambda b: (b, C_DT // 128)),
                pl.BlockSpec((2 * H, L), lambda b: (0, b)),
                small((SSM_K, 2 * GW)), small((1, 2 * GW)),
                small((1, 128)), small((2 * H, 128)), small((1, 128)), small((2 * H, 128)),
                small((2, GW)), small((1, GW))]
    args = [u, u, u, dt_t, p['ssm_conv_w'], p['ssm_conv_b'].reshape(1, 2 * GW),
            _pad_lanes(a_log), jnp.broadcast_to(a_log.reshape(2 * H, 1), (2 * H, 128)),
            _pad_lanes(dt_bias), jnp.broadcast_to(dt_bias.reshape(2 * H, 1), (2 * H, 128)),
            jnp.repeat(p['ssm_d'], SSM_P, axis=1), p['ssm_norm'].reshape(1, GW)]
    if has_h0:
        in_specs.append(pl.BlockSpec((None, None, 2, H, SSM_P, SSM_N), lambda b: (b, layer, 0, 0, 0, 0)))
        args.append(h0)
    out_specs = [pl.BlockSpec((L, GW), lambda b: (b, 0))]
    out_shape = [jax.ShapeDtypeStruct((nb * L, GW), bf16)]
    if want_state:
        out_specs.append(pl.BlockSpec((None, 2, H, SSM_P, SSM_N), lambda b: (b, 0, 0, 0, 0)))
        out_shape.append(jax.ShapeDtypeStruct((nb, 2, H, SSM_P, SSM_N), f32))
    res = pl.pallas_call(
        functools.partial(_ssm_kernel, L=L, has_h0=has_h0, want_state=want_state),
        grid=(nb,),
        in_specs=in_specs,
        out_specs=out_specs,
        out_shape=out_shape,
        scratch_shapes=[pltpu.VMEM((L + 16, 2 * GW), f32),
                        pltpu.VMEM((L, 2 * GW), f32),
                        pltpu.VMEM((L, GW), f32),
                        pltpu.VMEM((2, SSM_GROUPS, H // SSM_GROUPS * SSM_P, SSM_N), f32),
                        pltpu.VMEM((2, L, GW), f32),
                        pltpu.VMEM((2, L // CHUNK, SSM_GROUPS, H // SSM_GROUPS * SSM_P, SSM_N), f32),
                        pltpu.VMEM((L // CHUNK, 8, 128), f32)],
        compiler_params=_cparams(("arbitrary",)),
        name="ssd_mixer",
    )(*args)
    return res if want_state else (res[0], None)


def _ctx_att_kernel(sink_ref, q_ref, k_ref, v_ref, o_ref, *, L):
    G = ATT_HEADS // ATT_KV
    scale = ATT_D ** -0.5
    kb = [k_ref[:, j * ATT_D:(j + 1) * ATT_D].astype(bf16) for j in range(ATT_KV)]
    vb = [v_ref[:, j * ATT_D:(j + 1) * ATT_D].astype(bf16) for j in range(ATT_KV)]
    s = [_nt((q_ref[:, h * ATT_D:(h + 1) * ATT_D] * scale).astype(bf16), kb[h // G])
         for h in range(ATT_HEADS)]
    p, den = [], []
    for h in range(ATT_HEADS):
        sink = sink_ref[h]
        m = jnp.maximum(jnp.max(s[h], axis=-1, keepdims=True), sink)
        e = jnp.exp(s[h] - m)
        den.append(jnp.sum(e, axis=-1, keepdims=True) + jnp.exp(sink - m))
        p.append(e.astype(bf16))
    o = [_nn(p[h], vb[h // G]) for h in range(ATT_HEADS)]
    for h in range(ATT_HEADS):
        o_ref[:, h * ATT_D:(h + 1) * ATT_D] = (o[h] / den[h]).astype(o_ref.dtype)


def _ctx_attention(u, sink, nb, L):
    return pl.pallas_call(
        functools.partial(_ctx_att_kernel, L=L),
        grid=(nb,),
        in_specs=[pl.BlockSpec(memory_space=pltpu.SMEM),
                  pl.BlockSpec((L, GW), lambda b: (b, C_AQ // GW)),
                  pl.BlockSpec((L, 128), lambda b: (b, C_AK // 128)),
                  pl.BlockSpec((L, 128), lambda b: (b, C_AV // 128))],
        out_specs=pl.BlockSpec((L, GW), lambda b: (b, 0)),
        out_shape=jax.ShapeDtypeStruct((nb * L, GW), bf16),
        compiler_params=_cparams(("arbitrary",)),
        name="ctx_attention",
    )(sink, u, u, u)


def _rope(x, cos, sin):
    w = x.shape[1]
    lane = lax.broadcasted_iota(jnp.int32, x.shape, 1)
    first = (lane % 32) < 16
    rot = jnp.where(first, -pltpu.roll(x, w - 16, 1), pltpu.roll(x, 16, 1))
    return x * cos + rot * sin


def _half_variants(x, keep_fill):
    lane = lax.broadcasted_iota(jnp.int32, x.shape, 1)
    lo = lane < ATT_D
    a0 = jnp.where(lo, x, keep_fill)
    b1 = jnp.where(lo, keep_fill, x)
    xr = pltpu.roll(x, ATT_D, 1)
    a1 = jnp.where(lo, xr, keep_fill)
    b0 = jnp.where(lo, keep_fill, xr)
    return ((a0, b0), (a1, b1))


def _half_variants_t(xt):
    sub = lax.broadcasted_iota(jnp.int32, xt.shape, 0)
    lo = sub < ATT_D
    a0 = jnp.where(lo, xt, 0.0)
    b1 = jnp.where(lo, 0.0, xt)
    xr = pltpu.roll(xt, ATT_D, 0)
    a1 = jnp.where(lo, xr, 0.0)
    b0 = jnp.where(lo, 0.0, xr)
    return ((a0, b0), (a1, b1))


def _lat_att_kernel(sink_ref, q_ref, k_ref, v_ref, kc_ref, vc_ref, cos_ref, sin_ref, o_ref,
                    qr_scr, kt_scr, vv_scr, kct_scr, vcv_scr, bias_scr, *, L):
    G = ATT_HEADS // ATT_KV
    B = CHUNK
    nb = L // B
    scale = ATT_D ** -0.5
    zeros = jnp.zeros((B, 128), bf16)
    for j in range(ATT_KV):
        for hf in range(2):
            kt_scr[j, hf, 0] = zeros
            kt_scr[j, hf, nb + 1] = zeros
            vv_scr[j, hf, 0:B, :] = zeros
            vv_scr[j, hf, B + L:2 * B + L, :] = zeros
    kct = _half_variants_t(kc_ref[...].T)
    vcv = _half_variants(vc_ref[...], 1.0)
    for j in range(ATT_KV):
        for hf in range(2):
            kct_scr[j, hf] = kct[j][hf].astype(bf16)
            vcv_scr[j, hf] = vcv[j][hf].astype(bf16)

    rowi = lax.broadcasted_iota(jnp.int32, (2 * B, 3 * B), 0) % B
    coli = lax.broadcasted_iota(jnp.int32, (2 * B, 3 * B), 1)
    inwin = jnp.abs(rowi - (coli - B)) <= ATT_WIN
    ninf = jnp.float32(-jnp.inf)
    bias_scr[0] = jnp.where(inwin & (coli >= B), 0.0, ninf)
    bias_scr[1] = jnp.where(inwin, 0.0, ninf)
    bias_scr[2] = jnp.where(inwin & (coli < 2 * B), 0.0, ninf)

    def prep(n, carry):
        r0 = pl.multiple_of(n * B, B)
        cos = cos_ref[pl.ds(r0, B), :]
        sin = sin_ref[pl.ds(r0, B), :]
        cos4 = jnp.concatenate([cos] * 4, axis=1)
        sin4 = jnp.concatenate([sin] * 4, axis=1)
        qr_scr[pl.ds(r0, B), :] = (_rope(q_ref[pl.ds(r0, B), :], cos4, sin4) * scale).astype(bf16)
        kvar = _half_variants_t(_rope(k_ref[pl.ds(r0, B), :], cos, sin).T)
        vvar = _half_variants(v_ref[pl.ds(r0, B), :], 1.0)
        for j in range(ATT_KV):
            for hf in range(2):
                kt_scr[j, hf, n + 1] = kvar[j][hf].astype(bf16)
                vv_scr[j, hf, pl.ds(B + r0, B), :] = vvar[j][hf].astype(bf16)
        return carry

    lax.fori_loop(0, nb, prep, 0)

    lane_lo = lax.broadcasted_iota(jnp.int32, (2 * B, 128), 1) < ATT_D

    combos = [(j, hf) for j in range(ATT_KV) for hf in range(2)]

    def scores(n):
        r0 = pl.multiple_of(n * B, B)
        qs = [jnp.concatenate([qr_scr[pl.ds(r0, B), (2 * j) * 128:(2 * j + 1) * 128],
                               qr_scr[pl.ds(r0, B), (2 * j + 1) * 128:(2 * j + 2) * 128]], axis=0)
              for j in range(ATT_KV)]
        s_c = [_nn(qs[j], kct_scr[j, hf]) for j, hf in combos]
        s_b = [jnp.concatenate([_nn(qs[j], kt_scr[j, hf, n + t]) for t in range(3)], axis=1)
               for j, hf in combos]
        return s_c, s_b

    def finish(n, s_c, s_b):
        r0 = pl.multiple_of(n * B, B)
        bias = bias_scr[jnp.where(n == 0, 0, jnp.where(n == nb - 1, 2, 1))]
        p_c, p_b, esink = [], [], []
        for i, (j, hf) in enumerate(combos):
            sb = s_b[i] + bias
            sink = jnp.concatenate([jnp.full((B, 1), sink_ref[G * j + hf], f32),
                                    jnp.full((B, 1), sink_ref[G * j + 2 + hf], f32)], axis=0)
            m = jnp.maximum(jnp.maximum(jnp.max(s_c[i], axis=-1, keepdims=True),
                                        jnp.max(sb, axis=-1, keepdims=True)), sink)
            p_c.append(jnp.exp(s_c[i] - m).astype(bf16))
            p_b.append(jnp.exp(sb - m).astype(bf16))
            esink.append(jnp.exp(sink - m))
        oe = [_nn(p_c[i], vcv_scr[j, hf]) + _nn(p_b[i], vv_scr[j, hf, pl.ds(r0, 3 * B), :])
              for i, (j, hf) in enumerate(combos)]
        outs = []
        for i, (j, hf) in enumerate(combos):
            den = oe[i][:, (1 - hf) * ATT_D:(1 - hf) * ATT_D + 1] + esink[i]
            outs.append(oe[i] / den)
        for j in range(ATT_KV):
            o = jnp.where(lane_lo, outs[2 * j], outs[2 * j + 1])
            o_ref[pl.ds(r0, B), (2 * j) * 128:(2 * j + 1) * 128] = o[0:B].astype(o_ref.dtype)
            o_ref[pl.ds(r0, B), (2 * j + 1) * 128:(2 * j + 2) * 128] = o[B:2 * B].astype(o_ref.dtype)

    def blk_pair(i, carry):
        sa = scores(2 * i)
        sb = scores(2 * i + 1)
        finish(2 * i, *sa)
        finish(2 * i + 1, *sb)
        return carry

    lax.fori_loop(0, nb // 2, blk_pair, 0)


def _rope_tables(L):
    pos = jnp.arange(L)
    rows = (pos // GRID_W).astype(f32)
    cols = (pos % GRID_W).astype(f32)
    half = ATT_D // 4
    freqs = ROPE_THETA ** (-jnp.arange(half, dtype=f32) / half)
    ang_r = rows[:, None] * freqs
    ang_c = cols[:, None] * freqs
    ang = jnp.concatenate([ang_r, ang_r, ang_c, ang_c], axis=1)
    ang = jnp.concatenate([ang, ang], axis=1)
    return jnp.cos(ang), jnp.sin(ang)


def _lat_attention(u, k_cache, v_cache, sink, layer, nb, L):
    Lc = k_cache.shape[2]
    cos, sin = _rope_tables(L)
    kc = k_cache.reshape(nb, DEPTH, Lc, ATT_KV * ATT_D)
    vc = v_cache.reshape(nb, DEPTH, Lc, ATT_KV * ATT_D)
    cspec = pl.BlockSpec((None, None, Lc, 128), lambda b: (b, layer, 0, 0))
    return pl.pallas_call(
        functools.partial(_lat_att_kernel, L=L),
        grid=(nb,),
        in_specs=[pl.BlockSpec(memory_space=pltpu.SMEM),
                  pl.BlockSpec((L, GW), lambda b: (b, C_AQ // GW)),
                  pl.BlockSpec((L, 128), lambda b: (b, C_AK // 128)),
                  pl.BlockSpec((L, 128), lambda b: (b, C_AV // 128)),
                  cspec, cspec,
                  pl.BlockSpec((L, 128), lambda b: (0, 0)),
                  pl.BlockSpec((L, 128), lambda b: (0, 0))],
        out_specs=pl.BlockSpec((L, GW), lambda b: (b, 0)),
        out_shape=jax.ShapeDtypeStruct((nb * L, GW), bf16),
        scratch_shapes=[pltpu.VMEM((L, GW), bf16),
                        pltpu.VMEM((ATT_KV, 2, L // CHUNK + 2, 128, CHUNK), bf16),
                        pltpu.VMEM((ATT_KV, 2, L + 2 * CHUNK, 128), bf16),
                        pltpu.VMEM((ATT_KV, 2, 128, Lc), bf16),
                        pltpu.VMEM((ATT_KV, 2, Lc, 128), bf16),
                        pltpu.VMEM((3, 2 * CHUNK, 3 * CHUNK), f32)],
        compiler_params=_cparams(("arbitrary",)),
        name="lat_attention",
    )(sink, u, u, u, kc, vc, cos, sin)


def _permute_w_in_kernel(wt_ref, o_ref):
    ndt = 2 * SSM_HEADS
    moves = [(0, C_CONV, 1024), (1536, C_XBC, 1024), (1024, C_Z, GW), (2560 + ndt, C_RQ, C_DT - C_RQ)]
    for src, dst, n in moves:
        for r in range(0, n, 128):
            o_ref[:, dst + r:dst + r + 128] = wt_ref[src + r:src + r + 128, :].T.astype(bf16)
    tail = wt_ref[2560:2560 + 128, :].T
    lane = lax.broadcasted_iota(jnp.int32, tail.shape, 1)
    o_ref[:, C_DT:UW] = jnp.where(lane < ndt, tail, 0.0).astype(bf16)


def _permute_w_in(w):
    tr = 256
    return pl.pallas_call(
        _permute_w_in_kernel,
        grid=(DEPTH, D // tr),
        in_specs=[pl.BlockSpec((None, N_IN, tr), lambda l, i: (l, 0, i))],
        out_specs=pl.BlockSpec((None, tr, UW), lambda l, i: (l, i, 0)),
        out_shape=jax.ShapeDtypeStruct((DEPTH, D, UW), bf16),
        compiler_params=_cparams(("arbitrary", "arbitrary")),
        name="permute_w_in",
    )(jnp.swapaxes(w, 1, 2))


def _mix_and_mlp(x, u, mod, p, ctx, layer, nb, L, final):
    per_seq = ctx is not None
    dt_t = u[:, C_DT:C_DT + 2 * SSM_HEADS].T
    want_state = ctx is None
    o_conv = _conv_module(u, p['conv_w'], p['conv_b'], p['conv_ln_g'], p['conv_ln_b'], nb, L)
    if ctx is None:
        o_ssm, h_ssm = _ssm(u, dt_t, p, None, layer, nb, L, True)
        o_ret, h_ret = _retention(u, p['ret_log_decay'], p['ret_gn_g'], None, layer, nb, L, True)
        o_att = _ctx_attention(u, p['att_sink'], nb, L)
    else:
        k_c, v_c, s_ssm, s_ret = ctx
        o_ssm, h_ssm = _ssm(u, dt_t, p, s_ssm, layer, nb, L, False)
        o_ret, h_ret = _retention(u, p['ret_log_decay'], p['ret_gn_g'], s_ret, layer, nb, L, False)
        o_att = _lat_attention(u, k_c, v_c, p['att_sink'], layer, nb, L)
    x1 = _outproj(o_conv, o_ssm, o_ret, o_att, p['w_out_b'], x, mod, L, per_seq)
    x2 = _mlp(x1, p['norm_mlp'], mod, p['final_norm'], p['w1_b'], p['w2_b'], L, per_seq, final)
    states = None
    if want_state:
        ak = u[:, C_AK:C_AK + 128].reshape(nb, L, ATT_KV, ATT_D)
        av = u[:, C_AV:C_AV + 128].reshape(nb, L, ATT_KV, ATT_D)
        states = (ak, av, h_ssm, h_ret)
    return x2, states


def kernel(x_prompt, x_sample, cache_attn_k, cache_attn_v, state_ssm, state_ret, c, c_ctx, ada_w, ada_b, norm_mix, norm_mlp, w_in, conv_w, conv_b, conv_ln_g, conv_ln_b, ssm_conv_w, ssm_conv_b, ssm_a_log, ssm_dt_bias, ssm_d, ssm_norm, ret_log_decay, ret_gn_g, att_sink, w_out, w1, w2, final_norm):
    nbp, Lp, _ = x_prompt.shape
    nbs, Ls, _ = x_sample.shape
    cvec = jnp.concatenate([c_ctx[None, :], c, jnp.zeros((16 - 1 - nbs, D), f32)], axis=0)
    mod = _ada_mod(cvec, ada_w, ada_b)
    y_p = x_prompt.reshape(nbp * Lp, D)
    y_s = x_sample.reshape(nbs * Ls, D)
    new_k, new_v, new_ssm, new_ret = [], [], [], []
    w_in_p = _permute_w_in(w_in)
    ctx = (cache_attn_k, cache_attn_v, state_ssm, state_ret)
    for l in range(DEPTH):
        mod_l = mod[l].reshape(16, 6, 1, D)
        mod_p, mod_s = mod_l[0:1], mod_l[1:1 + nbs]
        u_p, (w_out_b,) = _inproj(y_p, norm_mix[l], mod_p, w_in_p, l, Lp, False, [w_out])
        u_s, (w1_b, w2_b) = _inproj(y_s, norm_mix[l], mod_s, w_in_p, l, Ls, True, [w1, w2])
        p = dict(norm_mlp=norm_mlp[l],
                 conv_w=conv_w[l], conv_b=conv_b[l], conv_ln_g=conv_ln_g[l], conv_ln_b=conv_ln_b[l],
                 ssm_conv_w=ssm_conv_w[l], ssm_conv_b=ssm_conv_b[l], ssm_a_log=ssm_a_log[l],
                 ssm_dt_bias=ssm_dt_bias[l], ssm_d=ssm_d[l], ssm_norm=ssm_norm[l],
                 ret_log_decay=ret_log_decay[l], ret_gn_g=ret_gn_g[l], att_sink=att_sink[l],
                 w_out_b=w_out_b, w1_b=w1_b, w2_b=w2_b, final_norm=final_norm)
        final = l == DEPTH - 1
        y_p, (k_l, v_l, hs_l, hr_l) = _mix_and_mlp(y_p, u_p, mod_p, p, None, l, nbp, Lp, final)
        new_k.append(k_l)
        new_v.append(v_l)
        new_ssm.append(hs_l)
        new_ret.append(hr_l)
        y_s, _ = _mix_and_mlp(y_s, u_s, mod_s, p, ctx, l, nbs, Ls, final)
    return (y_p.reshape(nbp, Lp, D), y_s.reshape(nbs, Ls, D),
            jnp.stack(new_k, axis=1), jnp.stack(new_v, axis=1),
            jnp.stack(new_ssm, axis=1), jnp.stack(new_ret, axis=1))
```

```python
import functools

import jax
import jax.numpy as jnp
from jax import lax
from jax.experimental import pallas as pl
from jax.experimental.pallas import tpu as pltpu

f32 = jnp.float32
bf16 = jnp.bfloat16

D = 2048
DEPTH = 2
GW = 512
CONV_K = 31
SSM_HEADS, SSM_P, SSM_N, SSM_GROUPS, SSM_K = 8, 64, 128, 2, 5
RET_HEADS, RET_D = 4, 128
ATT_HEADS, ATT_KV, ATT_D, ATT_WIN = 8, 2, 64, 128
GRID_W = 64
ROPE_THETA = 10000.0
CHUNK = 128
D_FF = 4 * D
EPS = 1e-6

C_CONV, C_XBC, C_Z, C_RQ, C_RK, C_RV, C_RG, C_AQ, C_AK, C_AV, C_DT = (
    0, 1024, 2048, 2560, 3072, 3584, 4096, 4608, 5120, 5248, 5376)
UW = 5504
N_IN = 5392
N_CHUNK = 512

VMEM_LIMIT = 56 * 1024 * 1024


def _cparams(sem):
    return pltpu.CompilerParams(dimension_semantics=sem, vmem_limit_bytes=VMEM_LIMIT)


def _sigmoid(x):
    return 1.0 / (1.0 + jnp.exp(-x))


def _silu(x):
    return x * _sigmoid(x)


def _nt(a, b):
    return lax.dot_general(a, b, (((1,), (1,)), ((), ())), preferred_element_type=f32)


def _tn(a, b):
    return lax.dot_general(a, b, (((0,), (0,)), ((), ())), preferred_element_type=f32)


def _nn(a, b):
    return jnp.dot(a, b, preferred_element_type=f32)


def _split3(x):
    hi = x.astype(bf16)
    r1 = x - hi.astype(f32)
    mid = r1.astype(bf16)
    lo = (r1 - mid.astype(f32)).astype(bf16)
    return hi, mid, lo


def _ada_kernel(c_ref, w_ref, b_ref, o_ref):
    c = c_ref[...]
    s = _silu(c).astype(bf16)
    o_ref[...] = _nn(s, w_ref[...].astype(bf16)) + b_ref[...]


def _ada_mod(cvec, ada_w, ada_b):
    tn = 1024
    return pl.pallas_call(
        _ada_kernel,
        grid=(DEPTH, 6 * D // tn),
        in_specs=[pl.BlockSpec((16, D), lambda l, j: (0, 0)),
                  pl.BlockSpec((None, D, tn), lambda l, j: (l, 0, j)),
                  pl.BlockSpec((None, 1, tn), lambda l, j: (l, 0, j))],
        out_specs=pl.BlockSpec((None, 16, tn), lambda l, j: (l, 0, j)),
        out_shape=jax.ShapeDtypeStruct((DEPTH, 16, 6 * D), f32),
        compiler_params=_cparams(("arbitrary", "arbitrary")),
        name="ada_mod",
    )(cvec, ada_w, ada_b.reshape(DEPTH, 1, 6 * D))


def _mod_spec(idx, tm, L, per_seq):
    if per_seq:
        return pl.BlockSpec((None, None, 1, D), lambda i, *_: ((i * tm) // L, idx, 0, 0))
    return pl.BlockSpec((None, None, 1, D), lambda i, *_: (0, idx, 0, 0))


def _inproj_kernel(x_ref, g_ref, sc_ref, sh_ref, w_ref, *refs, n_cast):
    cast_in, u_ref, dtt_ref, cast_out = refs[:n_cast], refs[n_cast], refs[n_cast + 1], refs[n_cast + 2:]
    x = x_ref[...]
    ms = jnp.mean(x * x, axis=-1, keepdims=True)
    h = x * lax.rsqrt(ms + EPS) * (g_ref[...] * (1.0 + sc_ref[...])) + sh_ref[...]
    hb = h.astype(bf16)
    for n0 in range(0, UW, N_CHUNK):
        n1 = min(n0 + N_CHUNK, UW)
        u_ref[:, n0:n1] = _nn(hb, w_ref[:, n0:n1])
    dtt_ref[...] = u_ref[:, C_DT:C_DT + 128].T[0:2 * SSM_HEADS, :]
    for src, dst in zip(cast_in, cast_out):
        dst[...] = src[...].astype(bf16)


def _inproj(x, norm_g, mod, w_in_p, layer, L, per_seq, casts):
    T = x.shape[0]
    tm = 256
    n = T // tm
    cast_in_specs, cast_out_specs, cast_shapes = [], [], []
    for w in casts:
        _, R, C = w.shape
        cast_in_specs.append(pl.BlockSpec((None, R // n, C), lambda i: (layer, i, 0)))
        cast_out_specs.append(pl.BlockSpec((R // n, C), lambda i: (i, 0)))
        cast_shapes.append(jax.ShapeDtypeStruct((R, C), bf16))
    res = pl.pallas_call(
        functools.partial(_inproj_kernel, n_cast=len(casts)),
        grid=(n,),
        in_specs=[pl.BlockSpec((tm, D), lambda i: (i, 0)),
                  pl.BlockSpec((1, D), lambda i: (0, 0)),
                  _mod_spec(1, tm, L, per_seq),
                  _mod_spec(0, tm, L, per_seq),
                  pl.BlockSpec((None, D, UW), lambda i: (layer, 0, 0), pipeline_mode=pl.Buffered(1))]
        + cast_in_specs,
        out_specs=[pl.BlockSpec((tm, UW), lambda i: (i, 0)),
                   pl.BlockSpec((2 * SSM_HEADS, tm), lambda i: (0, i))] + cast_out_specs,
        out_shape=[jax.ShapeDtypeStruct((T, UW), f32),
                   jax.ShapeDtypeStruct((2 * SSM_HEADS, T), f32)] + cast_shapes,
        compiler_params=_cparams(("arbitrary",)),
        name="in_proj",
    )(x, norm_g.reshape(1, D), mod, mod, w_in_p, *casts)
    return res[0], res[1], res[2:]


def _outproj_kernel(oc_ref, os_ref, or_ref, oa_ref, w_ref, x_ref, g_ref, o_ref):
    for n0 in range(0, D, N_CHUNK):
        n1 = n0 + N_CHUNK
        acc = _nn(oc_ref[...], w_ref[0:GW, n0:n1])
        acc = acc + _nn(os_ref[...], w_ref[GW:2 * GW, n0:n1])
        acc = acc + _nn(or_ref[...], w_ref[2 * GW:3 * GW, n0:n1])
        acc = acc + _nn(oa_ref[...], w_ref[3 * GW:4 * GW, n0:n1])
        o_ref[:, n0:n1] = x_ref[:, n0:n1] + g_ref[:, n0:n1] * acc


def _outproj(o_conv, o_ssm, o_ret, o_att, w_out_b, x, mod, L, per_seq):
    T = x.shape[0]
    tm = 512
    ospec = pl.BlockSpec((tm, GW), lambda i: (i, 0))
    return pl.pallas_call(
        _outproj_kernel,
        grid=(T // tm,),
        in_specs=[ospec, ospec, ospec, ospec,
                  pl.BlockSpec((D, D), lambda i: (0, 0), pipeline_mode=pl.Buffered(1)),
                  pl.BlockSpec((tm, D), lambda i: (i, 0)),
                  _mod_spec(2, tm, L, per_seq)],
        out_specs=pl.BlockSpec((tm, D), lambda i: (i, 0)),
        out_shape=jax.ShapeDtypeStruct((T, D), f32),
        compiler_params=_cparams(("arbitrary",)),
        name="out_proj",
    )(o_conv, o_ssm, o_ret, o_att, w_out_b, x, mod)


def _mlp_kernel(x_ref, g_ref, sc_ref, sh_ref, g2_ref, fn_ref, w1_ref, w2_ref, o_ref, h_scr, *, final):
    f = pl.program_id(1)
    tm = x_ref.shape[0]
    RB = 256

    def ffn_tile(r):
        a = _nn(h_scr[r:r + 2 * RB, :], w1_ref[...])
        a = jnp.maximum(a, 0.0)
        a = (a * a).astype(bf16)
        return _nn(a, w2_ref[...])

    @pl.when(f == 0)
    def _():
        gs = g_ref[...] * (1.0 + sc_ref[...])
        for r in range(0, tm, RB):
            x = x_ref[r:r + RB, :]
            ms = jnp.mean(x * x, axis=-1, keepdims=True)
            h_scr[r:r + RB, :] = (x * lax.rsqrt(ms + EPS) * gs + sh_ref[...]).astype(bf16)
        for r in range(0, tm, 2 * RB):
            o_ref[r:r + 2 * RB, :] = ffn_tile(r)

    last = pl.num_programs(1) - 1

    @pl.when(jnp.logical_and(f > 0, f < last))
    def _():
        for r in range(0, tm, 2 * RB):
            o_ref[r:r + 2 * RB, :] += ffn_tile(r)

    @pl.when(f == last)
    def _():
        for r in range(0, tm, 2 * RB):
            acc = o_ref[r:r + 2 * RB, :] + ffn_tile(r)
            y = x_ref[r:r + 2 * RB, :] + g2_ref[...] * acc
            if final:
                ms = jnp.mean(y * y, axis=-1, keepdims=True)
                y = y * lax.rsqrt(ms + EPS) * fn_ref[...]
            o_ref[r:r + 2 * RB, :] = y


def _mlp(x, norm_g, mod, final_norm, w1_b, w2_b, L, per_seq, final):
    T = x.shape[0]
    tm, tf = 1024, 512
    return pl.pallas_call(
        functools.partial(_mlp_kernel, final=final),
        grid=(T // tm, D_FF // tf),
        in_specs=[pl.BlockSpec((tm, D), lambda i, f: (i, 0)),
                  pl.BlockSpec((1, D), lambda i, f: (0, 0)),
                  _mod_spec(4, tm, L, per_seq),
                  _mod_spec(3, tm, L, per_seq),
                  _mod_spec(5, tm, L, per_seq),
                  pl.BlockSpec((1, D), lambda i, f: (0, 0)),
                  pl.BlockSpec((D, tf), lambda i, f: (0, f)),
                  pl.BlockSpec((tf, D), lambda i, f: (f, 0))],
        out_specs=pl.BlockSpec((tm, D), lambda i, f: (i, 0)),
        out_shape=jax.ShapeDtypeStruct((T, D), f32),
        scratch_shapes=[pltpu.VMEM((tm, D), bf16)],
        compiler_params=_cparams(("arbitrary", "arbitrary")),
        name="mlp",
    )(x, norm_g.reshape(1, D), mod, mod, mod, final_norm.reshape(1, D), w1_b, w2_b)


def _conv_kernel(u_ref, w_ref, b_ref, lg_ref, lb_ref, o_ref, vpad_ref, ph_ref, *, L):
    RC = 64
    pad = 16
    SUB = 8
    off0 = pad - CONV_K // 2
    nph = RC + SUB * ((off0 + CONV_K - 1) // SUB)
    vpad_ref[0:pad, :] = jnp.zeros((pad, GW), f32)
    vpad_ref[pad + L:2 * pad + L, :] = jnp.zeros((pad, GW), f32)

    def fill(i, carry):
        r0 = pl.multiple_of(i * RC, RC)
        a = u_ref[pl.ds(r0, RC), 0:GW]
        g = u_ref[pl.ds(r0, RC), GW:2 * GW]
        vpad_ref[pl.ds(pad + r0, RC), :] = a * _sigmoid(g)
        return carry

    lax.fori_loop(0, L // RC, fill, 0)

    def body(i, carry):
        r0 = pl.multiple_of(i * RC, RC)
        win = vpad_ref.at[pl.ds(r0, RC + 2 * pad), :]
        accs = []
        for c0 in range(0, GW, 128):
            ls = slice(c0, c0 + 128)
            for ph in range(SUB):
                ph_ref[ph, :, ls] = win[ph:ph + nph, ls]
            acc = jnp.broadcast_to(b_ref[:, ls], (RC, 128))
            for k in range(CONV_K):
                a, ph = divmod(off0 + k, SUB)
                acc = acc + ph_ref[ph, SUB * a:SUB * a + RC, ls] * w_ref[k:k + 1, ls]
            accs.append(acc)
        acc = jnp.concatenate(accs, axis=1)
        mu = jnp.mean(acc, axis=-1, keepdims=True)
        xc = acc - mu
        var = jnp.mean(xc * xc, axis=-1, keepdims=True)
        y = xc * lax.rsqrt(var + EPS) * lg_ref[...] + lb_ref[...]
        o_ref[pl.ds(r0, RC), :] = _silu(y).astype(o_ref.dtype)
        return carry

    lax.fori_loop(0, L // RC, body, 0)


def _conv_module(u, conv_w, conv_b, ln_g, ln_b, nb, L):
    return pl.pallas_call(
        functools.partial(_conv_kernel, L=L),
        grid=(nb,),
        in_specs=[pl.BlockSpec((L, 2 * GW), lambda b: (b, C_CONV // (2 * GW))),
                  pl.BlockSpec((CONV_K, GW), lambda b: (0, 0)),
                  pl.BlockSpec((1, GW), lambda b: (0, 0)),
                  pl.BlockSpec((1, GW), lambda b: (0, 0)),
                  pl.BlockSpec((1, GW), lambda b: (0, 0))],
        out_specs=pl.BlockSpec((L, GW), lambda b: (b, 0)),
        out_shape=jax.ShapeDtypeStruct((nb * L, GW), bf16),
        scratch_shapes=[pltpu.VMEM((L + 32, GW), f32), pltpu.VMEM((8, 64 + 24, GW), f32)],
        compiler_params=_cparams(("arbitrary",)),
        name="conv_module",
    )(u, conv_w, conv_b.reshape(1, GW), ln_g.reshape(1, GW), ln_b.reshape(1, GW))


def _ret_kernel(*refs, L, has_h0, want_state, n_prev):
    ld_ref, q_ref, k_ref, v_ref, g_ref, gn_ref = refs[:6]
    pos = 6
    h0_ref = None
    if has_h0:
        h0_ref = refs[pos]
        pos += 1
    prev_ref = None
    if n_prev:
        prev_ref = refs[pos]
        pos += 1
    o_ref = refs[pos]
    pos += 1
    hout_ref = None
    if want_state:
        hout_ref = refs[pos]
        pos += 1
    y_scr, st_scr, dec_scr, cs_scr = refs[pos:pos + 4]

    nc = L // CHUNK
    H = RET_HEADS
    row = lax.broadcasted_iota(jnp.int32, (CHUNK, CHUNK), 0)
    col = lax.broadcasted_iota(jnp.int32, (CHUNK, CHUNK), 1)
    rowf = row.astype(f32)
    diff = (row - col).astype(f32)
    kscale = RET_D ** -0.5
    sls = [slice(h * RET_D, (h + 1) * RET_D) for h in range(H)]

    for d in range(2):
        for h in range(H):
            if has_h0:
                st_scr[d, h] = h0_ref[d, h]
            else:
                st_scr[d, h] = jnp.zeros((RET_D, RET_D), f32)
    for h in range(H):
        laf = ld_ref[h]
        lab = ld_ref[H + h]
        dec_scr[h] = jnp.where(col < row, jnp.exp(diff * laf),
                               jnp.where(col > row, jnp.exp(-diff * lab), 2.0))


    def intra(c, carry):
        r0 = pl.multiple_of(c * CHUNK, CHUNK)
        qb = [q_ref[pl.ds(r0, CHUNK), sls[h]].astype(bf16) for h in range(H)]
        kb = [(k_ref[pl.ds(r0, CHUNK), sls[h]] * kscale).astype(bf16) for h in range(H)]
        vh = [v_ref[pl.ds(r0, CHUNK), sls[h]] for h in range(H)]
        s = [_nt(qb[h], kb[h]) for h in range(H)]
        cf = [_tn((vh[h] * jnp.exp((CHUNK - 1.0 - rowf) * ld_ref[h])).astype(bf16), kb[h])
              for h in range(H)]
        cb = [_tn((vh[h] * jnp.exp(rowf * ld_ref[H + h])).astype(bf16), kb[h]) for h in range(H)]
        m = [(s[h] * dec_scr[h]).astype(bf16) for h in range(H)]
        y = [_nn(m[h], vh[h].astype(bf16)) for h in range(H)]
        for h in range(H):
            y_scr[pl.ds(r0, CHUNK), sls[h]] = y[h]
            cs_scr[0, c, h] = cf[h]
            cs_scr[1, c, h] = cb[h]
        return carry

    lax.fori_loop(0, nc, intra, 0, unroll=2)

    def inter(d, c):
        r0 = pl.multiple_of(c * CHUNK, CHUNK)
        st = [st_scr[d, h] for h in range(H)]
        yi = [_nt(q_ref[pl.ds(r0, CHUNK), sls[h]].astype(bf16), st[h].astype(bf16)) for h in range(H)]
        for h in range(H):
            la = ld_ref[d * H + h]
            st_scr[d, h] = st[h] * jnp.exp(jnp.full((RET_D, RET_D), CHUNK * la, f32)) + cs_scr[d, c, h]
        return r0, yi

    def fwd(c, carry):
        r0, yi = inter(0, c)
        for h in range(H):
            y_scr[pl.ds(r0, CHUNK), sls[h]] += jnp.exp((rowf + 1.0) * ld_ref[h]) * yi[h]
        return carry

    lax.fori_loop(0, nc, fwd, 0, unroll=2)

    def bwd(i, carry):
        r0, yi = inter(1, nc - 1 - i)
        for h in range(H):
            sl = sls[h]
            y = y_scr[pl.ds(r0, CHUNK), sl] + jnp.exp((CHUNK - rowf) * ld_ref[H + h]) * yi[h]
            mu = jnp.mean(y, axis=-1, keepdims=True)
            yc = y - mu
            var = jnp.mean(yc * yc, axis=-1, keepdims=True)
            yn = yc * lax.rsqrt(var + EPS) * gn_ref[:, sl]
            g = g_ref[pl.ds(r0, CHUNK), sl]
            o_ref[pl.ds(r0, CHUNK), sl] = (_silu(g) * yn).astype(o_ref.dtype)
        return carry

    lax.fori_loop(0, nc, bwd, 0, unroll=2)

    if want_state:
        for l in range(n_prev):
            hout_ref[l] = prev_ref[l]
        for d in range(2):
            for h in range(RET_HEADS):
                hout_ref[n_prev, d, h] = st_scr[d, h]


def _retention(u, log_decay, gn_g, h0, layer, nb, L, want_state, prev=None):
    has_h0 = h0 is not None
    n_prev = 0 if prev is None else prev.shape[1]
    sshape = (2, RET_HEADS, RET_D, RET_D)
    cblk = lambda c: pl.BlockSpec((L, GW), lambda b, c=c: (b, c // GW))
    in_specs = [pl.BlockSpec(memory_space=pltpu.SMEM),
                cblk(C_RQ), cblk(C_RK), cblk(C_RV), cblk(C_RG),
                pl.BlockSpec((1, GW), lambda b: (0, 0))]
    args = [log_decay.reshape(2 * RET_HEADS), u, u, u, u, gn_g.reshape(1, GW)]
    if has_h0:
        in_specs.append(pl.BlockSpec((None, None, 2, RET_HEADS, RET_D, RET_D),
                                     lambda b: (b, layer, 0, 0, 0, 0)))
        args.append(h0)
    if n_prev:
        in_specs.append(pl.BlockSpec((None, n_prev) + sshape, lambda b: (b, 0, 0, 0, 0, 0)))
        args.append(prev)
    out_specs = [pl.BlockSpec((L, GW), lambda b: (b, 0))]
    out_shape = [jax.ShapeDtypeStruct((nb * L, GW), bf16)]
    if want_state:
        out_specs.append(pl.BlockSpec((None, n_prev + 1) + sshape, lambda b: (b, 0, 0, 0, 0, 0)))
        out_shape.append(jax.ShapeDtypeStruct((nb, n_prev + 1) + sshape, f32))
    res = pl.pallas_call(
        functools.partial(_ret_kernel, L=L, has_h0=has_h0, want_state=want_state, n_prev=n_prev),
        grid=(nb,),
        in_specs=in_specs,
        out_specs=out_specs,
        out_shape=out_shape,
        scratch_shapes=[pltpu.VMEM((L, GW), f32),
                        pltpu.VMEM((2, RET_HEADS, RET_D, RET_D), f32),
                        pltpu.VMEM((RET_HEADS, CHUNK, CHUNK), f32),
                        pltpu.VMEM((2, L // CHUNK, RET_HEADS, RET_D, RET_D), f32)],
        compiler_params=_cparams(("arbitrary",)),
        name="retention",
    )(*args)
    return res if want_state else (res[0], None)


def _softplus(x):
    return jnp.maximum(x, 0.0) + jnp.log(1.0 + jnp.exp(-jnp.abs(x)))


def _ssm_kernel(*refs, L, has_h0, want_state, n_prev):
    (xbc_ref, z_ref, dtc_ref, dtr_ref, cw_ref, cb_ref, alr_ref, alc_ref, dbr_ref, dbc_ref,
     dsk_ref, ng_ref) = refs[:12]
    pos = 12
    h0_ref = None
    if has_h0:
        h0_ref = refs[pos]
        pos += 1
    prev_ref = None
    if n_prev:
        prev_ref = refs[pos]
        pos += 1
    o_ref = refs[pos]
    pos += 1
    hout_ref = None
    if want_state:
        hout_ref = refs[pos]
        pos += 1
    xpad_ref, xc_scr, y_scr, st_scr, es_scr, cs_scr, tot_scr = refs[pos:pos + 7]

    nc = L // CHUNK
    H, P, N = SSM_HEADS, SSM_P, SSM_N
    HG = H // SSM_GROUPS
    pad = 8
    row = lax.broadcasted_iota(jnp.int32, (CHUNK, CHUNK), 0)
    col = lax.broadcasted_iota(jnp.int32, (CHUNK, CHUNK), 1)
    lt01 = (row >= col).astype(bf16)
    ut01 = (row <= col).astype(bf16)
    sub16 = lax.broadcasted_iota(jnp.int32, (2 * H, CHUNK), 0)
    lane_lo = col < P

    xpad_ref[0:pad, :] = jnp.zeros((pad, 2 * GW), f32)
    xpad_ref[pad + L:2 * pad + L, :] = jnp.zeros((pad, 2 * GW), f32)

    def fill(i, carry):
        r0 = pl.multiple_of(i * CHUNK, CHUNK)
        xpad_ref[pl.ds(pad + r0, CHUNK), :] = xbc_ref[pl.ds(r0, CHUNK), :]
        return carry

    lax.fori_loop(0, nc, fill, 0)

    for d in range(2):
        for h in range(H):
            g, hh = divmod(h, HG)
            if has_h0:
                st_scr[d, g, hh * P:(hh + 1) * P, :] = h0_ref[d, h]
            else:
                st_scr[d, g, hh * P:(hh + 1) * P, :] = jnp.zeros((P, N), f32)

    a_neg_r = -jnp.exp(alr_ref[...])
    a_neg_c = -jnp.exp(alc_ref[...])

    def per_head_lanes(v, base):
        cols = []
        for c2 in range(H // 2):
            a = jnp.broadcast_to(v[:, base + 2 * c2:base + 2 * c2 + 1], (CHUNK, 2 * P))
            b = jnp.broadcast_to(v[:, base + 2 * c2 + 1:base + 2 * c2 + 2], (CHUNK, 2 * P))
            cols.append(jnp.where(lane_lo, a, b))
        return jnp.concatenate(cols, axis=1)

    def grp(x, g, base):
        return x[:, base + g * N:base + (g + 1) * N]


    def intra(c, carry):
        r0 = pl.multiple_of(c * CHUNK, CHUNK)
        win = xpad_ref.at[pl.ds(r0, CHUNK + 2 * pad), :]
        off = pad - SSM_K // 2
        for c0 in range(0, 2 * GW, 128):
            ls = slice(c0, c0 + 128)
            acc = jnp.broadcast_to(cb_ref[:, ls], (CHUNK, 128))
            for k in range(SSM_K):
                acc = acc + win[off + k:off + k + CHUNK, ls] * cw_ref[k:k + 1, ls]
            xc_scr[pl.ds(r0, CHUNK), ls] = _silu(acc)
        xc = xc_scr[pl.ds(r0, CHUNK), :]
        xs = xc[:, 0:GW]
        bmb = [grp(xc, g, GW).astype(bf16) for g in range(SSM_GROUPS)]
        cmb = [grp(xc, g, GW + SSM_GROUPS * N).astype(bf16) for g in range(SSM_GROUPS)]
        gmat = [_nt(cmb[g], bmb[g]) for g in range(SSM_GROUPS)]

        dt_c = _softplus(dtc_ref[pl.ds(r0, CHUNK), :] + dbr_ref[...])
        dt_r = _softplus(dtr_ref[:, pl.ds(r0, CHUNK)] + dbc_ref[...])
        lc3 = _split3(dt_c * a_neg_r)
        lr3 = _split3(dt_r * a_neg_c)
        pc = [_nn(lt01, t) for t in lc3]
        sc = [_nn(ut01, t) for t in lc3]
        pr = [_nn(t, ut01) for t in lr3]
        sr = [_nn(t, lt01) for t in lr3]
        a_c = jnp.where(col < H, pc[0] + pc[1] + pc[2], sc[0] + sc[1] + sc[2])
        a_r = jnp.where(sub16 < H, pr[0] + pr[1] + pr[2], sr[0] + sr[1] + sr[2])
        tot = jnp.where(col[0:1, :] < H, a_c[CHUNK - 1:CHUNK, :], a_c[0:1, :])
        tot_scr[c] = jnp.broadcast_to(tot, (8, 128))
        ks = dt_c * jnp.exp(tot - a_c)
        es = jnp.exp(a_c)
        es_scr[0, pl.ds(r0, CHUNK), :] = per_head_lanes(es, 0)
        es_scr[1, pl.ds(r0, CHUNK), :] = per_head_lanes(es, H)
        xwf = (xs * per_head_lanes(ks, 0)).astype(bf16)
        xwb = (xs * per_head_lanes(ks, H)).astype(bf16)
        csf = [_tn(xwf[:, g * HG * P:(g + 1) * HG * P], bmb[g]) for g in range(SSM_GROUPS)]
        csb = [_tn(xwb[:, g * HG * P:(g + 1) * HG * P], bmb[g]) for g in range(SSM_GROUPS)]
        for g in range(SSM_GROUPS):
            cs_scr[0, c, g] = csf[g]
            cs_scr[1, c, g] = csb[g]

        ms = []
        for h in range(H):
            df = jnp.exp(jnp.minimum(a_c[:, h:h + 1] - a_r[h:h + 1, :], 0.0)) * dt_r[h:h + 1, :]
            db = jnp.exp(jnp.minimum(a_c[:, H + h:H + h + 1] - a_r[H + h:H + h + 1, :], 0.0)) \
                * dt_r[H + h:H + h + 1, :]
            dm = jnp.where(col <= row, df, 0.0) + jnp.where(col >= row, db, 0.0)
            ms.append((gmat[h // HG] * dm).astype(bf16))
        ys = []
        for c2 in range(H // 2):
            xcol = xs[:, c2 * 2 * P:(c2 + 1) * 2 * P]
            x_lo = jnp.where(lane_lo, xcol, 0.0).astype(bf16)
            x_hi = jnp.where(lane_lo, 0.0, xcol).astype(bf16)
            ys.append(_nn(ms[2 * c2], x_lo) + _nn(ms[2 * c2 + 1], x_hi))
        y = jnp.concatenate(ys, axis=1)
        y_scr[pl.ds(r0, CHUNK), :] = y + (dsk_ref[0:1, :] + dsk_ref[1:2, :]) * xs
        return carry

    lax.fori_loop(0, nc, intra, 0, unroll=2)

    def inter(d, c):
        r0 = pl.multiple_of(c * CHUNK, CHUNK)
        st = [st_scr[d, g] for g in range(SSM_GROUPS)]
        yi = [_nt(grp(xc_scr[pl.ds(r0, CHUNK), :], g, GW + SSM_GROUPS * N).astype(bf16),
                  st[g].astype(bf16)) for g in range(SSM_GROUPS)]
        tot = tot_scr[c]
        for g in range(SSM_GROUPS):
            dec = jnp.concatenate(
                [jnp.broadcast_to(jnp.exp(tot[0:1, d * H + g * HG + hh:d * H + g * HG + hh + 1]), (P, N))
                 for hh in range(HG)], axis=0)
            st_scr[d, g] = st[g] * dec + cs_scr[d, c, g]
        return r0, jnp.concatenate(yi, axis=1) * es_scr[d, pl.ds(r0, CHUNK), :]

    def fwd(c, carry):
        r0, yi = inter(0, c)
        y_scr[pl.ds(r0, CHUNK), :] += yi
        return carry

    lax.fori_loop(0, nc, fwd, 0, unroll=2)

    def bwd(i, carry):
        r0, yi = inter(1, nc - 1 - i)
        yz = (y_scr[pl.ds(r0, CHUNK), :] + yi) * _silu(z_ref[pl.ds(r0, CHUNK), :])
        ms = jnp.mean(yz * yz, axis=-1, keepdims=True)
        o_ref[pl.ds(r0, CHUNK), :] = (yz * lax.rsqrt(ms + EPS) * ng_ref[...]).astype(o_ref.dtype)
        return carry

    lax.fori_loop(0, nc, bwd, 0, unroll=2)

    if want_state:
        for l in range(n_prev):
            hout_ref[l] = prev_ref[l]
        for d in range(2):
            for h in range(H):
                g, hh = divmod(h, HG)
                hout_ref[n_prev, d, h] = st_scr[d, g, hh * P:(hh + 1) * P, :]


def _pad_lanes(v, n=128):
    return jnp.pad(v, ((0, 0), (0, n - v.shape[1])))


def _ssm(u, dt_t, p, h0, layer, nb, L, want_state, prev=None):
    has_h0 = h0 is not None
    n_prev = 0 if prev is None else prev.shape[1]
    sshape = (2, SSM_HEADS, SSM_P, SSM_N)
    H = SSM_HEADS
    a_log = p['ssm_a_log'].reshape(1, 2 * H)
    dt_bias = p['ssm_dt_bias'].reshape(1, 2 * H)
    small = lambda shape: pl.BlockSpec(shape, lambda b: (0, 0))
    in_specs = [pl.BlockSpec((L, 2 * GW), lambda b: (b, C_XBC // (2 * GW))),
                pl.BlockSpec((L, GW), lambda b: (b, C_Z // GW)),
                pl.BlockSpec((L, 128), lambda b: (b, C_DT // 128)),
                pl.BlockSpec((2 * H, L), lambda b: (0, b)),
                small((SSM_K, 2 * GW)), small((1, 2 * GW)),
                small((1, 128)), small((2 * H, 128)), small((1, 128)), small((2 * H, 128)),
                small((2, GW)), small((1, GW))]
    args = [u, u, u, dt_t, p['ssm_conv_w'], p['ssm_conv_b'].reshape(1, 2 * GW),
            _pad_lanes(a_log), jnp.broadcast_to(a_log.reshape(2 * H, 1), (2 * H, 128)),
            _pad_lanes(dt_bias), jnp.broadcast_to(dt_bias.reshape(2 * H, 1), (2 * H, 128)),
            jnp.repeat(p['ssm_d'], SSM_P, axis=1), p['ssm_norm'].reshape(1, GW)]
    if has_h0:
        in_specs.append(pl.BlockSpec((None, None, 2, H, SSM_P, SSM_N), lambda b: (b, layer, 0, 0, 0, 0)))
        args.append(h0)
    out_specs = [pl.BlockSpec((L, GW), lambda b: (b, 0))]
    out_shape = [jax.ShapeDtypeStruct((nb * L, GW), bf16)]
    if n_prev:
        in_specs.append(pl.BlockSpec((None, n_prev) + sshape, lambda b: (b, 0, 0, 0, 0, 0)))
        args.append(prev)
    if want_state:
        out_specs.append(pl.BlockSpec((None, n_prev + 1) + sshape, lambda b: (b, 0, 0, 0, 0, 0)))
        out_shape.append(jax.ShapeDtypeStruct((nb, n_prev + 1) + sshape, f32))
    res = pl.pallas_call(
        functools.partial(_ssm_kernel, L=L, has_h0=has_h0, want_state=want_state, n_prev=n_prev),
        grid=(nb,),
        in_specs=in_specs,
        out_specs=out_specs,
        out_shape=out_shape,
        scratch_shapes=[pltpu.VMEM((L + 16, 2 * GW), f32),
                        pltpu.VMEM((L, 2 * GW), f32),
                        pltpu.VMEM((L, GW), f32),
                        pltpu.VMEM((2, SSM_GROUPS, H // SSM_GROUPS * SSM_P, SSM_N), f32),
                        pltpu.VMEM((2, L, GW), f32),
                        pltpu.VMEM((2, L // CHUNK, SSM_GROUPS, H // SSM_GROUPS * SSM_P, SSM_N), f32),
                        pltpu.VMEM((L // CHUNK, 8, 128), f32)],
        compiler_params=_cparams(("arbitrary",)),
        name="ssd_mixer",
    )(*args)
    return res if want_state else (res[0], None)


def _ctx_att_kernel(sink_ref, q_ref, k_ref, v_ref, o_ref, *, L):
    G = ATT_HEADS // ATT_KV
    scale = ATT_D ** -0.5
    kb = [k_ref[:, j * ATT_D:(j + 1) * ATT_D].astype(bf16) for j in range(ATT_KV)]
    vb = [v_ref[:, j * ATT_D:(j + 1) * ATT_D].astype(bf16) for j in range(ATT_KV)]
    s = [_nt((q_ref[:, h * ATT_D:(h + 1) * ATT_D] * scale).astype(bf16), kb[h // G])
         for h in range(ATT_HEADS)]
    p, den = [], []
    for h in range(ATT_HEADS):
        sink = sink_ref[h]
        m = jnp.maximum(jnp.max(s[h], axis=-1, keepdims=True), sink)
        e = jnp.exp(s[h] - m)
        den.append(jnp.sum(e, axis=-1, keepdims=True) + jnp.exp(sink - m))
        p.append(e.astype(bf16))
    o = [_nn(p[h], vb[h // G]) for h in range(ATT_HEADS)]
    for h in range(ATT_HEADS):
        o_ref[:, h * ATT_D:(h + 1) * ATT_D] = (o[h] / den[h]).astype(o_ref.dtype)


def _ctx_attention(u, sink, nb, L):
    return pl.pallas_call(
        functools.partial(_ctx_att_kernel, L=L),
        grid=(nb,),
        in_specs=[pl.BlockSpec(memory_space=pltpu.SMEM),
                  pl.BlockSpec((L, GW), lambda b: (b, C_AQ // GW)),
                  pl.BlockSpec((L, 128), lambda b: (b, C_AK // 128)),
                  pl.BlockSpec((L, 128), lambda b: (b, C_AV // 128))],
        out_specs=pl.BlockSpec((L, GW), lambda b: (b, 0)),
        out_shape=jax.ShapeDtypeStruct((nb * L, GW), bf16),
        compiler_params=_cparams(("arbitrary",)),
        name="ctx_attention",
    )(sink, u, u, u)


def _rope(x, cos, sin):
    w = x.shape[1]
    lane = lax.broadcasted_iota(jnp.int32, x.shape, 1)
    first = (lane % 32) < 16
    rot = jnp.where(first, -pltpu.roll(x, w - 16, 1), pltpu.roll(x, 16, 1))
    return x * cos + rot * sin


def _half_variants(x, keep_fill):
    lane = lax.broadcasted_iota(jnp.int32, x.shape, 1)
    lo = lane < ATT_D
    a0 = jnp.where(lo, x, keep_fill)
    b1 = jnp.where(lo, keep_fill, x)
    xr = pltpu.roll(x, ATT_D, 1)
    a1 = jnp.where(lo, xr, keep_fill)
    b0 = jnp.where(lo, keep_fill, xr)
    return ((a0, b0), (a1, b1))


def _half_variants_t(xt):
    sub = lax.broadcasted_iota(jnp.int32, xt.shape, 0)
    lo = sub < ATT_D
    a0 = jnp.where(lo, xt, 0.0)
    b1 = jnp.where(lo, 0.0, xt)
    xr = pltpu.roll(xt, ATT_D, 0)
    a1 = jnp.where(lo, xr, 0.0)
    b0 = jnp.where(lo, 0.0, xr)
    return ((a0, b0), (a1, b1))


def _lat_att_kernel(sink_ref, q_ref, k_ref, v_ref, kc_ref, vc_ref, cos_ref, sin_ref, o_ref,
                    qr_scr, kt_scr, vv_scr, kct_scr, vcv_scr, bias_scr, *, L):
    G = ATT_HEADS // ATT_KV
    B = CHUNK
    nb = L // B
    scale = ATT_D ** -0.5
    zeros = jnp.zeros((B, 128), bf16)
    for j in range(ATT_KV):
        for hf in range(2):
            kt_scr[j, hf, 0] = zeros
            kt_scr[j, hf, nb + 1] = zeros
            vv_scr[j, hf, 0:B, :] = zeros
            vv_scr[j, hf, B + L:2 * B + L, :] = zeros
    kct = _half_variants_t(kc_ref[...].T)
    vcv = _half_variants(vc_ref[...], 1.0)
    for j in range(ATT_KV):
        for hf in range(2):
            kct_scr[j, hf] = kct[j][hf].astype(bf16)
            vcv_scr[j, hf] = vcv[j][hf].astype(bf16)

    rowi = lax.broadcasted_iota(jnp.int32, (2 * B, 3 * B), 0) % B
    coli = lax.broadcasted_iota(jnp.int32, (2 * B, 3 * B), 1)
    inwin = jnp.abs(rowi - (coli - B)) <= ATT_WIN
    ninf = jnp.float32(-jnp.inf)
    bias_scr[0] = jnp.where(inwin & (coli >= B), 0.0, ninf)
    bias_scr[1] = jnp.where(inwin, 0.0, ninf)
    bias_scr[2] = jnp.where(inwin & (coli < 2 * B), 0.0, ninf)

    def prep(n, carry):
        r0 = pl.multiple_of(n * B, B)
        cos = cos_ref[pl.ds(r0, B), :]
        sin = sin_ref[pl.ds(r0, B), :]
        cos4 = jnp.concatenate([cos] * 4, axis=1)
        sin4 = jnp.concatenate([sin] * 4, axis=1)
        qr_scr[pl.ds(r0, B), :] = (_rope(q_ref[pl.ds(r0, B), :], cos4, sin4) * scale).astype(bf16)
        kvar = _half_variants_t(_rope(k_ref[pl.ds(r0, B), :], cos, sin).T)
        vvar = _half_variants(v_ref[pl.ds(r0, B), :], 1.0)
        for j in range(ATT_KV):
            for hf in range(2):
                kt_scr[j, hf, n + 1] = kvar[j][hf].astype(bf16)
                vv_scr[j, hf, pl.ds(B + r0, B), :] = vvar[j][hf].astype(bf16)
        return carry

    lax.fori_loop(0, nb, prep, 0)

    lane_lo = lax.broadcasted_iota(jnp.int32, (2 * B, 128), 1) < ATT_D

    combos = [(j, hf) for j in range(ATT_KV) for hf in range(2)]

    def scores(n):
        r0 = pl.multiple_of(n * B, B)
        qs = [jnp.concatenate([qr_scr[pl.ds(r0, B), (2 * j) * 128:(2 * j + 1) * 128],
                               qr_scr[pl.ds(r0, B), (2 * j + 1) * 128:(2 * j + 2) * 128]], axis=0)
              for j in range(ATT_KV)]
        s_c = [_nn(qs[j], kct_scr[j, hf]) for j, hf in combos]
        s_b = [jnp.concatenate([_nn(qs[j], kt_scr[j, hf, n + t]) for t in range(3)], axis=1)
               for j, hf in combos]
        return s_c, s_b

    def finish(n, s_c, s_b):
        r0 = pl.multiple_of(n * B, B)
        bias = bias_scr[jnp.where(n == 0, 0, jnp.where(n == nb - 1, 2, 1))]
        p_c, p_b, esink = [], [], []
        for i, (j, hf) in enumerate(combos):
            sb = s_b[i] + bias
            sink = jnp.concatenate([jnp.full((B, 1), sink_ref[G * j + hf], f32),
                                    jnp.full((B, 1), sink_ref[G * j + 2 + hf], f32)], axis=0)
            m = jnp.maximum(jnp.maximum(jnp.max(s_c[i], axis=-1, keepdims=True),
                                        jnp.max(sb, axis=-1, keepdims=True)), sink)
            p_c.append(jnp.exp(s_c[i] - m).astype(bf16))
            p_b.append(jnp.exp(sb - m).astype(bf16))
            esink.append(jnp.exp(sink - m))
        oe = [_nn(p_c[i], vcv_scr[j, hf]) + _nn(p_b[i], vv_scr[j, hf, pl.ds(r0, 3 * B), :])
              for i, (j, hf) in enumerate(combos)]
        outs = []
        for i, (j, hf) in enumerate(combos):
            den = oe[i][:, (1 - hf) * ATT_D:(1 - hf) * ATT_D + 1] + esink[i]
            outs.append(oe[i] / den)
        for j in range(ATT_KV):
            o = jnp.where(lane_lo, outs[2 * j], outs[2 * j + 1])
            o_ref[pl.ds(r0, B), (2 * j) * 128:(2 * j + 1) * 128] = o[0:B].astype(o_ref.dtype)
            o_ref[pl.ds(r0, B), (2 * j + 1) * 128:(2 * j + 2) * 128] = o[B:2 * B].astype(o_ref.dtype)

    def blk_pair(i, carry):
        sa = scores(2 * i)
        sb = scores(2 * i + 1)
        finish(2 * i, *sa)
        finish(2 * i + 1, *sb)
        return carry

    lax.fori_loop(0, nb // 2, blk_pair, 0)


def _rope_tables(L):
    pos = jnp.arange(L)
    rows = (pos // GRID_W).astype(f32)
    cols = (pos % GRID_W).astype(f32)
    half = ATT_D // 4
    freqs = ROPE_THETA ** (-jnp.arange(half, dtype=f32) / half)
    ang_r = rows[:, None] * freqs
    ang_c = cols[:, None] * freqs
    ang = jnp.concatenate([ang_r, ang_r, ang_c, ang_c], axis=1)
    ang = jnp.concatenate([ang, ang], axis=1)
    return jnp.cos(ang), jnp.sin(ang)


def _lat_attention(u, k_cache, v_cache, sink, layer, nb, L):
    Lc = k_cache.shape[2]
    cos, sin = _rope_tables(L)
    kc = k_cache.reshape(nb, DEPTH, Lc, ATT_KV * ATT_D)
    vc = v_cache.reshape(nb, DEPTH, Lc, ATT_KV * ATT_D)
    cspec = pl.BlockSpec((None, None, Lc, 128), lambda b: (b, layer, 0, 0))
    return pl.pallas_call(
        functools.partial(_lat_att_kernel, L=L),
        grid=(nb,),
        in_specs=[pl.BlockSpec(memory_space=pltpu.SMEM),
                  pl.BlockSpec((L, GW), lambda b: (b, C_AQ // GW)),
                  pl.BlockSpec((L, 128), lambda b: (b, C_AK // 128)),
                  pl.BlockSpec((L, 128), lambda b: (b, C_AV // 128)),
                  cspec, cspec,
                  pl.BlockSpec((L, 128), lambda b: (0, 0)),
                  pl.BlockSpec((L, 128), lambda b: (0, 0))],
        out_specs=pl.BlockSpec((L, GW), lambda b: (b, 0)),
        out_shape=jax.ShapeDtypeStruct((nb * L, GW), bf16),
        scratch_shapes=[pltpu.VMEM((L, GW), bf16),
                        pltpu.VMEM((ATT_KV, 2, L // CHUNK + 2, 128, CHUNK), bf16),
                        pltpu.VMEM((ATT_KV, 2, L + 2 * CHUNK, 128), bf16),
                        pltpu.VMEM((ATT_KV, 2, 128, Lc), bf16),
                        pltpu.VMEM((ATT_KV, 2, Lc, 128), bf16),
                        pltpu.VMEM((3, 2 * CHUNK, 3 * CHUNK), f32)],
        compiler_params=_cparams(("arbitrary",)),
        name="lat_attention",
    )(sink, u, u, u, kc, vc, cos, sin)


def _permute_w_in_kernel(wt_ref, o_ref):
    ndt = 2 * SSM_HEADS
    moves = [(0, C_CONV, 1024), (1536, C_XBC, 1024), (1024, C_Z, GW), (2560 + ndt, C_RQ, C_DT - C_RQ)]
    for src, dst, n in moves:
        for r in range(0, n, 128):
            o_ref[:, dst + r:dst + r + 128] = wt_ref[src + r:src + r + 128, :].T.astype(bf16)
    tail = wt_ref[2560:2560 + 128, :].T
    lane = lax.broadcasted_iota(jnp.int32, tail.shape, 1)
    o_ref[:, C_DT:UW] = jnp.where(lane < ndt, tail, 0.0).astype(bf16)


def _permute_w_in(w):
    tr = 256
    return pl.pallas_call(
        _permute_w_in_kernel,
        grid=(DEPTH, D // tr),
        in_specs=[pl.BlockSpec((None, N_IN, tr), lambda l, i: (l, 0, i))],
        out_specs=pl.BlockSpec((None, tr, UW), lambda l, i: (l, i, 0)),
        out_shape=jax.ShapeDtypeStruct((DEPTH, D, UW), bf16),
        compiler_params=_cparams(("arbitrary", "arbitrary")),
        name="permute_w_in",
    )(jnp.swapaxes(w, 1, 2))


def _mix_and_mlp(x, u, dt_t, mod, p, ctx, layer, nb, L, final, prev_ssm=None, prev_ret=None):
    per_seq = ctx is not None
    want_state = ctx is None
    o_conv = _conv_module(u, p['conv_w'], p['conv_b'], p['conv_ln_g'], p['conv_ln_b'], nb, L)
    if ctx is None:
        o_ssm, h_ssm = _ssm(u, dt_t, p, None, layer, nb, L, True, prev_ssm)
        o_ret, h_ret = _retention(u, p['ret_log_decay'], p['ret_gn_g'], None, layer, nb, L, True, prev_ret)
        o_att = _ctx_attention(u, p['att_sink'], nb, L)
    else:
        k_c, v_c, s_ssm, s_ret = ctx
        o_ssm, h_ssm = _ssm(u, dt_t, p, s_ssm, layer, nb, L, False)
        o_ret, h_ret = _retention(u, p['ret_log_decay'], p['ret_gn_g'], s_ret, layer, nb, L, False)
        o_att = _lat_attention(u, k_c, v_c, p['att_sink'], layer, nb, L)
    x1 = _outproj(o_conv, o_ssm, o_ret, o_att, p['w_out_b'], x, mod, L, per_seq)
    x2 = _mlp(x1, p['norm_mlp'], mod, p['final_norm'], p['w1_b'], p['w2_b'], L, per_seq, final)
    states = None
    if want_state:
        ak = u[:, C_AK:C_AK + 128].reshape(nb, L, ATT_KV, ATT_D)
        av = u[:, C_AV:C_AV + 128].reshape(nb, L, ATT_KV, ATT_D)
        states = (ak, av, h_ssm, h_ret)
    return x2, states


def kernel(x_prompt, x_sample, cache_attn_k, cache_attn_v, state_ssm, state_ret, c, c_ctx, ada_w, ada_b, norm_mix, norm_mlp, w_in, conv_w, conv_b, conv_ln_g, conv_ln_b, ssm_conv_w, ssm_conv_b, ssm_a_log, ssm_dt_bias, ssm_d, ssm_norm, ret_log_decay, ret_gn_g, att_sink, w_out, w1, w2, final_norm):
    nbp, Lp, _ = x_prompt.shape
    nbs, Ls, _ = x_sample.shape
    cvec = jnp.concatenate([c_ctx[None, :], c, jnp.zeros((16 - 1 - nbs, D), f32)], axis=0)
    mod = _ada_mod(cvec, ada_w, ada_b)
    y_p = x_prompt.reshape(nbp * Lp, D)
    y_s = x_sample.reshape(nbs * Ls, D)
    new_k, new_v = [], []
    hs, hr = None, None
    w_in_p = _permute_w_in(w_in)
    ctx = (cache_attn_k, cache_attn_v, state_ssm, state_ret)
    for l in range(DEPTH):
        mod_l = mod[l].reshape(16, 6, 1, D)
        mod_p, mod_s = mod_l[0:1], mod_l[1:1 + nbs]
        u_p, dtt_p, (w_out_b,) = _inproj(y_p, norm_mix[l], mod_p, w_in_p, l, Lp, False, [w_out])
        u_s, dtt_s, (w1_b, w2_b) = _inproj(y_s, norm_mix[l], mod_s, w_in_p, l, Ls, True, [w1, w2])
        p = dict(norm_mlp=norm_mlp[l],
                 conv_w=conv_w[l], conv_b=conv_b[l], conv_ln_g=conv_ln_g[l], conv_ln_b=conv_ln_b[l],
                 ssm_conv_w=ssm_conv_w[l], ssm_conv_b=ssm_conv_b[l], ssm_a_log=ssm_a_log[l],
                 ssm_dt_bias=ssm_dt_bias[l], ssm_d=ssm_d[l], ssm_norm=ssm_norm[l],
                 ret_log_decay=ret_log_decay[l], ret_gn_g=ret_gn_g[l], att_sink=att_sink[l],
                 w_out_b=w_out_b, w1_b=w1_b, w2_b=w2_b, final_norm=final_norm)
        final = l == DEPTH - 1
        y_p, (k_l, v_l, hs, hr) = _mix_and_mlp(y_p, u_p, dtt_p, mod_p, p, None, l, nbp, Lp, final, hs, hr)
        new_k.append(k_l)
        new_v.append(v_l)
        y_s, _ = _mix_and_mlp(y_s, u_s, dtt_s, mod_s, p, ctx, l, nbs, Ls, final)
    return (y_p.reshape(nbp, Lp, D), y_s.reshape(nbs, Ls, D),
            jnp.stack(new_k, axis=1), jnp.stack(new_v, axis=1), hs, hr)
```

```python
import functools

import jax
import jax.numpy as jnp
from jax import lax
from jax.experimental import pallas as pl
from jax.experimental.pallas import tpu as pltpu

f32 = jnp.float32
bf16 = jnp.bfloat16

D = 2048
DEPTH = 2
GW = 512
CONV_K = 31
SSM_HEADS, SSM_P, SSM_N, SSM_GROUPS, SSM_K = 8, 64, 128, 2, 5
RET_HEADS, RET_D = 4, 128
ATT_HEADS, ATT_KV, ATT_D, ATT_WIN = 8, 2, 64, 128
GRID_W = 64
ROPE_THETA = 10000.0
CHUNK = 128
D_FF = 4 * D
EPS = 1e-6

C_CONV, C_XBC, C_Z, C_RQ, C_RK, C_RV, C_RG, C_AQ, C_AK, C_AV, C_DT = (
    0, 1024, 2048, 2560, 3072, 3584, 4096, 4608, 5120, 5248, 5376)
UW = 5504
N_IN = 5392
N_CHUNK = 512

VMEM_LIMIT = 56 * 1024 * 1024


def _cparams(sem):
    return pltpu.CompilerParams(dimension_semantics=sem, vmem_limit_bytes=VMEM_LIMIT)


def _sigmoid(x):
    return 1.0 / (1.0 + jnp.exp(-x))


def _silu(x):
    return x * _sigmoid(x)


def _nt(a, b):
    return lax.dot_general(a, b, (((1,), (1,)), ((), ())), preferred_element_type=f32)


def _tn(a, b):
    return lax.dot_general(a, b, (((0,), (0,)), ((), ())), preferred_element_type=f32)


def _nn(a, b):
    return jnp.dot(a, b, preferred_element_type=f32)


def _split3(x):
    hi = x.astype(bf16)
    r1 = x - hi.astype(f32)
    mid = r1.astype(bf16)
    lo = (r1 - mid.astype(f32)).astype(bf16)
    return hi, mid, lo


def _ada_kernel(c_ref, w_ref, b_ref, o_ref):
    c = c_ref[...]
    s = _silu(c).astype(bf16)
    o_ref[...] = _nn(s, w_ref[...].astype(bf16)) + b_ref[...]


def _ada_mod(cvec, ada_w, ada_b):
    tn = 1024
    return pl.pallas_call(
        _ada_kernel,
        grid=(DEPTH, 6 * D // tn),
        in_specs=[pl.BlockSpec((16, D), lambda l, j: (0, 0)),
                  pl.BlockSpec((None, D, tn), lambda l, j: (l, 0, j)),
                  pl.BlockSpec((None, 1, tn), lambda l, j: (l, 0, j))],
        out_specs=pl.BlockSpec((None, 16, tn), lambda l, j: (l, 0, j)),
        out_shape=jax.ShapeDtypeStruct((DEPTH, 16, 6 * D), f32),
        compiler_params=_cparams(("arbitrary", "arbitrary")),
        name="ada_mod",
    )(cvec, ada_w, ada_b.reshape(DEPTH, 1, 6 * D))


def _mod_spec(idx, tm, L, per_seq):
    if per_seq:
        return pl.BlockSpec((None, None, 1, D), lambda i, *_: ((i * tm) // L, idx, 0, 0))
    return pl.BlockSpec((None, None, 1, D), lambda i, *_: (0, idx, 0, 0))


def _inproj_kernel(x_ref, g_ref, sc_ref, sh_ref, w_ref, *refs, n_cast):
    cast_in, u_ref, dtt_ref, cast_out = refs[:n_cast], refs[n_cast], refs[n_cast + 1], refs[n_cast + 2:]
    x = x_ref[...]
    ms = jnp.mean(x * x, axis=-1, keepdims=True)
    h = x * lax.rsqrt(ms + EPS) * (g_ref[...] * (1.0 + sc_ref[...])) + sh_ref[...]
    hb = h.astype(bf16)
    for n0 in range(0, UW, N_CHUNK):
        n1 = min(n0 + N_CHUNK, UW)
        u_ref[:, n0:n1] = _nn(hb, w_ref[:, n0:n1])
    dtt_ref[...] = u_ref[:, C_DT:C_DT + 128].T[0:2 * SSM_HEADS, :]
    for src, dst in zip(cast_in, cast_out):
        dst[...] = src[...].astype(bf16)


def _inproj(x, norm_g, mod, w_in_p, layer, L, per_seq, casts):
    T = x.shape[0]
    tm = 256
    n = T // tm
    cast_in_specs, cast_out_specs, cast_shapes = [], [], []
    for w in casts:
        _, R, C = w.shape
        cast_in_specs.append(pl.BlockSpec((None, R // n, C), lambda i: (layer, i, 0)))
        cast_out_specs.append(pl.BlockSpec((R // n, C), lambda i: (i, 0)))
        cast_shapes.append(jax.ShapeDtypeStruct((R, C), bf16))
    res = pl.pallas_call(
        functools.partial(_inproj_kernel, n_cast=len(casts)),
        grid=(n,),
        in_specs=[pl.BlockSpec((tm, D), lambda i: (i, 0)),
                  pl.BlockSpec((1, D), lambda i: (0, 0)),
                  _mod_spec(1, tm, L, per_seq),
                  _mod_spec(0, tm, L, per_seq),
                  pl.BlockSpec((None, D, UW), lambda i: (layer, 0, 0), pipeline_mode=pl.Buffered(1))]
        + cast_in_specs,
        out_specs=[pl.BlockSpec((tm, UW), lambda i: (i, 0)),
                   pl.BlockSpec((2 * SSM_HEADS, tm), lambda i: (0, i))] + cast_out_specs,
        out_shape=[jax.ShapeDtypeStruct((T, UW), f32),
                   jax.ShapeDtypeStruct((2 * SSM_HEADS, T), f32)] + cast_shapes,
        compiler_params=_cparams(("arbitrary",)),
        name="in_proj",
    )(x, norm_g.reshape(1, D), mod, mod, w_in_p, *casts)
    return res[0], res[1], res[2:]


def _outproj_kernel(oc_ref, os_ref, or_ref, oa_ref, w_ref, x_ref, g_ref, o_ref):
    for n0 in range(0, D, N_CHUNK):
        n1 = n0 + N_CHUNK
        acc = _nn(oc_ref[...], w_ref[0:GW, n0:n1])
        acc = acc + _nn(os_ref[...], w_ref[GW:2 * GW, n0:n1])
        acc = acc + _nn(or_ref[...], w_ref[2 * GW:3 * GW, n0:n1])
        acc = acc + _nn(oa_ref[...], w_ref[3 * GW:4 * GW, n0:n1])
        o_ref[:, n0:n1] = x_ref[:, n0:n1] + g_ref[:, n0:n1] * acc


def _outproj(o_conv, o_ssm, o_ret, o_att, w_out_b, x, mod, L, per_seq):
    T = x.shape[0]
    tm = 512
    ospec = pl.BlockSpec((tm, GW), lambda i: (i, 0))
    return pl.pallas_call(
        _outproj_kernel,
        grid=(T // tm,),
        in_specs=[ospec, ospec, ospec, ospec,
                  pl.BlockSpec((D, D), lambda i: (0, 0), pipeline_mode=pl.Buffered(1)),
                  pl.BlockSpec((tm, D), lambda i: (i, 0)),
                  _mod_spec(2, tm, L, per_seq)],
        out_specs=pl.BlockSpec((tm, D), lambda i: (i, 0)),
        out_shape=jax.ShapeDtypeStruct((T, D), f32),
        compiler_params=_cparams(("arbitrary",)),
        name="out_proj",
    )(o_conv, o_ssm, o_ret, o_att, w_out_b, x, mod)


def _mlp_kernel(x_ref, g_ref, sc_ref, sh_ref, g2_ref, fn_ref, w1_ref, w2_ref, o_ref, h_scr, *, final):
    f = pl.program_id(1)
    tm = x_ref.shape[0]
    RB = 256

    def ffn_tile(r):
        a = _nn(h_scr[r:r + 2 * RB, :], w1_ref[...])
        a = jnp.maximum(a, 0.0)
        a = (a * a).astype(bf16)
        return _nn(a, w2_ref[...])

    @pl.when(f == 0)
    def _():
        gs = g_ref[...] * (1.0 + sc_ref[...])
        for r in range(0, tm, RB):
            x = x_ref[r:r + RB, :]
            ms = jnp.mean(x * x, axis=-1, keepdims=True)
            h_scr[r:r + RB, :] = (x * lax.rsqrt(ms + EPS) * gs + sh_ref[...]).astype(bf16)
        for r in range(0, tm, 2 * RB):
            o_ref[r:r + 2 * RB, :] = ffn_tile(r)

    last = pl.num_programs(1) - 1

    @pl.when(jnp.logical_and(f > 0, f < last))
    def _():
        for r in range(0, tm, 2 * RB):
            o_ref[r:r + 2 * RB, :] += ffn_tile(r)

    @pl.when(f == last)
    def _():
        for r in range(0, tm, 2 * RB):
            acc = o_ref[r:r + 2 * RB, :] + ffn_tile(r)
            y = x_ref[r:r + 2 * RB, :] + g2_ref[...] * acc
            if final:
                ms = jnp.mean(y * y, axis=-1, keepdims=True)
                y = y * lax.rsqrt(ms + EPS) * fn_ref[...]
            o_ref[r:r + 2 * RB, :] = y


def _mlp(x, norm_g, mod, final_norm, w1_b, w2_b, L, per_seq, final):
    T = x.shape[0]
    tm, tf = 1024, 512
    return pl.pallas_call(
        functools.partial(_mlp_kernel, final=final),
        grid=(T // tm, D_FF // tf),
        in_specs=[pl.BlockSpec((tm, D), lambda i, f: (i, 0)),
                  pl.BlockSpec((1, D), lambda i, f: (0, 0)),
                  _mod_spec(4, tm, L, per_seq),
                  _mod_spec(3, tm, L, per_seq),
                  _mod_spec(5, tm, L, per_seq),
                  pl.BlockSpec((1, D), lambda i, f: (0, 0)),
                  pl.BlockSpec((D, tf), lambda i, f: (0, f)),
                  pl.BlockSpec((tf, D), lambda i, f: (f, 0))],
        out_specs=pl.BlockSpec((tm, D), lambda i, f: (i, 0)),
        out_shape=jax.ShapeDtypeStruct((T, D), f32),
        scratch_shapes=[pltpu.VMEM((tm, D), bf16)],
        compiler_params=_cparams(("arbitrary", "arbitrary")),
        name="mlp",
    )(x, norm_g.reshape(1, D), mod, mod, mod, final_norm.reshape(1, D), w1_b, w2_b)


def _conv_kernel(u_ref, w_ref, b_ref, lg_ref, lb_ref, o_ref, vpad_ref, ph_ref, *, L):
    RC = 64
    pad = 16
    SUB = 8
    off0 = pad - CONV_K // 2
    nph = RC + SUB * ((off0 + CONV_K - 1) // SUB)
    vpad_ref[0:pad, :] = jnp.zeros((pad, GW), f32)
    vpad_ref[pad + L:2 * pad + L, :] = jnp.zeros((pad, GW), f32)

    def fill(i, carry):
        r0 = pl.multiple_of(i * RC, RC)
        a = u_ref[pl.ds(r0, RC), 0:GW]
        g = u_ref[pl.ds(r0, RC), GW:2 * GW]
        vpad_ref[pl.ds(pad + r0, RC), :] = a * _sigmoid(g)
        return carry

    lax.fori_loop(0, L // RC, fill, 0)

    def body(i, carry):
        r0 = pl.multiple_of(i * RC, RC)
        win = vpad_ref.at[pl.ds(r0, RC + 2 * pad), :]
        accs = []
        for c0 in range(0, GW, 128):
            ls = slice(c0, c0 + 128)
            for ph in range(SUB):
                ph_ref[ph, :, ls] = win[ph:ph + nph, ls]
            acc = jnp.broadcast_to(b_ref[:, ls], (RC, 128))
            for k in range(CONV_K):
                a, ph = divmod(off0 + k, SUB)
                acc = acc + ph_ref[ph, SUB * a:SUB * a + RC, ls] * w_ref[k:k + 1, ls]
            accs.append(acc)
        acc = jnp.concatenate(accs, axis=1)
        mu = jnp.mean(acc, axis=-1, keepdims=True)
        xc = acc - mu
        var = jnp.mean(xc * xc, axis=-1, keepdims=True)
        y = xc * lax.rsqrt(var + EPS) * lg_ref[...] + lb_ref[...]
        o_ref[pl.ds(r0, RC), :] = _silu(y).astype(o_ref.dtype)
        return carry

    lax.fori_loop(0, L // RC, body, 0)


def _conv_module(u, conv_w, conv_b, ln_g, ln_b, nb, L):
    return pl.pallas_call(
        functools.partial(_conv_kernel, L=L),
        grid=(nb,),
        in_specs=[pl.BlockSpec((L, 2 * GW), lambda b: (b, C_CONV // (2 * GW))),
                  pl.BlockSpec((CONV_K, GW), lambda b: (0, 0)),
                  pl.BlockSpec((1, GW), lambda b: (0, 0)),
                  pl.BlockSpec((1, GW), lambda b: (0, 0)),
                  pl.BlockSpec((1, GW), lambda b: (0, 0))],
        out_specs=pl.BlockSpec((L, GW), lambda b: (b, 0)),
        out_shape=jax.ShapeDtypeStruct((nb * L, GW), bf16),
        scratch_shapes=[pltpu.VMEM((L + 32, GW), f32), pltpu.VMEM((8, 64 + 24, GW), f32)],
        compiler_params=_cparams(("arbitrary",)),
        name="conv_module",
    )(u, conv_w, conv_b.reshape(1, GW), ln_g.reshape(1, GW), ln_b.reshape(1, GW))


def _ret_kernel(*refs, L, has_h0, want_state, n_prev):
    ld_ref, q_ref, k_ref, v_ref, g_ref, gn_ref = refs[:6]
    pos = 6
    h0_ref = None
    if has_h0:
        h0_ref = refs[pos]
        pos += 1
    prev_ref = None
    if n_prev:
        prev_ref = refs[pos]
        pos += 1
    o_ref = refs[pos]
    pos += 1
    hout_ref = None
    if want_state:
        hout_ref = refs[pos]
        pos += 1
    y_scr, st_scr, dec_scr, cs_scr = refs[pos:pos + 4]

    nc = L // CHUNK
    H = RET_HEADS
    row = lax.broadcasted_iota(jnp.int32, (CHUNK, CHUNK), 0)
    col = lax.broadcasted_iota(jnp.int32, (CHUNK, CHUNK), 1)
    rowf = row.astype(f32)
    diff = (row - col).astype(f32)
    kscale = RET_D ** -0.5
    sls = [slice(h * RET_D, (h + 1) * RET_D) for h in range(H)]

    for d in range(2):
        for h in range(H):
            if has_h0:
                st_scr[d, h] = h0_ref[d, h]
            else:
                st_scr[d, h] = jnp.zeros((RET_D, RET_D), f32)
    for h in range(H):
        laf = ld_ref[h]
        lab = ld_ref[H + h]
        dec_scr[h] = jnp.where(col < row, jnp.exp(diff * laf),
                               jnp.where(col > row, jnp.exp(-diff * lab), 2.0))


    def intra(c, carry):
        r0 = pl.multiple_of(c * CHUNK, CHUNK)
        qb = [q_ref[pl.ds(r0, CHUNK), sls[h]].astype(bf16) for h in range(H)]
        kb = [(k_ref[pl.ds(r0, CHUNK), sls[h]] * kscale).astype(bf16) for h in range(H)]
        vh = [v_ref[pl.ds(r0, CHUNK), sls[h]] for h in range(H)]
        s = [_nt(qb[h], kb[h]) for h in range(H)]
        cf = [_tn((vh[h] * jnp.exp((CHUNK - 1.0 - rowf) * ld_ref[h])).astype(bf16), kb[h])
              for h in range(H)]
        cb = [_tn((vh[h] * jnp.exp(rowf * ld_ref[H + h])).astype(bf16), kb[h]) for h in range(H)]
        m = [(s[h] * dec_scr[h]).astype(bf16) for h in range(H)]
        y = [_nn(m[h], vh[h].astype(bf16)) for h in range(H)]
        for h in range(H):
            y_scr[pl.ds(r0, CHUNK), sls[h]] = y[h]
            cs_scr[0, c, h] = cf[h]
            cs_scr[1, c, h] = cb[h]
        return carry

    lax.fori_loop(0, nc, intra, 0, unroll=2)

    def inter(d, c):
        r0 = pl.multiple_of(c * CHUNK, CHUNK)
        st = [st_scr[d, h] for h in range(H)]
        yi = [_nt(q_ref[pl.ds(r0, CHUNK), sls[h]].astype(bf16), st[h].astype(bf16)) for h in range(H)]
        for h in range(H):
            la = ld_ref[d * H + h]
            st_scr[d, h] = st[h] * jnp.exp(jnp.full((RET_D, RET_D), CHUNK * la, f32)) + cs_scr[d, c, h]
        return r0, yi

    def fwd(c, carry):
        r0, yi = inter(0, c)
        for h in range(H):
            y_scr[pl.ds(r0, CHUNK), sls[h]] += jnp.exp((rowf + 1.0) * ld_ref[h]) * yi[h]
        return carry

    lax.fori_loop(0, nc, fwd, 0, unroll=2)

    def bwd(i, carry):
        r0, yi = inter(1, nc - 1 - i)
        for h in range(H):
            sl = sls[h]
            y = y_scr[pl.ds(r0, CHUNK), sl] + jnp.exp((CHUNK - rowf) * ld_ref[H + h]) * yi[h]
            mu = jnp.mean(y, axis=-1, keepdims=True)
            yc = y - mu
            var = jnp.mean(yc * yc, axis=-1, keepdims=True)
            yn = yc * lax.rsqrt(var + EPS) * gn_ref[:, sl]
            g = g_ref[pl.ds(r0, CHUNK), sl]
            o_ref[pl.ds(r0, CHUNK), sl] = (_silu(g) * yn).astype(o_ref.dtype)
        return carry

    lax.fori_loop(0, nc, bwd, 0, unroll=2)

    if want_state:
        for l in range(n_prev):
            hout_ref[l] = prev_ref[l]
        for d in range(2):
            for h in range(RET_HEADS):
                hout_ref[n_prev, d, h] = st_scr[d, h]


def _retention(u, log_decay, gn_g, h0, layer, nb, L, want_state, prev=None):
    has_h0 = h0 is not None
    n_prev = 0 if prev is None else prev.shape[1]
    sshape = (2, RET_HEADS, RET_D, RET_D)
    cblk = lambda c: pl.BlockSpec((L, GW), lambda b, c=c: (b, c // GW))
    in_specs = [pl.BlockSpec(memory_space=pltpu.SMEM),
                cblk(C_RQ), cblk(C_RK), cblk(C_RV), cblk(C_RG),
                pl.BlockSpec((1, GW), lambda b: (0, 0))]
    args = [log_decay.reshape(2 * RET_HEADS), u, u, u, u, gn_g.reshape(1, GW)]
    if has_h0:
        in_specs.append(pl.BlockSpec((None, None, 2, RET_HEADS, RET_D, RET_D),
                                     lambda b: (b, layer, 0, 0, 0, 0)))
        args.append(h0)
    if n_prev:
        in_specs.append(pl.BlockSpec((None, n_prev) + sshape, lambda b: (b, 0, 0, 0, 0, 0)))
        args.append(prev)
    out_specs = [pl.BlockSpec((L, GW), lambda b: (b, 0))]
    out_shape = [jax.ShapeDtypeStruct((nb * L, GW), bf16)]
    if want_state:
        out_specs.append(pl.BlockSpec((None, n_prev + 1) + sshape, lambda b: (b, 0, 0, 0, 0, 0)))
        out_shape.append(jax.ShapeDtypeStruct((nb, n_prev + 1) + sshape, f32))
    res = pl.pallas_call(
        functools.partial(_ret_kernel, L=L, has_h0=has_h0, want_state=want_state, n_prev=n_prev),
        grid=(nb,),
        in_specs=in_specs,
        out_specs=out_specs,
        out_shape=out_shape,
        scratch_shapes=[pltpu.VMEM((L, GW), f32),
                        pltpu.VMEM((2, RET_HEADS, RET_D, RET_D), f32),
                        pltpu.VMEM((RET_HEADS, CHUNK, CHUNK), f32),
                        pltpu.VMEM((2, L // CHUNK, RET_HEADS, RET_D, RET_D), f32)],
        compiler_params=_cparams(("arbitrary",)),
        name="retention",
    )(*args)
    return res if want_state else (res[0], None)


def _softplus(x):
    return jnp.maximum(x, 0.0) + jnp.log(1.0 + jnp.exp(-jnp.abs(x)))


def _ssm_kernel(*refs, L, has_h0, want_state, n_prev):
    (xbc_ref, z_ref, dtc_ref, dtr_ref, cw_ref, cb_ref, alr_ref, alc_ref, dbr_ref, dbc_ref,
     dsk_ref, ng_ref) = refs[:12]
    pos = 12
    h0_ref = None
    if has_h0:
        h0_ref = refs[pos]
        pos += 1
    prev_ref = None
    if n_prev:
        prev_ref = refs[pos]
        pos += 1
    o_ref = refs[pos]
    pos += 1
    hout_ref = None
    if want_state:
        hout_ref = refs[pos]
        pos += 1
    xpad_ref, xc_scr, y_scr, st_scr, es_scr, cs_scr, tot_scr = refs[pos:pos + 7]

    nc = L // CHUNK
    H, P, N = SSM_HEADS, SSM_P, SSM_N
    HG = H // SSM_GROUPS
    pad = 8
    row = lax.broadcasted_iota(jnp.int32, (CHUNK, CHUNK), 0)
    col = lax.broadcasted_iota(jnp.int32, (CHUNK, CHUNK), 1)
    lt01 = (row >= col).astype(bf16)
    ut01 = (row <= col).astype(bf16)
    sub16 = lax.broadcasted_iota(jnp.int32, (2 * H, CHUNK), 0)
    lane_lo = col < P

    xpad_ref[0:pad, :] = jnp.zeros((pad, 2 * GW), f32)
    xpad_ref[pad + L:2 * pad + L, :] = jnp.zeros((pad, 2 * GW), f32)

    def fill(i, carry):
        r0 = pl.multiple_of(i * CHUNK, CHUNK)
        xpad_ref[pl.ds(pad + r0, CHUNK), :] = xbc_ref[pl.ds(r0, CHUNK), :]
        return carry

    lax.fori_loop(0, nc, fill, 0)

    for d in range(2):
        for h in range(H):
            g, hh = divmod(h, HG)
            if has_h0:
                st_scr[d, g, hh * P:(hh + 1) * P, :] = h0_ref[d, h]
            else:
                st_scr[d, g, hh * P:(hh + 1) * P, :] = jnp.zeros((P, N), f32)

    a_neg_r = -jnp.exp(alr_ref[...])
    a_neg_c = -jnp.exp(alc_ref[...])

    def per_head_lanes(v, base):
        cols = []
        for c2 in range(H // 2):
            a = jnp.broadcast_to(v[:, base + 2 * c2:base + 2 * c2 + 1], (CHUNK, 2 * P))
            b = jnp.broadcast_to(v[:, base + 2 * c2 + 1:base + 2 * c2 + 2], (CHUNK, 2 * P))
            cols.append(jnp.where(lane_lo, a, b))
        return jnp.concatenate(cols, axis=1)

    def grp(x, g, base):
        return x[:, base + g * N:base + (g + 1) * N]


    def intra(c, carry):
        r0 = pl.multiple_of(c * CHUNK, CHUNK)
        win = xpad_ref.at[pl.ds(r0, CHUNK + 2 * pad), :]
        off = pad - SSM_K // 2
        for c0 in range(0, 2 * GW, 128):
            ls = slice(c0, c0 + 128)
            acc = jnp.broadcast_to(cb_ref[:, ls], (CHUNK, 128))
            for k in range(SSM_K):
                acc = acc + win[off + k:off + k + CHUNK, ls] * cw_ref[k:k + 1, ls]
            xc_scr[pl.ds(r0, CHUNK), ls] = _silu(acc)
        xc = xc_scr[pl.ds(r0, CHUNK), :]
        xs = xc[:, 0:GW]
        bmb = [grp(xc, g, GW).astype(bf16) for g in range(SSM_GROUPS)]
        cmb = [grp(xc, g, GW + SSM_GROUPS * N).astype(bf16) for g in range(SSM_GROUPS)]
        gmat = [_nt(cmb[g], bmb[g]) for g in range(SSM_GROUPS)]

        dt_c = _softplus(dtc_ref[pl.ds(r0, CHUNK), :] + dbr_ref[...])
        dt_r = _softplus(dtr_ref[:, pl.ds(r0, CHUNK)] + dbc_ref[...])
        lc3 = _split3(dt_c * a_neg_r)
        lr3 = _split3(dt_r * a_neg_c)
        pc = [_nn(lt01, t) for t in lc3]
        sc = [_nn(ut01, t) for t in lc3]
        pr = [_nn(t, ut01) for t in lr3]
        sr = [_nn(t, lt01) for t in lr3]
        a_c = jnp.where(col < H, pc[0] + pc[1] + pc[2], sc[0] + sc[1] + sc[2])
        a_r = jnp.where(sub16 < H, pr[0] + pr[1] + pr[2], sr[0] + sr[1] + sr[2])
        tot = jnp.where(col[0:1, :] < H, a_c[CHUNK - 1:CHUNK, :], a_c[0:1, :])
        tot_scr[c] = jnp.broadcast_to(tot, (8, 128))
        ks = dt_c * jnp.exp(tot - a_c)
        es = jnp.exp(a_c)
        es_scr[0, pl.ds(r0, CHUNK), :] = per_head_lanes(es, 0)
        es_scr[1, pl.ds(r0, CHUNK), :] = per_head_lanes(es, H)
        xwf = (xs * per_head_lanes(ks, 0)).astype(bf16)
        xwb = (xs * per_head_lanes(ks, H)).astype(bf16)
        csf = [_tn(xwf[:, g * HG * P:(g + 1) * HG * P], bmb[g]) for g in range(SSM_GROUPS)]
        csb = [_tn(xwb[:, g * HG * P:(g + 1) * HG * P], bmb[g]) for g in range(SSM_GROUPS)]
        for g in range(SSM_GROUPS):
            cs_scr[0, c, g] = csf[g]
            cs_scr[1, c, g] = csb[g]

        ms = []
        for h in range(H):
            df = jnp.exp(jnp.minimum(a_c[:, h:h + 1] - a_r[h:h + 1, :], 0.0)) * dt_r[h:h + 1, :]
            db = jnp.exp(jnp.minimum(a_c[:, H + h:H + h + 1] - a_r[H + h:H + h + 1, :], 0.0)) \
                * dt_r[H + h:H + h + 1, :]
            dm = jnp.where(col <= row, df, 0.0) + jnp.where(col >= row, db, 0.0)
            ms.append((gmat[h // HG] * dm).astype(bf16))
        ys = []
        for c2 in range(H // 2):
            xcol = xs[:, c2 * 2 * P:(c2 + 1) * 2 * P]
            x_lo = jnp.where(lane_lo, xcol, 0.0).astype(bf16)
            x_hi = jnp.where(lane_lo, 0.0, xcol).astype(bf16)
            ys.append(_nn(ms[2 * c2], x_lo) + _nn(ms[2 * c2 + 1], x_hi))
        y = jnp.concatenate(ys, axis=1)
        y_scr[pl.ds(r0, CHUNK), :] = y + (dsk_ref[0:1, :] + dsk_ref[1:2, :]) * xs
        return carry

    lax.fori_loop(0, nc, intra, 0, unroll=2)

    def inter(d, c):
        r0 = pl.multiple_of(c * CHUNK, CHUNK)
        st = [st_scr[d, g] for g in range(SSM_GROUPS)]
        yi = [_nt(grp(xc_scr[pl.ds(r0, CHUNK), :], g, GW + SSM_GROUPS * N).astype(bf16),
                  st[g].astype(bf16)) for g in range(SSM_GROUPS)]
        tot = tot_scr[c]
        for g in range(SSM_GROUPS):
            dec = jnp.concatenate(
                [jnp.broadcast_to(jnp.exp(tot[0:1, d * H + g * HG + hh:d * H + g * HG + hh + 1]), (P, N))
                 for hh in range(HG)], axis=0)
            st_scr[d, g] = st[g] * dec + cs_scr[d, c, g]
        return r0, jnp.concatenate(yi, axis=1) * es_scr[d, pl.ds(r0, CHUNK), :]

    def fwd(c, carry):
        r0, yi = inter(0, c)
        y_scr[pl.ds(r0, CHUNK), :] += yi
        return carry

    lax.fori_loop(0, nc, fwd, 0, unroll=2)

    def bwd(i, carry):
        r0, yi = inter(1, nc - 1 - i)
        yz = (y_scr[pl.ds(r0, CHUNK), :] + yi) * _silu(z_ref[pl.ds(r0, CHUNK), :])
        ms = jnp.mean(yz * yz, axis=-1, keepdims=True)
        o_ref[pl.ds(r0, CHUNK), :] = (yz * lax.rsqrt(ms + EPS) * ng_ref[...]).astype(o_ref.dtype)
        return carry

    lax.fori_loop(0, nc, bwd, 0, unroll=2)

    if want_state:
        for l in range(n_prev):
            hout_ref[l] = prev_ref[l]
        for d in range(2):
            for h in range(H):
                g, hh = divmod(h, HG)
                hout_ref[n_prev, d, h] = st_scr[d, g, hh * P:(hh + 1) * P, :]


def _pad_lanes(v, n=128):
    return jnp.pad(v, ((0, 0), (0, n - v.shape[1])))


def _ssm(u, dt_t, p, h0, layer, nb, L, want_state, prev=None):
    has_h0 = h0 is not None
    n_prev = 0 if prev is None else prev.shape[1]
    sshape = (2, SSM_HEADS, SSM_P, SSM_N)
    H = SSM_HEADS
    a_log = p['ssm_a_log'].reshape(1, 2 * H)
    dt_bias = p['ssm_dt_bias'].reshape(1, 2 * H)
    small = lambda shape: pl.BlockSpec(shape, lambda b: (0, 0))
    in_specs = [pl.BlockSpec((L, 2 * GW), lambda b: (b, C_XBC // (2 * GW))),
                pl.BlockSpec((L, GW), lambda b: (b, C_Z // GW)),
                pl.BlockSpec((L, 128), lambda b: (b, C_DT // 128)),
                pl.BlockSpec((2 * H, L), lambda b: (0, b)),
                small((SSM_K, 2 * GW)), small((1, 2 * GW)),
                small((1, 128)), small((2 * H, 128)), small((1, 128)), small((2 * H, 128)),
                small((2, GW)), small((1, GW))]
    args = [u, u, u, dt_t, p['ssm_conv_w'], p['ssm_conv_b'].reshape(1, 2 * GW),
            _pad_lanes(a_log), jnp.broadcast_to(a_log.reshape(2 * H, 1), (2 * H, 128)),
            _pad_lanes(dt_bias), jnp.broadcast_to(dt_bias.reshape(2 * H, 1), (2 * H, 128)),
            jnp.repeat(p['ssm_d'], SSM_P, axis=1), p['ssm_norm'].reshape(1, GW)]
    if has_h0:
        in_specs.append(pl.BlockSpec((None, None, 2, H, SSM_P, SSM_N), lambda b: (b, layer, 0, 0, 0, 0)))
        args.append(h0)
    out_specs = [pl.BlockSpec((L, GW), lambda b: (b, 0))]
    out_shape = [jax.ShapeDtypeStruct((nb * L, GW), bf16)]
    if n_prev:
        in_specs.append(pl.BlockSpec((None, n_prev) + sshape, lambda b: (b, 0, 0, 0, 0, 0)))
        args.append(prev)
    if want_state:
        out_specs.append(pl.BlockSpec((None, n_prev + 1) + sshape, lambda b: (b, 0, 0, 0, 0, 0)))
        out_shape.append(jax.ShapeDtypeStruct((nb, n_prev + 1) + sshape, f32))
    res = pl.pallas_call(
        functools.partial(_ssm_kernel, L=L, has_h0=has_h0, want_state=want_state, n_prev=n_prev),
        grid=(nb,),
        in_specs=in_specs,
        out_specs=out_specs,
        out_shape=out_shape,
        scratch_shapes=[pltpu.VMEM((L + 16, 2 * GW), f32),
                        pltpu.VMEM((L, 2 * GW), f32),
                        pltpu.VMEM((L, GW), f32),
                        pltpu.VMEM((2, SSM_GROUPS, H // SSM_GROUPS * SSM_P, SSM_N), f32),
                        pltpu.VMEM((2, L, GW), f32),
                        pltpu.VMEM((2, L // CHUNK, SSM_GROUPS, H // SSM_GROUPS * SSM_P, SSM_N), f32),
                        pltpu.VMEM((L // CHUNK, 8, 128), f32)],
        compiler_params=_cparams(("arbitrary",)),
        name="ssd_mixer",
    )(*args)
    return res if want_state else (res[0], None)


def _ctx_att_kernel(sink_ref, q_ref, k_ref, v_ref, o_ref, *, L):
    G = ATT_HEADS // ATT_KV
    scale = ATT_D ** -0.5
    kb = [k_ref[:, j * ATT_D:(j + 1) * ATT_D].astype(bf16) for j in range(ATT_KV)]
    vb = [v_ref[:, j * ATT_D:(j + 1) * ATT_D].astype(bf16) for j in range(ATT_KV)]
    s = [_nt((q_ref[:, h * ATT_D:(h + 1) * ATT_D] * scale).astype(bf16), kb[h // G])
         for h in range(ATT_HEADS)]
    p, den = [], []
    for h in range(ATT_HEADS):
        sink = sink_ref[h]
        m = jnp.maximum(jnp.max(s[h], axis=-1, keepdims=True), sink)
        e = jnp.exp(s[h] - m)
        den.append(jnp.sum(e, axis=-1, keepdims=True) + jnp.exp(sink - m))
        p.append(e.astype(bf16))
    o = [_nn(p[h], vb[h // G]) for h in range(ATT_HEADS)]
    for h in range(ATT_HEADS):
        o_ref[:, h * ATT_D:(h + 1) * ATT_D] = (o[h] / den[h]).astype(o_ref.dtype)


def _ctx_attention(u, sink, nb, L):
    return pl.pallas_call(
        functools.partial(_ctx_att_kernel, L=L),
        grid=(nb,),
        in_specs=[pl.BlockSpec(memory_space=pltpu.SMEM),
                  pl.BlockSpec((L, GW), lambda b: (b, C_AQ // GW)),
                  pl.BlockSpec((L, 128), lambda b: (b, C_AK // 128)),
                  pl.BlockSpec((L, 128), lambda b: (b, C_AV // 128))],
        out_specs=pl.BlockSpec((L, GW), lambda b: (b, 0)),
        out_shape=jax.ShapeDtypeStruct((nb * L, GW), bf16),
        compiler_params=_cparams(("arbitrary",)),
        name="ctx_attention",
    )(sink, u, u, u)


def _rope(x, cos, sin):
    w = x.shape[1]
    lane = lax.broadcasted_iota(jnp.int32, x.shape, 1)
    first = (lane % 32) < 16
    rot = jnp.where(first, -pltpu.roll(x, w - 16, 1), pltpu.roll(x, 16, 1))
    return x * cos + rot * sin


def _half_variants(x, keep_fill):
    lane = lax.broadcasted_iota(jnp.int32, x.shape, 1)
    lo = lane < ATT_D
    a0 = jnp.where(lo, x, keep_fill)
    b1 = jnp.where(lo, keep_fill, x)
    xr = pltpu.roll(x, ATT_D, 1)
    a1 = jnp.where(lo, xr, keep_fill)
    b0 = jnp.where(lo, keep_fill, xr)
    return ((a0, b0), (a1, b1))


def _half_variants_t(xt):
    sub = lax.broadcasted_iota(jnp.int32, xt.shape, 0)
    lo = sub < ATT_D
    a0 = jnp.where(lo, xt, 0.0)
    b1 = jnp.where(lo, 0.0, xt)
    xr = pltpu.roll(xt, ATT_D, 0)
    a1 = jnp.where(lo, xr, 0.0)
    b0 = jnp.where(lo, 0.0, xr)
    return ((a0, b0), (a1, b1))


def _lat_att_kernel(sink_ref, q_ref, k_ref, v_ref, kc_ref, vc_ref, cos_ref, sin_ref, o_ref,
                    qr_scr, kt_scr, vv_scr, kct_scr, vcv_scr, bias_scr, *, L):
    G = ATT_HEADS // ATT_KV
    B = CHUNK
    nb = L // B
    scale = ATT_D ** -0.5
    zeros = jnp.zeros((B, 128), bf16)
    for j in range(ATT_KV):
        for hf in range(2):
            kt_scr[j, hf, 0] = zeros
            kt_scr[j, hf, nb + 1] = zeros
            vv_scr[j, hf, 0:B, :] = zeros
            vv_scr[j, hf, B + L:2 * B + L, :] = zeros
    kct = _half_variants_t(kc_ref[...].T)
    vcv = _half_variants(vc_ref[...], 1.0)
    for j in range(ATT_KV):
        for hf in range(2):
            kct_scr[j, hf] = kct[j][hf].astype(bf16)
            vcv_scr[j, hf] = vcv[j][hf].astype(bf16)

    rowi = lax.broadcasted_iota(jnp.int32, (2 * B, 3 * B), 0) % B
    coli = lax.broadcasted_iota(jnp.int32, (2 * B, 3 * B), 1)
    inwin = jnp.abs(rowi - (coli - B)) <= ATT_WIN
    ninf = jnp.float32(-jnp.inf)
    bias_scr[0] = jnp.where(inwin & (coli >= B), 0.0, ninf)
    bias_scr[1] = jnp.where(inwin, 0.0, ninf)
    bias_scr[2] = jnp.where(inwin & (coli < 2 * B), 0.0, ninf)

    def prep(n, carry):
        r0 = pl.multiple_of(n * B, B)
        cos = cos_ref[pl.ds(r0, B), :]
        sin = sin_ref[pl.ds(r0, B), :]
        cos4 = jnp.concatenate([cos] * 4, axis=1)
        sin4 = jnp.concatenate([sin] * 4, axis=1)
        qr_scr[pl.ds(r0, B), :] = (_rope(q_ref[pl.ds(r0, B), :], cos4, sin4) * scale).astype(bf16)
        kvar = _half_variants_t(_rope(k_ref[pl.ds(r0, B), :], cos, sin).T)
        vvar = _half_variants(v_ref[pl.ds(r0, B), :], 1.0)
        for j in range(ATT_KV):
            for hf in range(2):
                kt_scr[j, hf, n + 1] = kvar[j][hf].astype(bf16)
                vv_scr[j, hf, pl.ds(B + r0, B), :] = vvar[j][hf].astype(bf16)
        return carry

    lax.fori_loop(0, nb, prep, 0)

    lane_lo = lax.broadcasted_iota(jnp.int32, (2 * B, 128), 1) < ATT_D

    combos = [(j, hf) for j in range(ATT_KV) for hf in range(2)]
    BLK_GROUP = 8

    def scores(n):
        r0 = pl.multiple_of(n * B, B)
        qs = [jnp.concatenate([qr_scr[pl.ds(r0, B), (2 * j) * 128:(2 * j + 1) * 128],
                               qr_scr[pl.ds(r0, B), (2 * j + 1) * 128:(2 * j + 2) * 128]], axis=0)
              for j in range(ATT_KV)]
        s_c = [_nn(qs[j], kct_scr[j, hf]) for j, hf in combos]
        s_b = [jnp.concatenate([_nn(qs[j], kt_scr[j, hf, n + t]) for t in range(3)], axis=1)
               for j, hf in combos]
        return s_c, s_b

    def finish(n, s_c, s_b):
        r0 = pl.multiple_of(n * B, B)
        bias = bias_scr[jnp.where(n == 0, 0, jnp.where(n == nb - 1, 2, 1))]
        p_c, p_b, esink = [], [], []
        for i, (j, hf) in enumerate(combos):
            sb = s_b[i] + bias
            sink = jnp.concatenate([jnp.full((B, 1), sink_ref[G * j + hf], f32),
                                    jnp.full((B, 1), sink_ref[G * j + 2 + hf], f32)], axis=0)
            m = jnp.maximum(jnp.maximum(jnp.max(s_c[i], axis=-1, keepdims=True),
                                        jnp.max(sb, axis=-1, keepdims=True)), sink)
            p_c.append(jnp.exp(s_c[i] - m).astype(bf16))
            p_b.append(jnp.exp(sb - m).astype(bf16))
            esink.append(jnp.exp(sink - m))
        oe = [_nn(p_c[i], vcv_scr[j, hf]) + _nn(p_b[i], vv_scr[j, hf, pl.ds(r0, 3 * B), :])
              for i, (j, hf) in enumerate(combos)]
        outs = []
        for i, (j, hf) in enumerate(combos):
            den = oe[i][:, (1 - hf) * ATT_D:(1 - hf) * ATT_D + 1] + esink[i]
            outs.append(oe[i] / den)
        for j in range(ATT_KV):
            o = jnp.where(lane_lo, outs[2 * j], outs[2 * j + 1])
            o_ref[pl.ds(r0, B), (2 * j) * 128:(2 * j + 1) * 128] = o[0:B].astype(o_ref.dtype)
            o_ref[pl.ds(r0, B), (2 * j + 1) * 128:(2 * j + 2) * 128] = o[B:2 * B].astype(o_ref.dtype)

    def blk_group(i, carry):
        n0 = i * BLK_GROUP
        s_next = scores(n0)
        for t in range(BLK_GROUP):
            s_cur = s_next
            if t + 1 < BLK_GROUP:
                s_next = scores(n0 + t + 1)
            finish(n0 + t, *s_cur)
        return carry

    lax.fori_loop(0, nb // BLK_GROUP, blk_group, 0)


def _rope_tables(L):
    pos = jnp.arange(L)
    rows = (pos // GRID_W).astype(f32)
    cols = (pos % GRID_W).astype(f32)
    half = ATT_D // 4
    freqs = ROPE_THETA ** (-jnp.arange(half, dtype=f32) / half)
    ang_r = rows[:, None] * freqs
    ang_c = cols[:, None] * freqs
    ang = jnp.concatenate([ang_r, ang_r, ang_c, ang_c], axis=1)
    ang = jnp.concatenate([ang, ang], axis=1)
    return jnp.cos(ang), jnp.sin(ang)


def _lat_attention(u, k_cache, v_cache, sink, layer, nb, L):
    Lc = k_cache.shape[2]
    cos, sin = _rope_tables(L)
    kc = k_cache.reshape(nb, DEPTH, Lc, ATT_KV * ATT_D)
    vc = v_cache.reshape(nb, DEPTH, Lc, ATT_KV * ATT_D)
    cspec = pl.BlockSpec((None, None, Lc, 128), lambda b: (b, layer, 0, 0))
    return pl.pallas_call(
        functools.partial(_lat_att_kernel, L=L),
        grid=(nb,),
        in_specs=[pl.BlockSpec(memory_space=pltpu.SMEM),
                  pl.BlockSpec((L, GW), lambda b: (b, C_AQ // GW)),
                  pl.BlockSpec((L, 128), lambda b: (b, C_AK // 128)),
                  pl.BlockSpec((L, 128), lambda b: (b, C_AV // 128)),
                  cspec, cspec,
                  pl.BlockSpec((L, 128), lambda b: (0, 0)),
                  pl.BlockSpec((L, 128), lambda b: (0, 0))],
        out_specs=pl.BlockSpec((L, GW), lambda b: (b, 0)),
        out_shape=jax.ShapeDtypeStruct((nb * L, GW), bf16),
        scratch_shapes=[pltpu.VMEM((L, GW), bf16),
                        pltpu.VMEM((ATT_KV, 2, L // CHUNK + 2, 128, CHUNK), bf16),
                        pltpu.VMEM((ATT_KV, 2, L + 2 * CHUNK, 128), bf16),
                        pltpu.VMEM((ATT_KV, 2, 128, Lc), bf16),
                        pltpu.VMEM((ATT_KV, 2, Lc, 128), bf16),
                        pltpu.VMEM((3, 2 * CHUNK, 3 * CHUNK), f32)],
        compiler_params=_cparams(("arbitrary",)),
        name="lat_attention",
    )(sink, u, u, u, kc, vc, cos, sin)


def _permute_w_in_kernel(wt_ref, o_ref):
    ndt = 2 * SSM_HEADS
    moves = [(0, C_CONV, 1024), (1536, C_XBC, 1024), (1024, C_Z, GW), (2560 + ndt, C_RQ, C_DT - C_RQ)]
    for src, dst, n in moves:
        for r in range(0, n, 128):
            o_ref[:, dst + r:dst + r + 128] = wt_ref[src + r:src + r + 128, :].T.astype(bf16)
    tail = wt_ref[2560:2560 + 128, :].T
    lane = lax.broadcasted_iota(jnp.int32, tail.shape, 1)
    o_ref[:, C_DT:UW] = jnp.where(lane < ndt, tail, 0.0).astype(bf16)


def _permute_w_in(w):
    tr = 256
    return pl.pallas_call(
        _permute_w_in_kernel,
        grid=(DEPTH, D // tr),
        in_specs=[pl.BlockSpec((None, N_IN, tr), lambda l, i: (l, 0, i))],
        out_specs=pl.BlockSpec((None, tr, UW), lambda l, i: (l, i, 0)),
        out_shape=jax.ShapeDtypeStruct((DEPTH, D, UW), bf16),
        compiler_params=_cparams(("arbitrary", "arbitrary")),
        name="permute_w_in",
    )(jnp.swapaxes(w, 1, 2))


def _mix_and_mlp(x, u, dt_t, mod, p, ctx, layer, nb, L, final, prev_ssm=None, prev_ret=None):
    per_seq = ctx is not None
    want_state = ctx is None
    o_conv = _conv_module(u, p['conv_w'], p['conv_b'], p['conv_ln_g'], p['conv_ln_b'], nb, L)
    if ctx is None:
        o_ssm, h_ssm = _ssm(u, dt_t, p, None, layer, nb, L, True, prev_ssm)
        o_ret, h_ret = _retention(u, p['ret_log_decay'], p['ret_gn_g'], None, layer, nb, L, True, prev_ret)
        o_att = _ctx_attention(u, p['att_sink'], nb, L)
    else:
        k_c, v_c, s_ssm, s_ret = ctx
        o_ssm, h_ssm = _ssm(u, dt_t, p, s_ssm, layer, nb, L, False)
        o_ret, h_ret = _retention(u, p['ret_log_decay'], p['ret_gn_g'], s_ret, layer, nb, L, False)
        o_att = _lat_attention(u, k_c, v_c, p['att_sink'], layer, nb, L)
    x1 = _outproj(o_conv, o_ssm, o_ret, o_att, p['w_out_b'], x, mod, L, per_seq)
    x2 = _mlp(x1, p['norm_mlp'], mod, p['final_norm'], p['w1_b'], p['w2_b'], L, per_seq, final)
    states = None
    if want_state:
        ak = u[:, C_AK:C_AK + 128].reshape(nb, L, ATT_KV, ATT_D)
        av = u[:, C_AV:C_AV + 128].reshape(nb, L, ATT_KV, ATT_D)
        states = (ak, av, h_ssm, h_ret)
    return x2, states


def kernel(x_prompt, x_sample, cache_attn_k, cache_attn_v, state_ssm, state_ret, c, c_ctx, ada_w, ada_b, norm_mix, norm_mlp, w_in, conv_w, conv_b, conv_ln_g, conv_ln_b, ssm_conv_w, ssm_conv_b, ssm_a_log, ssm_dt_bias, ssm_d, ssm_norm, ret_log_decay, ret_gn_g, att_sink, w_out, w1, w2, final_norm):
    nbp, Lp, _ = x_prompt.shape
    nbs, Ls, _ = x_sample.shape
    cvec = jnp.concatenate([c_ctx[None, :], c, jnp.zeros((16 - 1 - nbs, D), f32)], axis=0)
    mod = _ada_mod(cvec, ada_w, ada_b)
    y_p = x_prompt.reshape(nbp * Lp, D)
    y_s = x_sample.reshape(nbs * Ls, D)
    new_k, new_v = [], []
    hs, hr = None, None
    w_in_p = _permute_w_in(w_in)
    ctx = (cache_attn_k, cache_attn_v, state_ssm, state_ret)
    for l in range(DEPTH):
        mod_l = mod[l].reshape(16, 6, 1, D)
        mod_p, mod_s = mod_l[0:1], mod_l[1:1 + nbs]
        u_p, dtt_p, (w_out_b,) = _inproj(y_p, norm_mix[l], mod_p, w_in_p, l, Lp, False, [w_out])
        u_s, dtt_s, (w1_b, w2_b) = _inproj(y_s, norm_mix[l], mod_s, w_in_p, l, Ls, True, [w1, w2])
        p = dict(norm_mlp=norm_mlp[l],
                 conv_w=conv_w[l], conv_b=conv_b[l], conv_ln_g=conv_ln_g[l], conv_ln_b=conv_ln_b[l],
                 ssm_conv_w=ssm_conv_w[l], ssm_conv_b=ssm_conv_b[l], ssm_a_log=ssm_a_log[l],
                 ssm_dt_bias=ssm_dt_bias[l], ssm_d=ssm_d[l], ssm_norm=ssm_norm[l],
                 ret_log_decay=ret_log_decay[l], ret_gn_g=ret_gn_g[l], att_sink=att_sink[l],
                 w_out_b=w_out_b, w1_b=w1_b, w2_b=w2_b, final_norm=final_norm)
        final = l == DEPTH - 1
        y_p, (k_l, v_l, hs, hr) = _mix_and_mlp(y_p, u_p, dtt_p, mod_p, p, None, l, nbp, Lp, final, hs, hr)
        new_k.append(k_l)
        new_v.append(v_l)
        y_s, _ = _mix_and_mlp(y_s, u_s, dtt_s, mod_s, p, ctx, l, nbs, Ls, final)
    return (y_p.reshape(nbp, Lp, D), y_s.reshape(nbs, Ls, D),
            jnp.stack(new_k, axis=1), jnp.stack(new_v, axis=1), hs, hr)
```

```python
import functools

import jax
import jax.numpy as jnp
from jax import lax
from jax.experimental import pallas as pl
from jax.experimental.pallas import tpu as pltpu

f32 = jnp.float32
bf16 = jnp.bfloat16

D = 2048
DEPTH = 2
GW = 512
CONV_K = 31
SSM_HEADS, SSM_P, SSM_N, SSM_GROUPS, SSM_K = 8, 64, 128, 2, 5
RET_HEADS, RET_D = 4, 128
ATT_HEADS, ATT_KV, ATT_D, ATT_WIN = 8, 2, 64, 128
GRID_W = 64
ROPE_THETA = 10000.0
CHUNK = 128
D_FF = 4 * D
EPS = 1e-6

C_CONV, C_XBC, C_Z, C_RQ, C_RK, C_RV, C_RG, C_AQ, C_AK, C_AV, C_DT = (
    0, 1024, 2048, 2560, 3072, 3584, 4096, 4608, 5120, 5248, 5376)
UW = 5504
N_IN = 5392
N_CHUNK = 512

VMEM_LIMIT = 56 * 1024 * 1024


def _cparams(sem):
    return pltpu.CompilerParams(dimension_semantics=sem, vmem_limit_bytes=VMEM_LIMIT)


def _sigmoid(x):
    return 1.0 / (1.0 + jnp.exp(-x))


def _silu(x):
    return x * _sigmoid(x)


def _nt(a, b):
    return lax.dot_general(a, b, (((1,), (1,)), ((), ())), preferred_element_type=f32)


def _tn(a, b):
    return lax.dot_general(a, b, (((0,), (0,)), ((), ())), preferred_element_type=f32)


def _nn(a, b):
    return jnp.dot(a, b, preferred_element_type=f32)


def _split3(x):
    hi = x.astype(bf16)
    r1 = x - hi.astype(f32)
    mid = r1.astype(bf16)
    lo = (r1 - mid.astype(f32)).astype(bf16)
    return hi, mid, lo


def _ada_kernel(c_ref, w_ref, b_ref, o_ref):
    c = c_ref[...]
    s = _silu(c).astype(bf16)
    o_ref[...] = _nn(s, w_ref[...].astype(bf16)) + b_ref[...]


def _ada_mod(cvec, ada_w, ada_b):
    tn = 1024
    return pl.pallas_call(
        _ada_kernel,
        grid=(DEPTH, 6 * D // tn),
        in_specs=[pl.BlockSpec((16, D), lambda l, j: (0, 0)),
                  pl.BlockSpec((None, D, tn), lambda l, j: (l, 0, j)),
                  pl.BlockSpec((None, 1, tn), lambda l, j: (l, 0, j))],
        out_specs=pl.BlockSpec((None, 16, tn), lambda l, j: (l, 0, j)),
        out_shape=jax.ShapeDtypeStruct((DEPTH, 16, 6 * D), f32),
        compiler_params=_cparams(("arbitrary", "arbitrary")),
        name="ada_mod",
    )(cvec, ada_w, ada_b.reshape(DEPTH, 1, 6 * D))


def _mod_spec(idx, tm, L, per_seq):
    if per_seq:
        return pl.BlockSpec((None, None, 1, D), lambda i, *_: ((i * tm) // L, idx, 0, 0))
    return pl.BlockSpec((None, None, 1, D), lambda i, *_: (0, idx, 0, 0))


def _inproj_kernel(x_ref, g_ref, sc_ref, sh_ref, w_ref, *refs, n_cast):
    cast_in, u_ref, dtt_ref, cast_out = refs[:n_cast], refs[n_cast], refs[n_cast + 1], refs[n_cast + 2:]
    x = x_ref[...]
    ms = jnp.mean(x * x, axis=-1, keepdims=True)
    h = x * lax.rsqrt(ms + EPS) * (g_ref[...] * (1.0 + sc_ref[...])) + sh_ref[...]
    hb = h.astype(bf16)
    for n0 in range(0, UW, N_CHUNK):
        n1 = min(n0 + N_CHUNK, UW)
        u_ref[:, n0:n1] = _nn(hb, w_ref[:, n0:n1])
    dtt_ref[...] = u_ref[:, C_DT:C_DT + 128].T[0:2 * SSM_HEADS, :]
    for src, dst in zip(cast_in, cast_out):
        dst[...] = src[...].astype(bf16)


def _inproj(x, norm_g, mod, w_in_p, layer, L, per_seq, casts):
    T = x.shape[0]
    tm = 256
    n = T // tm
    cast_in_specs, cast_out_specs, cast_shapes = [], [], []
    for w in casts:
        _, R, C = w.shape
        cast_in_specs.append(pl.BlockSpec((None, R // n, C), lambda i: (layer, i, 0)))
        cast_out_specs.append(pl.BlockSpec((R // n, C), lambda i: (i, 0)))
        cast_shapes.append(jax.ShapeDtypeStruct((R, C), bf16))
    res = pl.pallas_call(
        functools.partial(_inproj_kernel, n_cast=len(casts)),
        grid=(n,),
        in_specs=[pl.BlockSpec((tm, D), lambda i: (i, 0)),
                  pl.BlockSpec((1, D), lambda i: (0, 0)),
                  _mod_spec(1, tm, L, per_seq),
                  _mod_spec(0, tm, L, per_seq),
                  pl.BlockSpec((None, D, UW), lambda i: (layer, 0, 0), pipeline_mode=pl.Buffered(1))]
        + cast_in_specs,
        out_specs=[pl.BlockSpec((tm, UW), lambda i: (i, 0)),
                   pl.BlockSpec((2 * SSM_HEADS, tm), lambda i: (0, i))] + cast_out_specs,
        out_shape=[jax.ShapeDtypeStruct((T, UW), f32),
                   jax.ShapeDtypeStruct((2 * SSM_HEADS, T), f32)] + cast_shapes,
        compiler_params=_cparams(("arbitrary",)),
        name="in_proj",
    )(x, norm_g.reshape(1, D), mod, mod, w_in_p, *casts)
    return res[0], res[1], res[2:]


def _outproj_kernel(oc_ref, os_ref, or_ref, oa_ref, w_ref, x_ref, g_ref, o_ref):
    for n0 in range(0, D, N_CHUNK):
        n1 = n0 + N_CHUNK
        acc = _nn(oc_ref[...], w_ref[0:GW, n0:n1])
        acc = acc + _nn(os_ref[...], w_ref[GW:2 * GW, n0:n1])
        acc = acc + _nn(or_ref[...], w_ref[2 * GW:3 * GW, n0:n1])
        acc = acc + _nn(oa_ref[...], w_ref[3 * GW:4 * GW, n0:n1])
        o_ref[:, n0:n1] = x_ref[:, n0:n1] + g_ref[:, n0:n1] * acc


def _outproj(o_conv, o_ssm, o_ret, o_att, w_out_b, x, mod, L, per_seq):
    T = x.shape[0]
    tm = 512
    ospec = pl.BlockSpec((tm, GW), lambda i: (i, 0))
    return pl.pallas_call(
        _outproj_kernel,
        grid=(T // tm,),
        in_specs=[ospec, ospec, ospec, ospec,
                  pl.BlockSpec((D, D), lambda i: (0, 0), pipeline_mode=pl.Buffered(1)),
                  pl.BlockSpec((tm, D), lambda i: (i, 0)),
                  _mod_spec(2, tm, L, per_seq)],
        out_specs=pl.BlockSpec((tm, D), lambda i: (i, 0)),
        out_shape=jax.ShapeDtypeStruct((T, D), f32),
        compiler_params=_cparams(("arbitrary",)),
        name="out_proj",
    )(o_conv, o_ssm, o_ret, o_att, w_out_b, x, mod)


def _mlp_kernel(x_ref, g_ref, sc_ref, sh_ref, g2_ref, fn_ref, w1_ref, w2_ref, o_ref, h_scr, *, final):
    f = pl.program_id(1)
    tm = x_ref.shape[0]
    RB = 256

    def ffn_tile(r):
        a = _nn(h_scr[r:r + 2 * RB, :], w1_ref[...])
        a = jnp.maximum(a, 0.0)
        a = (a * a).astype(bf16)
        return _nn(a, w2_ref[...])

    @pl.when(f == 0)
    def _():
        gs = g_ref[...] * (1.0 + sc_ref[...])
        for r in range(0, tm, RB):
            x = x_ref[r:r + RB, :]
            ms = jnp.mean(x * x, axis=-1, keepdims=True)
            h_scr[r:r + RB, :] = (x * lax.rsqrt(ms + EPS) * gs + sh_ref[...]).astype(bf16)
        for r in range(0, tm, 2 * RB):
            o_ref[r:r + 2 * RB, :] = ffn_tile(r)

    last = pl.num_programs(1) - 1

    @pl.when(jnp.logical_and(f > 0, f < last))
    def _():
        for r in range(0, tm, 2 * RB):
            o_ref[r:r + 2 * RB, :] += ffn_tile(r)

    @pl.when(f == last)
    def _():
        for r in range(0, tm, 2 * RB):
            acc = o_ref[r:r + 2 * RB, :] + ffn_tile(r)
            y = x_ref[r:r + 2 * RB, :] + g2_ref[...] * acc
            if final:
                ms = jnp.mean(y * y, axis=-1, keepdims=True)
                y = y * lax.rsqrt(ms + EPS) * fn_ref[...]
            o_ref[r:r + 2 * RB, :] = y


def _mlp(x, norm_g, mod, final_norm, w1_b, w2_b, L, per_seq, final):
    T = x.shape[0]
    tm, tf = 1024, 512
    return pl.pallas_call(
        functools.partial(_mlp_kernel, final=final),
        grid=(T // tm, D_FF // tf),
        in_specs=[pl.BlockSpec((tm, D), lambda i, f: (i, 0)),
                  pl.BlockSpec((1, D), lambda i, f: (0, 0)),
                  _mod_spec(4, tm, L, per_seq),
                  _mod_spec(3, tm, L, per_seq),
                  _mod_spec(5, tm, L, per_seq),
                  pl.BlockSpec((1, D), lambda i, f: (0, 0)),
                  pl.BlockSpec((D, tf), lambda i, f: (0, f)),
                  pl.BlockSpec((tf, D), lambda i, f: (f, 0))],
        out_specs=pl.BlockSpec((tm, D), lambda i, f: (i, 0)),
        out_shape=jax.ShapeDtypeStruct((T, D), f32),
        scratch_shapes=[pltpu.VMEM((tm, D), bf16)],
        compiler_params=_cparams(("arbitrary", "arbitrary")),
        name="mlp",
    )(x, norm_g.reshape(1, D), mod, mod, mod, final_norm.reshape(1, D), w1_b, w2_b)


def _conv_kernel(u_ref, w_ref, b_ref, lg_ref, lb_ref, o_ref, vpad_ref, ph_ref, *, L):
    RC = 64
    pad = 16
    SUB = 8
    off0 = pad - CONV_K // 2
    nph = RC + SUB * ((off0 + CONV_K - 1) // SUB)
    vpad_ref[0:pad, :] = jnp.zeros((pad, GW), f32)
    vpad_ref[pad + L:2 * pad + L, :] = jnp.zeros((pad, GW), f32)

    def fill(i, carry):
        r0 = pl.multiple_of(i * RC, RC)
        a = u_ref[pl.ds(r0, RC), 0:GW]
        g = u_ref[pl.ds(r0, RC), GW:2 * GW]
        vpad_ref[pl.ds(pad + r0, RC), :] = a * _sigmoid(g)
        return carry

    lax.fori_loop(0, L // RC, fill, 0)

    def body(i, carry):
        r0 = pl.multiple_of(i * RC, RC)
        win = vpad_ref.at[pl.ds(r0, RC + 2 * pad), :]
        accs = []
        for c0 in range(0, GW, 128):
            ls = slice(c0, c0 + 128)
            for ph in range(SUB):
                ph_ref[ph, :, ls] = win[ph:ph + nph, ls]
            acc = jnp.broadcast_to(b_ref[:, ls], (RC, 128))
            for k in range(CONV_K):
                a, ph = divmod(off0 + k, SUB)
                acc = acc + ph_ref[ph, SUB * a:SUB * a + RC, ls] * w_ref[k:k + 1, ls]
            accs.append(acc)
        acc = jnp.concatenate(accs, axis=1)
        mu = jnp.mean(acc, axis=-1, keepdims=True)
        xc = acc - mu
        var = jnp.mean(xc * xc, axis=-1, keepdims=True)
        y = xc * lax.rsqrt(var + EPS) * lg_ref[...] + lb_ref[...]
        o_ref[pl.ds(r0, RC), :] = _silu(y).astype(o_ref.dtype)
        return carry

    lax.fori_loop(0, L // RC, body, 0, unroll=2)


def _conv_module(u, conv_w, conv_b, ln_g, ln_b, nb, L):
    return pl.pallas_call(
        functools.partial(_conv_kernel, L=L),
        grid=(nb,),
        in_specs=[pl.BlockSpec((L, 2 * GW), lambda b: (b, C_CONV // (2 * GW))),
                  pl.BlockSpec((CONV_K, GW), lambda b: (0, 0)),
                  pl.BlockSpec((1, GW), lambda b: (0, 0)),
                  pl.BlockSpec((1, GW), lambda b: (0, 0)),
                  pl.BlockSpec((1, GW), lambda b: (0, 0))],
        out_specs=pl.BlockSpec((L, GW), lambda b: (b, 0)),
        out_shape=jax.ShapeDtypeStruct((nb * L, GW), bf16),
        scratch_shapes=[pltpu.VMEM((L + 32, GW), f32), pltpu.VMEM((8, 64 + 24, GW), f32)],
        compiler_params=_cparams(("arbitrary",)),
        name="conv_module",
    )(u, conv_w, conv_b.reshape(1, GW), ln_g.reshape(1, GW), ln_b.reshape(1, GW))


def _ret_kernel(*refs, L, has_h0, want_state, n_prev):
    ld_ref, q_ref, k_ref, v_ref, g_ref, gn_ref = refs[:6]
    pos = 6
    h0_ref = None
    if has_h0:
        h0_ref = refs[pos]
        pos += 1
    prev_ref = None
    if n_prev:
        prev_ref = refs[pos]
        pos += 1
    o_ref = refs[pos]
    pos += 1
    hout_ref = None
    if want_state:
        hout_ref = refs[pos]
        pos += 1
    y_scr, st_scr, dec_scr, cs_scr = refs[pos:pos + 4]

    nc = L // CHUNK
    H = RET_HEADS
    row = lax.broadcasted_iota(jnp.int32, (CHUNK, CHUNK), 0)
    col = lax.broadcasted_iota(jnp.int32, (CHUNK, CHUNK), 1)
    rowf = row.astype(f32)
    diff = (row - col).astype(f32)
    kscale = RET_D ** -0.5
    sls = [slice(h * RET_D, (h + 1) * RET_D) for h in range(H)]

    for d in range(2):
        for h in range(H):
            if has_h0:
                st_scr[d, h] = h0_ref[d, h]
            else:
                st_scr[d, h] = jnp.zeros((RET_D, RET_D), f32)
    for h in range(H):
        laf = ld_ref[h]
        lab = ld_ref[H + h]
        dec_scr[h] = jnp.where(col < row, jnp.exp(diff * laf),
                               jnp.where(col > row, jnp.exp(-diff * lab), 2.0))


    def intra(c, carry):
        r0 = pl.multiple_of(c * CHUNK, CHUNK)
        qb = [q_ref[pl.ds(r0, CHUNK), sls[h]].astype(bf16) for h in range(H)]
        kb = [(k_ref[pl.ds(r0, CHUNK), sls[h]] * kscale).astype(bf16) for h in range(H)]
        vh = [v_ref[pl.ds(r0, CHUNK), sls[h]] for h in range(H)]
        s = [_nt(qb[h], kb[h]) for h in range(H)]
        cf = [_tn((vh[h] * jnp.exp((CHUNK - 1.0 - rowf) * ld_ref[h])).astype(bf16), kb[h])
              for h in range(H)]
        cb = [_tn((vh[h] * jnp.exp(rowf * ld_ref[H + h])).astype(bf16), kb[h]) for h in range(H)]
        m = [(s[h] * dec_scr[h]).astype(bf16) for h in range(H)]
        y = [_nn(m[h], vh[h].astype(bf16)) for h in range(H)]
        for h in range(H):
            y_scr[pl.ds(r0, CHUNK), sls[h]] = y[h]
            cs_scr[0, c, h] = cf[h]
            cs_scr[1, c, h] = cb[h]
        return carry

    lax.fori_loop(0, nc, intra, 0, unroll=min(nc, 4))

    def inter(d, c):
        r0 = pl.multiple_of(c * CHUNK, CHUNK)
        st = [st_scr[d, h] for h in range(H)]
        yi = [_nt(q_ref[pl.ds(r0, CHUNK), sls[h]].astype(bf16), st[h].astype(bf16)) for h in range(H)]
        for h in range(H):
            la = ld_ref[d * H + h]
            st_scr[d, h] = st[h] * jnp.exp(jnp.full((RET_D, RET_D), CHUNK * la, f32)) + cs_scr[d, c, h]
        return r0, yi

    def fwd(c, carry):
        r0, yi = inter(0, c)
        for h in range(H):
            y_scr[pl.ds(r0, CHUNK), sls[h]] += jnp.exp((rowf + 1.0) * ld_ref[h]) * yi[h]
        return carry

    lax.fori_loop(0, nc, fwd, 0, unroll=2)

    def bwd(i, carry):
        r0, yi = inter(1, nc - 1 - i)
        for h in range(H):
            sl = sls[h]
            y = y_scr[pl.ds(r0, CHUNK), sl] + jnp.exp((CHUNK - rowf) * ld_ref[H + h]) * yi[h]
            mu = jnp.mean(y, axis=-1, keepdims=True)
            yc = y - mu
            var = jnp.mean(yc * yc, axis=-1, keepdims=True)
            yn = yc * lax.rsqrt(var + EPS) * gn_ref[:, sl]
            g = g_ref[pl.ds(r0, CHUNK), sl]
            o_ref[pl.ds(r0, CHUNK), sl] = (_silu(g) * yn).astype(o_ref.dtype)
        return carry

    lax.fori_loop(0, nc, bwd, 0, unroll=2)

    if want_state:
        for l in range(n_prev):
            hout_ref[l] = prev_ref[l]
        for d in range(2):
            for h in range(RET_HEADS):
                hout_ref[n_prev, d, h] = st_scr[d, h]


def _retention(u, log_decay, gn_g, h0, layer, nb, L, want_state, prev=None):
    has_h0 = h0 is not None
    n_prev = 0 if prev is None else prev.shape[1]
    sshape = (2, RET_HEADS, RET_D, RET_D)
    cblk = lambda c: pl.BlockSpec((L, GW), lambda b, c=c: (b, c // GW))
    in_specs = [pl.BlockSpec(memory_space=pltpu.SMEM),
                cblk(C_RQ), cblk(C_RK), cblk(C_RV), cblk(C_RG),
                pl.BlockSpec((1, GW), lambda b: (0, 0))]
    args = [log_decay.reshape(2 * RET_HEADS), u, u, u, u, gn_g.reshape(1, GW)]
    if has_h0:
        in_specs.append(pl.BlockSpec((None, None, 2, RET_HEADS, RET_D, RET_D),
                                     lambda b: (b, layer, 0, 0, 0, 0)))
        args.append(h0)
    if n_prev:
        in_specs.append(pl.BlockSpec((None, n_prev) + sshape, lambda b: (b, 0, 0, 0, 0, 0)))
        args.append(prev)
    out_specs = [pl.BlockSpec((L, GW), lambda b: (b, 0))]
    out_shape = [jax.ShapeDtypeStruct((nb * L, GW), bf16)]
    if want_state:
        out_specs.append(pl.BlockSpec((None, n_prev + 1) + sshape, lambda b: (b, 0, 0, 0, 0, 0)))
        out_shape.append(jax.ShapeDtypeStruct((nb, n_prev + 1) + sshape, f32))
    res = pl.pallas_call(
        functools.partial(_ret_kernel, L=L, has_h0=has_h0, want_state=want_state, n_prev=n_prev),
        grid=(nb,),
        in_specs=in_specs,
        out_specs=out_specs,
        out_shape=out_shape,
        scratch_shapes=[pltpu.VMEM((L, GW), f32),
                        pltpu.VMEM((2, RET_HEADS, RET_D, RET_D), f32),
                        pltpu.VMEM((RET_HEADS, CHUNK, CHUNK), f32),
                        pltpu.VMEM((2, L // CHUNK, RET_HEADS, RET_D, RET_D), f32)],
        compiler_params=_cparams(("arbitrary",)),
        name="retention",
    )(*args)
    return res if want_state else (res[0], None)


def _softplus(x):
    return jnp.maximum(x, 0.0) + jnp.log(1.0 + jnp.exp(-jnp.abs(x)))


def _ssm_kernel(*refs, L, has_h0, want_state, n_prev):
    (xbc_ref, z_ref, dtc_ref, dtr_ref, cw_ref, cb_ref, alr_ref, alc_ref, dbr_ref, dbc_ref,
     dsk_ref, ng_ref) = refs[:12]
    pos = 12
    h0_ref = None
    if has_h0:
        h0_ref = refs[pos]
        pos += 1
    prev_ref = None
    if n_prev:
        prev_ref = refs[pos]
        pos += 1
    o_ref = refs[pos]
    pos += 1
    hout_ref = None
    if want_state:
        hout_ref = refs[pos]
        pos += 1
    xpad_ref, xc_scr, y_scr, st_scr, es_scr, cs_scr, tot_scr = refs[pos:pos + 7]

    nc = L // CHUNK
    H, P, N = SSM_HEADS, SSM_P, SSM_N
    HG = H // SSM_GROUPS
    pad = 8
    row = lax.broadcasted_iota(jnp.int32, (CHUNK, CHUNK), 0)
    col = lax.broadcasted_iota(jnp.int32, (CHUNK, CHUNK), 1)
    lt01 = (row >= col).astype(bf16)
    ut01 = (row <= col).astype(bf16)
    sub16 = lax.broadcasted_iota(jnp.int32, (2 * H, CHUNK), 0)
    lane_lo = col < P

    xpad_ref[0:pad, :] = jnp.zeros((pad, 2 * GW), f32)
    xpad_ref[pad + L:2 * pad + L, :] = jnp.zeros((pad, 2 * GW), f32)

    def fill(i, carry):
        r0 = pl.multiple_of(i * CHUNK, CHUNK)
        xpad_ref[pl.ds(pad + r0, CHUNK), :] = xbc_ref[pl.ds(r0, CHUNK), :]
        return carry

    lax.fori_loop(0, nc, fill, 0)

    for d in range(2):
        for h in range(H):
            g, hh = divmod(h, HG)
            if has_h0:
                st_scr[d, g, hh * P:(hh + 1) * P, :] = h0_ref[d, h]
            else:
                st_scr[d, g, hh * P:(hh + 1) * P, :] = jnp.zeros((P, N), f32)

    a_neg_r = -jnp.exp(alr_ref[...])
    a_neg_c = -jnp.exp(alc_ref[...])

    def per_head_lanes(v, base):
        cols = []
        for c2 in range(H // 2):
            a = jnp.broadcast_to(v[:, base + 2 * c2:base + 2 * c2 + 1], (CHUNK, 2 * P))
            b = jnp.broadcast_to(v[:, base + 2 * c2 + 1:base + 2 * c2 + 2], (CHUNK, 2 * P))
            cols.append(jnp.where(lane_lo, a, b))
        return jnp.concatenate(cols, axis=1)

    def grp(x, g, base):
        return x[:, base + g * N:base + (g + 1) * N]


    def intra(c, carry):
        r0 = pl.multiple_of(c * CHUNK, CHUNK)
        win = xpad_ref.at[pl.ds(r0, CHUNK + 2 * pad), :]
        off = pad - SSM_K // 2
        for c0 in range(0, 2 * GW, 128):
            ls = slice(c0, c0 + 128)
            acc = jnp.broadcast_to(cb_ref[:, ls], (CHUNK, 128))
            for k in range(SSM_K):
                acc = acc + win[off + k:off + k + CHUNK, ls] * cw_ref[k:k + 1, ls]
            xc_scr[pl.ds(r0, CHUNK), ls] = _silu(acc)
        xc = xc_scr[pl.ds(r0, CHUNK), :]
        xs = xc[:, 0:GW]
        bmb = [grp(xc, g, GW).astype(bf16) for g in range(SSM_GROUPS)]
        cmb = [grp(xc, g, GW + SSM_GROUPS * N).astype(bf16) for g in range(SSM_GROUPS)]
        gmat = [_nt(cmb[g], bmb[g]) for g in range(SSM_GROUPS)]

        dt_c = _softplus(dtc_ref[pl.ds(r0, CHUNK), :] + dbr_ref[...])
        dt_r = _softplus(dtr_ref[:, pl.ds(r0, CHUNK)] + dbc_ref[...])
        lc3 = _split3(dt_c * a_neg_r)
        lr3 = _split3(dt_r * a_neg_c)
        pc = [_nn(lt01, t) for t in lc3]
        sc = [_nn(ut01, t) for t in lc3]
        pr = [_nn(t, ut01) for t in lr3]
        sr = [_nn(t, lt01) for t in lr3]
        a_c = jnp.where(col < H, pc[0] + pc[1] + pc[2], sc[0] + sc[1] + sc[2])
        a_r = jnp.where(sub16 < H, pr[0] + pr[1] + pr[2], sr[0] + sr[1] + sr[2])
        tot = jnp.where(col[0:1, :] < H, a_c[CHUNK - 1:CHUNK, :], a_c[0:1, :])
        tot_scr[c] = jnp.broadcast_to(tot, (8, 128))
        ks = dt_c * jnp.exp(tot - a_c)
        es = jnp.exp(a_c)
        es_scr[0, pl.ds(r0, CHUNK), :] = per_head_lanes(es, 0)
        es_scr[1, pl.ds(r0, CHUNK), :] = per_head_lanes(es, H)
        xwf = (xs * per_head_lanes(ks, 0)).astype(bf16)
        xwb = (xs * per_head_lanes(ks, H)).astype(bf16)
        csf = [_tn(xwf[:, g * HG * P:(g + 1) * HG * P], bmb[g]) for g in range(SSM_GROUPS)]
        csb = [_tn(xwb[:, g * HG * P:(g + 1) * HG * P], bmb[g]) for g in range(SSM_GROUPS)]
        for g in range(SSM_GROUPS):
            cs_scr[0, c, g] = csf[g]
            cs_scr[1, c, g] = csb[g]

        ms = []
        for h in range(H):
            df = jnp.exp(jnp.minimum(a_c[:, h:h + 1] - a_r[h:h + 1, :], 0.0)) * dt_r[h:h + 1, :]
            db = jnp.exp(jnp.minimum(a_c[:, H + h:H + h + 1] - a_r[H + h:H + h + 1, :], 0.0)) \
                * dt_r[H + h:H + h + 1, :]
            dm = jnp.where(col <= row, df, 0.0) + jnp.where(col >= row, db, 0.0)
            ms.append((gmat[h // HG] * dm).astype(bf16))
        ys = []
        for c2 in range(H // 2):
            xcol = xs[:, c2 * 2 * P:(c2 + 1) * 2 * P]
            x_lo = jnp.where(lane_lo, xcol, 0.0).astype(bf16)
            x_hi = jnp.where(lane_lo, 0.0, xcol).astype(bf16)
            ys.append(_nn(ms[2 * c2], x_lo) + _nn(ms[2 * c2 + 1], x_hi))
        y = jnp.concatenate(ys, axis=1)
        y_scr[pl.ds(r0, CHUNK), :] = y + (dsk_ref[0:1, :] + dsk_ref[1:2, :]) * xs
        return carry

    lax.fori_loop(0, nc, intra, 0, unroll=min(nc, 4))

    def inter(d, c):
        r0 = pl.multiple_of(c * CHUNK, CHUNK)
        st = [st_scr[d, g] for g in range(SSM_GROUPS)]
        yi = [_nt(grp(xc_scr[pl.ds(r0, CHUNK), :], g, GW + SSM_GROUPS * N).astype(bf16),
                  st[g].astype(bf16)) for g in range(SSM_GROUPS)]
        tot = tot_scr[c]
        for g in range(SSM_GROUPS):
            dec = jnp.concatenate(
                [jnp.broadcast_to(jnp.exp(tot[0:1, d * H + g * HG + hh:d * H + g * HG + hh + 1]), (P, N))
                 for hh in range(HG)], axis=0)
            st_scr[d, g] = st[g] * dec + cs_scr[d, c, g]
        return r0, jnp.concatenate(yi, axis=1) * es_scr[d, pl.ds(r0, CHUNK), :]

    def fwd(c, carry):
        r0, yi = inter(0, c)
        y_scr[pl.ds(r0, CHUNK), :] += yi
        return carry

    lax.fori_loop(0, nc, fwd, 0, unroll=2)

    def bwd(i, carry):
        r0, yi = inter(1, nc - 1 - i)
        yz = (y_scr[pl.ds(r0, CHUNK), :] + yi) * _silu(z_ref[pl.ds(r0, CHUNK), :])
        ms = jnp.mean(yz * yz, axis=-1, keepdims=True)
        o_ref[pl.ds(r0, CHUNK), :] = (yz * lax.rsqrt(ms + EPS) * ng_ref[...]).astype(o_ref.dtype)
        return carry

    lax.fori_loop(0, nc, bwd, 0, unroll=2)

    if want_state:
        for l in range(n_prev):
            hout_ref[l] = prev_ref[l]
        for d in range(2):
            for h in range(H):
                g, hh = divmod(h, HG)
                hout_ref[n_prev, d, h] = st_scr[d, g, hh * P:(hh + 1) * P, :]


def _pad_lanes(v, n=128):
    return jnp.pad(v, ((0, 0), (0, n - v.shape[1])))


def _ssm(u, dt_t, p, h0, layer, nb, L, want_state, prev=None):
    has_h0 = h0 is not None
    n_prev = 0 if prev is None else prev.shape[1]
    sshape = (2, SSM_HEADS, SSM_P, SSM_N)
    H = SSM_HEADS
    a_log = p['ssm_a_log'].reshape(1, 2 * H)
    dt_bias = p['ssm_dt_bias'].reshape(1, 2 * H)
    small = lambda shape: pl.BlockSpec(shape, lambda b: (0, 0))
    in_specs = [pl.BlockSpec((L, 2 * GW), lambda b: (b, C_XBC // (2 * GW))),
                pl.BlockSpec((L, GW), lambda b: (b, C_Z // GW)),
                pl.BlockSpec((L, 128), lambda b: (b, C_DT // 128)),
                pl.BlockSpec((2 * H, L), lambda b: (0, b)),
                small((SSM_K, 2 * GW)), small((1, 2 * GW)),
                small((1, 128)), small((2 * H, 128)), small((1, 128)), small((2 * H, 128)),
                small((2, GW)), small((1, GW))]
    args = [u, u, u, dt_t, p['ssm_conv_w'], p['ssm_conv_b'].reshape(1, 2 * GW),
            _pad_lanes(a_log), jnp.broadcast_to(a_log.reshape(2 * H, 1), (2 * H, 128)),
            _pad_lanes(dt_bias), jnp.broadcast_to(dt_bias.reshape(2 * H, 1), (2 * H, 128)),
            jnp.repeat(p['ssm_d'], SSM_P, axis=1), p['ssm_norm'].reshape(1, GW)]
    if has_h0:
        in_specs.append(pl.BlockSpec((None, None, 2, H, SSM_P, SSM_N), lambda b: (b, layer, 0, 0, 0, 0)))
        args.append(h0)
    out_specs = [pl.BlockSpec((L, GW), lambda b: (b, 0))]
    out_shape = [jax.ShapeDtypeStruct((nb * L, GW), bf16)]
    if n_prev:
        in_specs.append(pl.BlockSpec((None, n_prev) + sshape, lambda b: (b, 0, 0, 0, 0, 0)))
        args.append(prev)
    if want_state:
        out_specs.append(pl.BlockSpec((None, n_prev + 1) + sshape, lambda b: (b, 0, 0, 0, 0, 0)))
        out_shape.append(jax.ShapeDtypeStruct((nb, n_prev + 1) + sshape, f32))
    res = pl.pallas_call(
        functools.partial(_ssm_kernel, L=L, has_h0=has_h0, want_state=want_state, n_prev=n_prev),
        grid=(nb,),
        in_specs=in_specs,
        out_specs=out_specs,
        out_shape=out_shape,
        scratch_shapes=[pltpu.VMEM((L + 16, 2 * GW), f32),
                        pltpu.VMEM((L, 2 * GW), f32),
                        pltpu.VMEM((L, GW), f32),
                        pltpu.VMEM((2, SSM_GROUPS, H // SSM_GROUPS * SSM_P, SSM_N), f32),
                        pltpu.VMEM((2, L, GW), f32),
                        pltpu.VMEM((2, L // CHUNK, SSM_GROUPS, H // SSM_GROUPS * SSM_P, SSM_N), f32),
                        pltpu.VMEM((L // CHUNK, 8, 128), f32)],
        compiler_params=_cparams(("arbitrary",)),
        name="ssd_mixer",
    )(*args)
    return res if want_state else (res[0], None)


def _ctx_att_kernel(sink_ref, q_ref, k_ref, v_ref, o_ref, *, L):
    G = ATT_HEADS // ATT_KV
    scale = ATT_D ** -0.5
    kb = [k_ref[:, j * ATT_D:(j + 1) * ATT_D].astype(bf16) for j in range(ATT_KV)]
    vb = [v_ref[:, j * ATT_D:(j + 1) * ATT_D].astype(bf16) for j in range(ATT_KV)]
    s = [_nt((q_ref[:, h * ATT_D:(h + 1) * ATT_D] * scale).astype(bf16), kb[h // G])
         for h in range(ATT_HEADS)]
    p, den = [], []
    for h in range(ATT_HEADS):
        sink = sink_ref[h]
        m = jnp.maximum(jnp.max(s[h], axis=-1, keepdims=True), sink)
        e = jnp.exp(s[h] - m)
        den.append(jnp.sum(e, axis=-1, keepdims=True) + jnp.exp(sink - m))
        p.append(e.astype(bf16))
    o = [_nn(p[h], vb[h // G]) for h in range(ATT_HEADS)]
    for h in range(ATT_HEADS):
        o_ref[:, h * ATT_D:(h + 1) * ATT_D] = (o[h] / den[h]).astype(o_ref.dtype)


def _ctx_attention(u, sink, nb, L):
    return pl.pallas_call(
        functools.partial(_ctx_att_kernel, L=L),
        grid=(nb,),
        in_specs=[pl.BlockSpec(memory_space=pltpu.SMEM),
                  pl.BlockSpec((L, GW), lambda b: (b, C_AQ // GW)),
                  pl.BlockSpec((L, 128), lambda b: (b, C_AK // 128)),
                  pl.BlockSpec((L, 128), lambda b: (b, C_AV // 128))],
        out_specs=pl.BlockSpec((L, GW), lambda b: (b, 0)),
        out_shape=jax.ShapeDtypeStruct((nb * L, GW), bf16),
        compiler_params=_cparams(("arbitrary",)),
        name="ctx_attention",
    )(sink, u, u, u)


def _rope(x, cos, sin):
    w = x.shape[1]
    lane = lax.broadcasted_iota(jnp.int32, x.shape, 1)
    first = (lane % 32) < 16
    rot = jnp.where(first, -pltpu.roll(x, w - 16, 1), pltpu.roll(x, 16, 1))
    return x * cos + rot * sin


def _half_variants(x, keep_fill):
    lane = lax.broadcasted_iota(jnp.int32, x.shape, 1)
    lo = lane < ATT_D
    a0 = jnp.where(lo, x, keep_fill)
    b1 = jnp.where(lo, keep_fill, x)
    xr = pltpu.roll(x, ATT_D, 1)
    a1 = jnp.where(lo, xr, keep_fill)
    b0 = jnp.where(lo, keep_fill, xr)
    return ((a0, b0), (a1, b1))


def _half_variants_t(xt):
    sub = lax.broadcasted_iota(jnp.int32, xt.shape, 0)
    lo = sub < ATT_D
    a0 = jnp.where(lo, xt, 0.0)
    b1 = jnp.where(lo, 0.0, xt)
    xr = pltpu.roll(xt, ATT_D, 0)
    a1 = jnp.where(lo, xr, 0.0)
    b0 = jnp.where(lo, 0.0, xr)
    return ((a0, b0), (a1, b1))


def _lat_att_kernel(sink_ref, q_ref, k_ref, v_ref, kc_ref, vc_ref, cos_ref, sin_ref, o_ref,
                    qr_scr, kt_scr, vv_scr, kct_scr, vcv_scr, bias_scr, *, L):
    G = ATT_HEADS // ATT_KV
    B = CHUNK
    nb = L // B
    scale = ATT_D ** -0.5
    zeros = jnp.zeros((B, 128), bf16)
    for j in range(ATT_KV):
        for hf in range(2):
            kt_scr[j, hf, 0] = zeros
            kt_scr[j, hf, nb + 1] = zeros
            vv_scr[j, hf, 0:B, :] = zeros
            vv_scr[j, hf, B + L:2 * B + L, :] = zeros
    kct = _half_variants_t(kc_ref[...].T)
    vcv = _half_variants(vc_ref[...], 1.0)
    for j in range(ATT_KV):
        for hf in range(2):
            kct_scr[j, hf] = kct[j][hf].astype(bf16)
            vcv_scr[j, hf] = vcv[j][hf].astype(bf16)

    rowi = lax.broadcasted_iota(jnp.int32, (2 * B, 3 * B), 0) % B
    coli = lax.broadcasted_iota(jnp.int32, (2 * B, 3 * B), 1)
    inwin = jnp.abs(rowi - (coli - B)) <= ATT_WIN
    ninf = jnp.float32(-jnp.inf)
    bias_scr[0] = jnp.where(inwin & (coli >= B), 0.0, ninf)
    bias_scr[1] = jnp.where(inwin, 0.0, ninf)
    bias_scr[2] = jnp.where(inwin & (coli < 2 * B), 0.0, ninf)

    def prep(n, carry):
        r0 = pl.multiple_of(n * B, B)
        cos = cos_ref[pl.ds(r0, B), :]
        sin = sin_ref[pl.ds(r0, B), :]
        cos4 = jnp.concatenate([cos] * 4, axis=1)
        sin4 = jnp.concatenate([sin] * 4, axis=1)
        qr_scr[pl.ds(r0, B), :] = (_rope(q_ref[pl.ds(r0, B), :], cos4, sin4) * scale).astype(bf16)
        kvar = _half_variants_t(_rope(k_ref[pl.ds(r0, B), :], cos, sin).T)
        vvar = _half_variants(v_ref[pl.ds(r0, B), :], 1.0)
        for j in range(ATT_KV):
            for hf in range(2):
                kt_scr[j, hf, n + 1] = kvar[j][hf].astype(bf16)
                vv_scr[j, hf, pl.ds(B + r0, B), :] = vvar[j][hf].astype(bf16)
        return carry

    lax.fori_loop(0, nb, prep, 0)

    lane_lo = lax.broadcasted_iota(jnp.int32, (2 * B, 128), 1) < ATT_D

    combos = [(j, hf) for j in range(ATT_KV) for hf in range(2)]
    BLK_GROUP = 8

    def scores(n):
        r0 = pl.multiple_of(n * B, B)
        qs = [jnp.concatenate([qr_scr[pl.ds(r0, B), (2 * j) * 128:(2 * j + 1) * 128],
                               qr_scr[pl.ds(r0, B), (2 * j + 1) * 128:(2 * j + 2) * 128]], axis=0)
              for j in range(ATT_KV)]
        s_c = [_nn(qs[j], kct_scr[j, hf]) for j, hf in combos]
        s_b = [jnp.concatenate([_nn(qs[j], kt_scr[j, hf, n + t]) for t in range(3)], axis=1)
               for j, hf in combos]
        return s_c, s_b

    def finish(n, s_c, s_b):
        r0 = pl.multiple_of(n * B, B)
        bias = bias_scr[jnp.where(n == 0, 0, jnp.where(n == nb - 1, 2, 1))]
        p_c, p_b, esink = [], [], []
        for i, (j, hf) in enumerate(combos):
            sb = s_b[i] + bias
            sink = jnp.concatenate([jnp.full((B, 1), sink_ref[G * j + hf], f32),
                                    jnp.full((B, 1), sink_ref[G * j + 2 + hf], f32)], axis=0)
            m = jnp.maximum(jnp.maximum(jnp.max(s_c[i], axis=-1, keepdims=True),
                                        jnp.max(sb, axis=-1, keepdims=True)), sink)
            p_c.append(jnp.exp(s_c[i] - m).astype(bf16))
            p_b.append(jnp.exp(sb - m).astype(bf16))
            esink.append(jnp.exp(sink - m))
        oe = [_nn(p_c[i], vcv_scr[j, hf]) + _nn(p_b[i], vv_scr[j, hf, pl.ds(r0, 3 * B), :])
              for i, (j, hf) in enumerate(combos)]
        outs = []
        for i, (j, hf) in enumerate(combos):
            den = oe[i][:, (1 - hf) * ATT_D:(1 - hf) * ATT_D + 1] + esink[i]
            outs.append(oe[i] / den)
        for j in range(ATT_KV):
            o = jnp.where(lane_lo, outs[2 * j], outs[2 * j + 1])
            o_ref[pl.ds(r0, B), (2 * j) * 128:(2 * j + 1) * 128] = o[0:B].astype(o_ref.dtype)
            o_ref[pl.ds(r0, B), (2 * j + 1) * 128:(2 * j + 2) * 128] = o[B:2 * B].astype(o_ref.dtype)

    def blk_group(i, carry):
        n0 = i * BLK_GROUP
        s_next = scores(n0)
        for t in range(BLK_GROUP):
            s_cur = s_next
            if t + 1 < BLK_GROUP:
                s_next = scores(n0 + t + 1)
            finish(n0 + t, *s_cur)
        return carry

    lax.fori_loop(0, nb // BLK_GROUP, blk_group, 0)


def _rope_tables(L):
    pos = jnp.arange(L)
    rows = (pos // GRID_W).astype(f32)
    cols = (pos % GRID_W).astype(f32)
    half = ATT_D // 4
    freqs = ROPE_THETA ** (-jnp.arange(half, dtype=f32) / half)
    ang_r = rows[:, None] * freqs
    ang_c = cols[:, None] * freqs
    ang = jnp.concatenate([ang_r, ang_r, ang_c, ang_c], axis=1)
    ang = jnp.concatenate([ang, ang], axis=1)
    return jnp.cos(ang), jnp.sin(ang)


def _lat_attention(u, k_cache, v_cache, sink, layer, nb, L):
    Lc = k_cache.shape[2]
    cos, sin = _rope_tables(L)
    kc = k_cache.reshape(nb, DEPTH, Lc, ATT_KV * ATT_D)
    vc = v_cache.reshape(nb, DEPTH, Lc, ATT_KV * ATT_D)
    cspec = pl.BlockSpec((None, None, Lc, 128), lambda b: (b, layer, 0, 0))
    return pl.pallas_call(
        functools.partial(_lat_att_kernel, L=L),
        grid=(nb,),
        in_specs=[pl.BlockSpec(memory_space=pltpu.SMEM),
                  pl.BlockSpec((L, GW), lambda b: (b, C_AQ // GW)),
                  pl.BlockSpec((L, 128), lambda b: (b, C_AK // 128)),
                  pl.BlockSpec((L, 128), lambda b: (b, C_AV // 128)),
                  cspec, cspec,
                  pl.BlockSpec((L, 128), lambda b: (0, 0)),
                  pl.BlockSpec((L, 128), lambda b: (0, 0))],
        out_specs=pl.BlockSpec((L, GW), lambda b: (b, 0)),
        out_shape=jax.ShapeDtypeStruct((nb * L, GW), bf16),
        scratch_shapes=[pltpu.VMEM((L, GW), bf16),
                        pltpu.VMEM((ATT_KV, 2, L // CHUNK + 2, 128, CHUNK), bf16),
                        pltpu.VMEM((ATT_KV, 2, L + 2 * CHUNK, 128), bf16),
                        pltpu.VMEM((ATT_KV, 2, 128, Lc), bf16),
                        pltpu.VMEM((ATT_KV, 2, Lc, 128), bf16),
                        pltpu.VMEM((3, 2 * CHUNK, 3 * CHUNK), f32)],
        compiler_params=_cparams(("arbitrary",)),
        name="lat_attention",
    )(sink, u, u, u, kc, vc, cos, sin)


def _permute_w_in_kernel(wt_ref, o_ref):
    ndt = 2 * SSM_HEADS
    moves = [(0, C_CONV, 1024), (1536, C_XBC, 1024), (1024, C_Z, GW), (2560 + ndt, C_RQ, C_DT - C_RQ)]
    for src, dst, n in moves:
        for r in range(0, n, 128):
            o_ref[:, dst + r:dst + r + 128] = wt_ref[src + r:src + r + 128, :].T.astype(bf16)
    tail = wt_ref[2560:2560 + 128, :].T
    lane = lax.broadcasted_iota(jnp.int32, tail.shape, 1)
    o_ref[:, C_DT:UW] = jnp.where(lane < ndt, tail, 0.0).astype(bf16)


def _permute_w_in(w):
    tr = 256
    return pl.pallas_call(
        _permute_w_in_kernel,
        grid=(DEPTH, D // tr),
        in_specs=[pl.BlockSpec((None, N_IN, tr), lambda l, i: (l, 0, i))],
        out_specs=pl.BlockSpec((None, tr, UW), lambda l, i: (l, i, 0)),
        out_shape=jax.ShapeDtypeStruct((DEPTH, D, UW), bf16),
        compiler_params=_cparams(("arbitrary", "arbitrary")),
        name="permute_w_in",
    )(jnp.swapaxes(w, 1, 2))


def _mix_and_mlp(x, u, dt_t, mod, p, ctx, layer, nb, L, final, prev_ssm=None, prev_ret=None):
    per_seq = ctx is not None
    want_state = ctx is None
    o_conv = _conv_module(u, p['conv_w'], p['conv_b'], p['conv_ln_g'], p['conv_ln_b'], nb, L)
    if ctx is None:
        o_ssm, h_ssm = _ssm(u, dt_t, p, None, layer, nb, L, True, prev_ssm)
        o_ret, h_ret = _retention(u, p['ret_log_decay'], p['ret_gn_g'], None, layer, nb, L, True, prev_ret)
        o_att = _ctx_attention(u, p['att_sink'], nb, L)
    else:
        k_c, v_c, s_ssm, s_ret = ctx
        o_ssm, h_ssm = _ssm(u, dt_t, p, s_ssm, layer, nb, L, False)
        o_ret, h_ret = _retention(u, p['ret_log_decay'], p['ret_gn_g'], s_ret, layer, nb, L, False)
        o_att = _lat_attention(u, k_c, v_c, p['att_sink'], layer, nb, L)
    x1 = _outproj(o_conv, o_ssm, o_ret, o_att, p['w_out_b'], x, mod, L, per_seq)
    x2 = _mlp(x1, p['norm_mlp'], mod, p['final_norm'], p['w1_b'], p['w2_b'], L, per_seq, final)
    states = None
    if want_state:
        ak = u[:, C_AK:C_AK + 128].reshape(nb, L, ATT_KV, ATT_D)
        av = u[:, C_AV:C_AV + 128].reshape(nb, L, ATT_KV, ATT_D)
        states = (ak, av, h_ssm, h_ret)
    return x2, states


def kernel(x_prompt, x_sample, cache_attn_k, cache_attn_v, state_ssm, state_ret, c, c_ctx, ada_w, ada_b, norm_mix, norm_mlp, w_in, conv_w, conv_b, conv_ln_g, conv_ln_b, ssm_conv_w, ssm_conv_b, ssm_a_log, ssm_dt_bias, ssm_d, ssm_norm, ret_log_decay, ret_gn_g, att_sink, w_out, w1, w2, final_norm):
    nbp, Lp, _ = x_prompt.shape
    nbs, Ls, _ = x_sample.shape
    cvec = jnp.concatenate([c_ctx[None, :], c, jnp.zeros((16 - 1 - nbs, D), f32)], axis=0)
    mod = _ada_mod(cvec, ada_w, ada_b)
    y_p = x_prompt.reshape(nbp * Lp, D)
    y_s = x_sample.reshape(nbs * Ls, D)
    new_k, new_v = [], []
    hs, hr = None, None
    w_in_p = _permute_w_in(w_in)
    ctx = (cache_attn_k, cache_attn_v, state_ssm, state_ret)
    for l in range(DEPTH):
        mod_l = mod[l].reshape(16, 6, 1, D)
        mod_p, mod_s = mod_l[0:1], mod_l[1:1 + nbs]
        u_p, dtt_p, (w_out_b,) = _inproj(y_p, norm_mix[l], mod_p, w_in_p, l, Lp, False, [w_out])
        u_s, dtt_s, (w1_b, w2_b) = _inproj(y_s, norm_mix[l], mod_s, w_in_p, l, Ls, True, [w1, w2])
        p = dict(norm_mlp=norm_mlp[l],
                 conv_w=conv_w[l], conv_b=conv_b[l], conv_ln_g=conv_ln_g[l], conv_ln_b=conv_ln_b[l],
                 ssm_conv_w=ssm_conv_w[l], ssm_conv_b=ssm_conv_b[l], ssm_a_log=ssm_a_log[l],
                 ssm_dt_bias=ssm_dt_bias[l], ssm_d=ssm_d[l], ssm_norm=ssm_norm[l],
                 ret_log_decay=ret_log_decay[l], ret_gn_g=ret_gn_g[l], att_sink=att_sink[l],
                 w_out_b=w_out_b, w1_b=w1_b, w2_b=w2_b, final_norm=final_norm)
        final = l == DEPTH - 1
        y_p, (k_l, v_l, hs, hr) = _mix_and_mlp(y_p, u_p, dtt_p, mod_p, p, None, l, nbp, Lp, final, hs, hr)
        new_k.append(k_l)
        new_v.append(v_l)
        y_s, _ = _mix_and_mlp(y_s, u_s, dtt_s, mod_s, p, ctx, l, nbs, Ls, final)
    return (y_p.reshape(nbp, Lp, D), y_s.reshape(nbs, Ls, D),
            jnp.stack(new_k, axis=1), jnp.stack(new_v, axis=1), hs, hr)
```

```python
import functools

import jax
import jax.numpy as jnp
from jax import lax
from jax.experimental import pallas as pl
from jax.experimental.pallas import tpu as pltpu

f32 = jnp.float32
bf16 = jnp.bfloat16

D = 2048
DEPTH = 2
GW = 512
CONV_K = 31
SSM_HEADS, SSM_P, SSM_N, SSM_GROUPS, SSM_K = 8, 64, 128, 2, 5
RET_HEADS, RET_D = 4, 128
ATT_HEADS, ATT_KV, ATT_D, ATT_WIN = 8, 2, 64, 128
GRID_W = 64
ROPE_THETA = 10000.0
CHUNK = 128
D_FF = 4 * D
EPS = 1e-6

C_CONV, C_XBC, C_Z, C_RQ, C_RK, C_RV, C_RG, C_AQ, C_AK, C_AV, C_DT = (
    0, 1024, 2048, 2560, 3072, 3584, 4096, 4608, 5120, 5248, 5376)
UW = 5504
N_IN = 5392
N_CHUNK = 512

VMEM_LIMIT = 56 * 1024 * 1024


def _cparams(sem):
    return pltpu.CompilerParams(dimension_semantics=sem, vmem_limit_bytes=VMEM_LIMIT)


def _sigmoid(x):
    return 1.0 / (1.0 + jnp.exp(-x))


def _silu(x):
    return x * _sigmoid(x)


def _nt(a, b):
    return lax.dot_general(a, b, (((1,), (1,)), ((), ())), preferred_element_type=f32)


def _tn(a, b):
    return lax.dot_general(a, b, (((0,), (0,)), ((), ())), preferred_element_type=f32)


def _nn(a, b):
    return jnp.dot(a, b, preferred_element_type=f32)


def _split3(x):
    hi = x.astype(bf16)
    r1 = x - hi.astype(f32)
    mid = r1.astype(bf16)
    lo = (r1 - mid.astype(f32)).astype(bf16)
    return hi, mid, lo


def _ada_kernel(c_ref, w_ref, b_ref, o_ref):
    c = c_ref[...]
    s = _silu(c).astype(bf16)
    o_ref[...] = _nn(s, w_ref[...].astype(bf16)) + b_ref[...]


def _ada_mod(cvec, ada_w, ada_b):
    tn = 1024
    return pl.pallas_call(
        _ada_kernel,
        grid=(DEPTH, 6 * D // tn),
        in_specs=[pl.BlockSpec((16, D), lambda l, j: (0, 0)),
                  pl.BlockSpec((None, D, tn), lambda l, j: (l, 0, j)),
                  pl.BlockSpec((None, 1, tn), lambda l, j: (l, 0, j))],
        out_specs=pl.BlockSpec((None, 16, tn), lambda l, j: (l, 0, j)),
        out_shape=jax.ShapeDtypeStruct((DEPTH, 16, 6 * D), f32),
        compiler_params=_cparams(("arbitrary", "arbitrary")),
        name="ada_mod",
    )(cvec, ada_w, ada_b.reshape(DEPTH, 1, 6 * D))


def _mod_spec(idx, tm, L, per_seq):
    if per_seq:
        return pl.BlockSpec((None, None, 1, D), lambda i, *_: ((i * tm) // L, idx, 0, 0))
    return pl.BlockSpec((None, None, 1, D), lambda i, *_: (0, idx, 0, 0))


def _inproj_kernel(x_ref, g_ref, sc_ref, sh_ref, w_ref, *refs, n_cast):
    cast_in, u_ref, dtt_ref, cast_out = refs[:n_cast], refs[n_cast], refs[n_cast + 1], refs[n_cast + 2:]
    x = x_ref[...]
    ms = jnp.mean(x * x, axis=-1, keepdims=True)
    h = x * lax.rsqrt(ms + EPS) * (g_ref[...] * (1.0 + sc_ref[...])) + sh_ref[...]
    hb = h.astype(bf16)
    for n0 in range(0, UW, N_CHUNK):
        n1 = min(n0 + N_CHUNK, UW)
        u_ref[:, n0:n1] = _nn(hb, w_ref[:, n0:n1])
    dtt_ref[...] = u_ref[:, C_DT:C_DT + 128].T[0:2 * SSM_HEADS, :]
    for src, dst in zip(cast_in, cast_out):
        dst[...] = src[...].astype(bf16)


def _inproj(x, norm_g, mod, w_in_p, layer, L, per_seq, casts):
    T = x.shape[0]
    tm = 256
    n = T // tm
    cast_in_specs, cast_out_specs, cast_shapes = [], [], []
    for w in casts:
        _, R, C = w.shape
        cast_in_specs.append(pl.BlockSpec((None, R // n, C), lambda i: (layer, i, 0)))
        cast_out_specs.append(pl.BlockSpec((R // n, C), lambda i: (i, 0)))
        cast_shapes.append(jax.ShapeDtypeStruct((R, C), bf16))
    res = pl.pallas_call(
        functools.partial(_inproj_kernel, n_cast=len(casts)),
        grid=(n,),
        in_specs=[pl.BlockSpec((tm, D), lambda i: (i, 0)),
                  pl.BlockSpec((1, D), lambda i: (0, 0)),
                  _mod_spec(1, tm, L, per_seq),
                  _mod_spec(0, tm, L, per_seq),
                  pl.BlockSpec((None, D, UW), lambda i: (layer, 0, 0), pipeline_mode=pl.Buffered(1))]
        + cast_in_specs,
        out_specs=[pl.BlockSpec((tm, UW), lambda i: (i, 0)),
                   pl.BlockSpec((2 * SSM_HEADS, tm), lambda i: (0, i))] + cast_out_specs,
        out_shape=[jax.ShapeDtypeStruct((T, UW), f32),
                   jax.ShapeDtypeStruct((2 * SSM_HEADS, T), f32)] + cast_shapes,
        compiler_params=_cparams(("arbitrary",)),
        name="in_proj",
    )(x, norm_g.reshape(1, D), mod, mod, w_in_p, *casts)
    return res[0], res[1], res[2:]


def _outproj_kernel(oc_ref, os_ref, or_ref, oa_ref, w_ref, x_ref, g_ref, o_ref):
    mix = jnp.concatenate([oc_ref[...], os_ref[...], or_ref[...], oa_ref[...]], axis=1)
    for n0 in range(0, D, N_CHUNK):
        n1 = n0 + N_CHUNK
        o_ref[:, n0:n1] = x_ref[:, n0:n1] + g_ref[:, n0:n1] * _nn(mix, w_ref[:, n0:n1])


def _outproj(o_conv, o_ssm, o_ret, o_att, w_out_b, x, mod, L, per_seq):
    T = x.shape[0]
    tm = 512
    ospec = pl.BlockSpec((tm, GW), lambda i: (i, 0))
    return pl.pallas_call(
        _outproj_kernel,
        grid=(T // tm,),
        in_specs=[ospec, ospec, ospec, ospec,
                  pl.BlockSpec((D, D), lambda i: (0, 0), pipeline_mode=pl.Buffered(1)),
                  pl.BlockSpec((tm, D), lambda i: (i, 0)),
                  _mod_spec(2, tm, L, per_seq)],
        out_specs=pl.BlockSpec((tm, D), lambda i: (i, 0)),
        out_shape=jax.ShapeDtypeStruct((T, D), f32),
        compiler_params=_cparams(("arbitrary",)),
        name="out_proj",
    )(o_conv, o_ssm, o_ret, o_att, w_out_b, x, mod)


def _mlp_kernel(x_ref, g_ref, sc_ref, sh_ref, g2_ref, fn_ref, w1_ref, w2_ref, o_ref, h_scr, *, final):
    f = pl.program_id(1)
    tm = x_ref.shape[0]
    RB = 256

    def ffn_tile(r):
        a = _nn(h_scr[r:r + 2 * RB, :], w1_ref[...])
        a = jnp.maximum(a, 0.0)
        a = (a * a).astype(bf16)
        return _nn(a, w2_ref[...])

    @pl.when(f == 0)
    def _():
        gs = g_ref[...] * (1.0 + sc_ref[...])
        for r in range(0, tm, RB):
            x = x_ref[r:r + RB, :]
            ms = jnp.mean(x * x, axis=-1, keepdims=True)
            h_scr[r:r + RB, :] = (x * lax.rsqrt(ms + EPS) * gs + sh_ref[...]).astype(bf16)
        for r in range(0, tm, 2 * RB):
            o_ref[r:r + 2 * RB, :] = ffn_tile(r)

    last = pl.num_programs(1) - 1

    @pl.when(jnp.logical_and(f > 0, f < last))
    def _():
        for r in range(0, tm, 2 * RB):
            o_ref[r:r + 2 * RB, :] += ffn_tile(r)

    @pl.when(f == last)
    def _():
        for r in range(0, tm, 2 * RB):
            acc = o_ref[r:r + 2 * RB, :] + ffn_tile(r)
            y = x_ref[r:r + 2 * RB, :] + g2_ref[...] * acc
            if final:
                ms = jnp.mean(y * y, axis=-1, keepdims=True)
                y = y * lax.rsqrt(ms + EPS) * fn_ref[...]
            o_ref[r:r + 2 * RB, :] = y


def _mlp(x, norm_g, mod, final_norm, w1_b, w2_b, L, per_seq, final):
    T = x.shape[0]
    tm, tf = 1024, 512
    return pl.pallas_call(
        functools.partial(_mlp_kernel, final=final),
        grid=(T // tm, D_FF // tf),
        in_specs=[pl.BlockSpec((tm, D), lambda i, f: (i, 0)),
                  pl.BlockSpec((1, D), lambda i, f: (0, 0)),
                  _mod_spec(4, tm, L, per_seq),
                  _mod_spec(3, tm, L, per_seq),
                  _mod_spec(5, tm, L, per_seq),
                  pl.BlockSpec((1, D), lambda i, f: (0, 0)),
                  pl.BlockSpec((D, tf), lambda i, f: (0, f)),
                  pl.BlockSpec((tf, D), lambda i, f: (f, 0))],
        out_specs=pl.BlockSpec((tm, D), lambda i, f: (i, 0)),
        out_shape=jax.ShapeDtypeStruct((T, D), f32),
        scratch_shapes=[pltpu.VMEM((tm, D), bf16)],
        compiler_params=_cparams(("arbitrary", "arbitrary")),
        name="mlp",
    )(x, norm_g.reshape(1, D), mod, mod, mod, final_norm.reshape(1, D), w1_b, w2_b)


def _conv_kernel(u_ref, w_ref, b_ref, lg_ref, lb_ref, o_ref, vpad_ref, ph_ref, *, L):
    RC = 64
    pad = 16
    SUB = 8
    off0 = pad - CONV_K // 2
    nph = RC + SUB * ((off0 + CONV_K - 1) // SUB)
    vpad_ref[0:pad, :] = jnp.zeros((pad, GW), f32)
    vpad_ref[pad + L:2 * pad + L, :] = jnp.zeros((pad, GW), f32)

    def fill(i, carry):
        r0 = pl.multiple_of(i * RC, RC)
        a = u_ref[pl.ds(r0, RC), 0:GW]
        g = u_ref[pl.ds(r0, RC), GW:2 * GW]
        vpad_ref[pl.ds(pad + r0, RC), :] = a * _sigmoid(g)
        return carry

    lax.fori_loop(0, L // RC, fill, 0)

    def body(i, carry):
        r0 = pl.multiple_of(i * RC, RC)
        win = vpad_ref.at[pl.ds(r0, RC + 2 * pad), :]
        accs = []
        for c0 in range(0, GW, 128):
            ls = slice(c0, c0 + 128)
            for ph in range(SUB):
                ph_ref[ph, :, ls] = win[ph:ph + nph, ls]
            acc = jnp.broadcast_to(b_ref[:, ls], (RC, 128))
            for k in range(CONV_K):
                a, ph = divmod(off0 + k, SUB)
                acc = acc + ph_ref[ph, SUB * a:SUB * a + RC, ls] * w_ref[k:k + 1, ls]
            accs.append(acc)
        acc = jnp.concatenate(accs, axis=1)
        mu = jnp.mean(acc, axis=-1, keepdims=True)
        xc = acc - mu
        var = jnp.mean(xc * xc, axis=-1, keepdims=True)
        y = xc * lax.rsqrt(var + EPS) * lg_ref[...] + lb_ref[...]
        o_ref[pl.ds(r0, RC), :] = _silu(y).astype(o_ref.dtype)
        return carry

    lax.fori_loop(0, L // RC, body, 0, unroll=2)


def _conv_module(u, conv_w, conv_b, ln_g, ln_b, nb, L):
    return pl.pallas_call(
        functools.partial(_conv_kernel, L=L),
        grid=(nb,),
        in_specs=[pl.BlockSpec((L, 2 * GW), lambda b: (b, C_CONV // (2 * GW))),
                  pl.BlockSpec((CONV_K, GW), lambda b: (0, 0)),
                  pl.BlockSpec((1, GW), lambda b: (0, 0)),
                  pl.BlockSpec((1, GW), lambda b: (0, 0)),
                  pl.BlockSpec((1, GW), lambda b: (0, 0))],
        out_specs=pl.BlockSpec((L, GW), lambda b: (b, 0)),
        out_shape=jax.ShapeDtypeStruct((nb * L, GW), bf16),
        scratch_shapes=[pltpu.VMEM((L + 32, GW), f32), pltpu.VMEM((8, 64 + 24, GW), f32)],
        compiler_params=_cparams(("arbitrary",)),
        name="conv_module",
    )(u, conv_w, conv_b.reshape(1, GW), ln_g.reshape(1, GW), ln_b.reshape(1, GW))


def _ret_kernel(*refs, L, has_h0, want_state, n_prev):
    ld_ref, q_ref, k_ref, v_ref, g_ref, gn_ref = refs[:6]
    pos = 6
    h0_ref = None
    if has_h0:
        h0_ref = refs[pos]
        pos += 1
    prev_ref = None
    if n_prev:
        prev_ref = refs[pos]
        pos += 1
    o_ref = refs[pos]
    pos += 1
    hout_ref = None
    if want_state:
        hout_ref = refs[pos]
        pos += 1
    y_scr, st_scr, dec_scr, cs_scr = refs[pos:pos + 4]

    nc = L // CHUNK
    H = RET_HEADS
    row = lax.broadcasted_iota(jnp.int32, (CHUNK, CHUNK), 0)
    col = lax.broadcasted_iota(jnp.int32, (CHUNK, CHUNK), 1)
    rowf = row.astype(f32)
    diff = (row - col).astype(f32)
    kscale = RET_D ** -0.5
    sls = [slice(h * RET_D, (h + 1) * RET_D) for h in range(H)]

    for d in range(2):
        for h in range(H):
            if has_h0:
                st_scr[d, h] = h0_ref[d, h]
            else:
                st_scr[d, h] = jnp.zeros((RET_D, RET_D), f32)
    for h in range(H):
        laf = ld_ref[h]
        lab = ld_ref[H + h]
        dec_scr[h] = jnp.where(col < row, jnp.exp(diff * laf),
                               jnp.where(col > row, jnp.exp(-diff * lab), 2.0))


    def intra(c, carry):
        r0 = pl.multiple_of(c * CHUNK, CHUNK)
        qb = [q_ref[pl.ds(r0, CHUNK), sls[h]].astype(bf16) for h in range(H)]
        kb = [(k_ref[pl.ds(r0, CHUNK), sls[h]] * kscale).astype(bf16) for h in range(H)]
        vh = [v_ref[pl.ds(r0, CHUNK), sls[h]] for h in range(H)]
        s = [_nt(qb[h], kb[h]) for h in range(H)]
        cf = [_tn((vh[h] * jnp.exp((CHUNK - 1.0 - rowf) * ld_ref[h])).astype(bf16), kb[h])
              for h in range(H)]
        cb = [_tn((vh[h] * jnp.exp(rowf * ld_ref[H + h])).astype(bf16), kb[h]) for h in range(H)]
        m = [(s[h] * dec_scr[h]).astype(bf16) for h in range(H)]
        y = [_nn(m[h], vh[h].astype(bf16)) for h in range(H)]
        for h in range(H):
            y_scr[pl.ds(r0, CHUNK), sls[h]] = y[h]
            cs_scr[0, c, h] = cf[h]
            cs_scr[1, c, h] = cb[h]
        return carry

    lax.fori_loop(0, nc, intra, 0, unroll=min(nc, 4))

    def inter(d, c):
        r0 = pl.multiple_of(c * CHUNK, CHUNK)
        st = [st_scr[d, h] for h in range(H)]
        yi = [_nt(q_ref[pl.ds(r0, CHUNK), sls[h]].astype(bf16), st[h].astype(bf16)) for h in range(H)]
        for h in range(H):
            la = ld_ref[d * H + h]
            st_scr[d, h] = st[h] * jnp.exp(jnp.full((RET_D, RET_D), CHUNK * la, f32)) + cs_scr[d, c, h]
        return r0, yi

    def fwd(c, carry):
        r0, yi = inter(0, c)
        for h in range(H):
            y_scr[pl.ds(r0, CHUNK), sls[h]] += jnp.exp((rowf + 1.0) * ld_ref[h]) * yi[h]
        return carry

    lax.fori_loop(0, nc, fwd, 0, unroll=2)

    def bwd(i, carry):
        r0, yi = inter(1, nc - 1 - i)
        for h in range(H):
            sl = sls[h]
            y = y_scr[pl.ds(r0, CHUNK), sl] + jnp.exp((CHUNK - rowf) * ld_ref[H + h]) * yi[h]
            mu = jnp.mean(y, axis=-1, keepdims=True)
            yc = y - mu
            var = jnp.mean(yc * yc, axis=-1, keepdims=True)
            yn = yc * lax.rsqrt(var + EPS) * gn_ref[:, sl]
            g = g_ref[pl.ds(r0, CHUNK), sl]
            o_ref[pl.ds(r0, CHUNK), sl] = (_silu(g) * yn).astype(o_ref.dtype)
        return carry

    lax.fori_loop(0, nc, bwd, 0, unroll=2)

    if want_state:
        for l in range(n_prev):
            hout_ref[l] = prev_ref[l]
        for d in range(2):
            for h in range(RET_HEADS):
                hout_ref[n_prev, d, h] = st_scr[d, h]


def _retention(u, log_decay, gn_g, h0, layer, nb, L, want_state, prev=None):
    has_h0 = h0 is not None
    n_prev = 0 if prev is None else prev.shape[1]
    sshape = (2, RET_HEADS, RET_D, RET_D)
    cblk = lambda c: pl.BlockSpec((L, GW), lambda b, c=c: (b, c // GW))
    in_specs = [pl.BlockSpec(memory_space=pltpu.SMEM),
                cblk(C_RQ), cblk(C_RK), cblk(C_RV), cblk(C_RG),
                pl.BlockSpec((1, GW), lambda b: (0, 0))]
    args = [log_decay.reshape(2 * RET_HEADS), u, u, u, u, gn_g.reshape(1, GW)]
    if has_h0:
        in_specs.append(pl.BlockSpec((None, None, 2, RET_HEADS, RET_D, RET_D),
                                     lambda b: (b, layer, 0, 0, 0, 0)))
        args.append(h0)
    if n_prev:
        in_specs.append(pl.BlockSpec((None, n_prev) + sshape, lambda b: (b, 0, 0, 0, 0, 0)))
        args.append(prev)
    out_specs = [pl.BlockSpec((L, GW), lambda b: (b, 0))]
    out_shape = [jax.ShapeDtypeStruct((nb * L, GW), bf16)]
    if want_state:
        out_specs.append(pl.BlockSpec((None, n_prev + 1) + sshape, lambda b: (b, 0, 0, 0, 0, 0)))
        out_shape.append(jax.ShapeDtypeStruct((nb, n_prev + 1) + sshape, f32))
    res = pl.pallas_call(
        functools.partial(_ret_kernel, L=L, has_h0=has_h0, want_state=want_state, n_prev=n_prev),
        grid=(nb,),
        in_specs=in_specs,
        out_specs=out_specs,
        out_shape=out_shape,
        scratch_shapes=[pltpu.VMEM((L, GW), f32),
                        pltpu.VMEM((2, RET_HEADS, RET_D, RET_D), f32),
                        pltpu.VMEM((RET_HEADS, CHUNK, CHUNK), f32),
                        pltpu.VMEM((2, L // CHUNK, RET_HEADS, RET_D, RET_D), f32)],
        compiler_params=_cparams(("arbitrary",)),
        name="retention",
    )(*args)
    return res if want_state else (res[0], None)


def _softplus(x):
    return jnp.maximum(x, 0.0) + jnp.log(1.0 + jnp.exp(-jnp.abs(x)))


def _ssm_kernel(*refs, L, has_h0, want_state, n_prev):
    (xbc_ref, z_ref, dtc_ref, dtr_ref, cw_ref, cb_ref, alr_ref, alc_ref, dbr_ref, dbc_ref,
     dsk_ref, ng_ref) = refs[:12]
    pos = 12
    h0_ref = None
    if has_h0:
        h0_ref = refs[pos]
        pos += 1
    prev_ref = None
    if n_prev:
        prev_ref = refs[pos]
        pos += 1
    o_ref = refs[pos]
    pos += 1
    hout_ref = None
    if want_state:
        hout_ref = refs[pos]
        pos += 1
    xpad_ref, xc_scr, y_scr, st_scr, es_scr, cs_scr, tot_scr = refs[pos:pos + 7]

    nc = L // CHUNK
    H, P, N = SSM_HEADS, SSM_P, SSM_N
    HG = H // SSM_GROUPS
    pad = 8
    row = lax.broadcasted_iota(jnp.int32, (CHUNK, CHUNK), 0)
    col = lax.broadcasted_iota(jnp.int32, (CHUNK, CHUNK), 1)
    lt01 = (row >= col).astype(bf16)
    ut01 = (row <= col).astype(bf16)
    sub16 = lax.broadcasted_iota(jnp.int32, (2 * H, CHUNK), 0)
    lane_lo = col < P

    xpad_ref[0:pad, :] = jnp.zeros((pad, 2 * GW), f32)
    xpad_ref[pad + L:2 * pad + L, :] = jnp.zeros((pad, 2 * GW), f32)

    def fill(i, carry):
        r0 = pl.multiple_of(i * CHUNK, CHUNK)
        xpad_ref[pl.ds(pad + r0, CHUNK), :] = xbc_ref[pl.ds(r0, CHUNK), :]
        return carry

    lax.fori_loop(0, nc, fill, 0)

    for d in range(2):
        for h in range(H):
            g, hh = divmod(h, HG)
            if has_h0:
                st_scr[d, g, hh * P:(hh + 1) * P, :] = h0_ref[d, h]
            else:
                st_scr[d, g, hh * P:(hh + 1) * P, :] = jnp.zeros((P, N), f32)

    a_neg_r = -jnp.exp(alr_ref[...])
    a_neg_c = -jnp.exp(alc_ref[...])

    def per_head_lanes(v, base):
        cols = []
        for c2 in range(H // 2):
            a = jnp.broadcast_to(v[:, base + 2 * c2:base + 2 * c2 + 1], (CHUNK, 2 * P))
            b = jnp.broadcast_to(v[:, base + 2 * c2 + 1:base + 2 * c2 + 2], (CHUNK, 2 * P))
            cols.append(jnp.where(lane_lo, a, b))
        return jnp.concatenate(cols, axis=1)

    def grp(x, g, base):
        return x[:, base + g * N:base + (g + 1) * N]


    def intra(c, carry):
        r0 = pl.multiple_of(c * CHUNK, CHUNK)
        win = xpad_ref.at[pl.ds(r0, CHUNK + 2 * pad), :]
        off = pad - SSM_K // 2
        for c0 in range(0, 2 * GW, 128):
            ls = slice(c0, c0 + 128)
            acc = jnp.broadcast_to(cb_ref[:, ls], (CHUNK, 128))
            for k in range(SSM_K):
                acc = acc + win[off + k:off + k + CHUNK, ls] * cw_ref[k:k + 1, ls]
            xc_scr[pl.ds(r0, CHUNK), ls] = _silu(acc)
        xc = xc_scr[pl.ds(r0, CHUNK), :]
        xs = xc[:, 0:GW]
        bmb = [grp(xc, g, GW).astype(bf16) for g in range(SSM_GROUPS)]
        cmb = [grp(xc, g, GW + SSM_GROUPS * N).astype(bf16) for g in range(SSM_GROUPS)]
        gmat = [_nt(cmb[g], bmb[g]) for g in range(SSM_GROUPS)]

        dt_c = _softplus(dtc_ref[pl.ds(r0, CHUNK), :] + dbr_ref[...])
        dt_r = _softplus(dtr_ref[:, pl.ds(r0, CHUNK)] + dbc_ref[...])
        lc3 = _split3(dt_c * a_neg_r)
        lr3 = _split3(dt_r * a_neg_c)
        pc = [_nn(lt01, t) for t in lc3]
        sc = [_nn(ut01, t) for t in lc3]
        pr = [_nn(t, ut01) for t in lr3]
        sr = [_nn(t, lt01) for t in lr3]
        a_c = jnp.where(col < H, pc[0] + pc[1] + pc[2], sc[0] + sc[1] + sc[2])
        a_r = jnp.where(sub16 < H, pr[0] + pr[1] + pr[2], sr[0] + sr[1] + sr[2])
        tot = jnp.where(col[0:1, :] < H, a_c[CHUNK - 1:CHUNK, :], a_c[0:1, :])
        tot_scr[c] = jnp.broadcast_to(tot, (8, 128))
        ks = dt_c * jnp.exp(tot - a_c)
        es = jnp.exp(a_c)
        es_scr[0, pl.ds(r0, CHUNK), :] = per_head_lanes(es, 0)
        es_scr[1, pl.ds(r0, CHUNK), :] = per_head_lanes(es, H)
        xwf = (xs * per_head_lanes(ks, 0)).astype(bf16)
        xwb = (xs * per_head_lanes(ks, H)).astype(bf16)
        csf = [_tn(xwf[:, g * HG * P:(g + 1) * HG * P], bmb[g]) for g in range(SSM_GROUPS)]
        csb = [_tn(xwb[:, g * HG * P:(g + 1) * HG * P], bmb[g]) for g in range(SSM_GROUPS)]
        for g in range(SSM_GROUPS):
            cs_scr[0, c, g] = csf[g]
            cs_scr[1, c, g] = csb[g]

        ms = []
        for h in range(H):
            df = jnp.exp(jnp.minimum(a_c[:, h:h + 1] - a_r[h:h + 1, :], 0.0)) * dt_r[h:h + 1, :]
            db = jnp.exp(jnp.minimum(a_c[:, H + h:H + h + 1] - a_r[H + h:H + h + 1, :], 0.0)) \
                * dt_r[H + h:H + h + 1, :]
            dm = jnp.where(col <= row, df, 0.0) + jnp.where(col >= row, db, 0.0)
            ms.append((gmat[h // HG] * dm).astype(bf16))
        ys = []
        for c2 in range(H // 2):
            xcol = xs[:, c2 * 2 * P:(c2 + 1) * 2 * P]
            x_lo = jnp.where(lane_lo, xcol, 0.0).astype(bf16)
            x_hi = jnp.where(lane_lo, 0.0, xcol).astype(bf16)
            ys.append(_nn(ms[2 * c2], x_lo) + _nn(ms[2 * c2 + 1], x_hi))
        y = jnp.concatenate(ys, axis=1)
        y_scr[pl.ds(r0, CHUNK), :] = y + (dsk_ref[0:1, :] + dsk_ref[1:2, :]) * xs
        return carry

    lax.fori_loop(0, nc, intra, 0, unroll=min(nc, 4))

    def inter(d, c):
        r0 = pl.multiple_of(c * CHUNK, CHUNK)
        st = [st_scr[d, g] for g in range(SSM_GROUPS)]
        yi = [_nt(grp(xc_scr[pl.ds(r0, CHUNK), :], g, GW + SSM_GROUPS * N).astype(bf16),
                  st[g].astype(bf16)) for g in range(SSM_GROUPS)]
        tot = tot_scr[c]
        for g in range(SSM_GROUPS):
            dec = jnp.concatenate(
                [jnp.broadcast_to(jnp.exp(tot[0:1, d * H + g * HG + hh:d * H + g * HG + hh + 1]), (P, N))
                 for hh in range(HG)], axis=0)
            st_scr[d, g] = st[g] * dec + cs_scr[d, c, g]
        return r0, jnp.concatenate(yi, axis=1) * es_scr[d, pl.ds(r0, CHUNK), :]

    def fwd(c, carry):
        r0, yi = inter(0, c)
        y_scr[pl.ds(r0, CHUNK), :] += yi
        return carry

    lax.fori_loop(0, nc, fwd, 0, unroll=2)

    def bwd(i, carry):
        r0, yi = inter(1, nc - 1 - i)
        yz = (y_scr[pl.ds(r0, CHUNK), :] + yi) * _silu(z_ref[pl.ds(r0, CHUNK), :])
        ms = jnp.mean(yz * yz, axis=-1, keepdims=True)
        o_ref[pl.ds(r0, CHUNK), :] = (yz * lax.rsqrt(ms + EPS) * ng_ref[...]).astype(o_ref.dtype)
        return carry

    lax.fori_loop(0, nc, bwd, 0, unroll=2)

    if want_state:
        for l in range(n_prev):
            hout_ref[l] = prev_ref[l]
        for d in range(2):
            for h in range(H):
                g, hh = divmod(h, HG)
                hout_ref[n_prev, d, h] = st_scr[d, g, hh * P:(hh + 1) * P, :]


def _pad_lanes(v, n=128):
    return jnp.pad(v, ((0, 0), (0, n - v.shape[1])))


def _ssm(u, dt_t, p, h0, layer, nb, L, want_state, prev=None):
    has_h0 = h0 is not None
    n_prev = 0 if prev is None else prev.shape[1]
    sshape = (2, SSM_HEADS, SSM_P, SSM_N)
    H = SSM_HEADS
    a_log = p['ssm_a_log'].reshape(1, 2 * H)
    dt_bias = p['ssm_dt_bias'].reshape(1, 2 * H)
    small = lambda shape: pl.BlockSpec(shape, lambda b: (0, 0))
    in_specs = [pl.BlockSpec((L, 2 * GW), lambda b: (b, C_XBC // (2 * GW))),
                pl.BlockSpec((L, GW), lambda b: (b, C_Z // GW)),
                pl.BlockSpec((L, 128), lambda b: (b, C_DT // 128)),
                pl.BlockSpec((2 * H, L), lambda b: (0, b)),
                small((SSM_K, 2 * GW)), small((1, 2 * GW)),
                small((1, 128)), small((2 * H, 128)), small((1, 128)), small((2 * H, 128)),
                small((2, GW)), small((1, GW))]
    args = [u, u, u, dt_t, p['ssm_conv_w'], p['ssm_conv_b'].reshape(1, 2 * GW),
            _pad_lanes(a_log), jnp.broadcast_to(a_log.reshape(2 * H, 1), (2 * H, 128)),
            _pad_lanes(dt_bias), jnp.broadcast_to(dt_bias.reshape(2 * H, 1), (2 * H, 128)),
            jnp.repeat(p['ssm_d'], SSM_P, axis=1), p['ssm_norm'].reshape(1, GW)]
    if has_h0:
        in_specs.append(pl.BlockSpec((None, None, 2, H, SSM_P, SSM_N), lambda b: (b, layer, 0, 0, 0, 0)))
        args.append(h0)
    out_specs = [pl.BlockSpec((L, GW), lambda b: (b, 0))]
    out_shape = [jax.ShapeDtypeStruct((nb * L, GW), bf16)]
    if n_prev:
        in_specs.append(pl.BlockSpec((None, n_prev) + sshape, lambda b: (b, 0, 0, 0, 0, 0)))
        args.append(prev)
    if want_state:
        out_specs.append(pl.BlockSpec((None, n_prev + 1) + sshape, lambda b: (b, 0, 0, 0, 0, 0)))
        out_shape.append(jax.ShapeDtypeStruct((nb, n_prev + 1) + sshape, f32))
    res = pl.pallas_call(
        functools.partial(_ssm_kernel, L=L, has_h0=has_h0, want_state=want_state, n_prev=n_prev),
        grid=(nb,),
        in_specs=in_specs,
        out_specs=out_specs,
        out_shape=out_shape,
        scratch_shapes=[pltpu.VMEM((L + 16, 2 * GW), f32),
                        pltpu.VMEM((L, 2 * GW), f32),
                        pltpu.VMEM((L, GW), f32),
                        pltpu.VMEM((2, SSM_GROUPS, H // SSM_GROUPS * SSM_P, SSM_N), f32),
                        pltpu.VMEM((2, L, GW), f32),
                        pltpu.VMEM((2, L // CHUNK, SSM_GROUPS, H // SSM_GROUPS * SSM_P, SSM_N), f32),
                        pltpu.VMEM((L // CHUNK, 8, 128), f32)],
        compiler_params=_cparams(("arbitrary",)),
        name="ssd_mixer",
    )(*args)
    return res if want_state else (res[0], None)


def _ctx_att_kernel(sink_ref, q_ref, k_ref, v_ref, o_ref, *, L):
    G = ATT_HEADS // ATT_KV
    scale = ATT_D ** -0.5
    kb = [k_ref[:, j * ATT_D:(j + 1) * ATT_D].astype(bf16) for j in range(ATT_KV)]
    vb = [v_ref[:, j * ATT_D:(j + 1) * ATT_D].astype(bf16) for j in range(ATT_KV)]
    s = [_nt((q_ref[:, h * ATT_D:(h + 1) * ATT_D] * scale).astype(bf16), kb[h // G])
         for h in range(ATT_HEADS)]
    p, den = [], []
    for h in range(ATT_HEADS):
        sink = sink_ref[h]
        m = jnp.maximum(jnp.max(s[h], axis=-1, keepdims=True), sink)
        e = jnp.exp(s[h] - m)
        den.append(jnp.sum(e, axis=-1, keepdims=True) + jnp.exp(sink - m))
        p.append(e.astype(bf16))
    o = [_nn(p[h], vb[h // G]) for h in range(ATT_HEADS)]
    for h in range(ATT_HEADS):
        o_ref[:, h * ATT_D:(h + 1) * ATT_D] = (o[h] / den[h]).astype(o_ref.dtype)


def _ctx_attention(u, sink, nb, L):
    return pl.pallas_call(
        functools.partial(_ctx_att_kernel, L=L),
        grid=(nb,),
        in_specs=[pl.BlockSpec(memory_space=pltpu.SMEM),
                  pl.BlockSpec((L, GW), lambda b: (b, C_AQ // GW)),
                  pl.BlockSpec((L, 128), lambda b: (b, C_AK // 128)),
                  pl.BlockSpec((L, 128), lambda b: (b, C_AV // 128))],
        out_specs=pl.BlockSpec((L, GW), lambda b: (b, 0)),
        out_shape=jax.ShapeDtypeStruct((nb * L, GW), bf16),
        compiler_params=_cparams(("arbitrary",)),
        name="ctx_attention",
    )(sink, u, u, u)


def _rope(x, cos, sin):
    w = x.shape[1]
    lane = lax.broadcasted_iota(jnp.int32, x.shape, 1)
    first = (lane % 32) < 16
    rot = jnp.where(first, -pltpu.roll(x, w - 16, 1), pltpu.roll(x, 16, 1))
    return x * cos + rot * sin


def _half_variants(x, keep_fill):
    lane = lax.broadcasted_iota(jnp.int32, x.shape, 1)
    lo = lane < ATT_D
    a0 = jnp.where(lo, x, keep_fill)
    b1 = jnp.where(lo, keep_fill, x)
    xr = pltpu.roll(x, ATT_D, 1)
    a1 = jnp.where(lo, xr, keep_fill)
    b0 = jnp.where(lo, keep_fill, xr)
    return ((a0, b0), (a1, b1))


def _half_variants_t(xt):
    sub = lax.broadcasted_iota(jnp.int32, xt.shape, 0)
    lo = sub < ATT_D
    a0 = jnp.where(lo, xt, 0.0)
    b1 = jnp.where(lo, 0.0, xt)
    xr = pltpu.roll(xt, ATT_D, 0)
    a1 = jnp.where(lo, xr, 0.0)
    b0 = jnp.where(lo, 0.0, xr)
    return ((a0, b0), (a1, b1))


def _lat_att_kernel(sink_ref, q_ref, k_ref, v_ref, kc_ref, vc_ref, cos_ref, sin_ref, o_ref,
                    qr_scr, kt_scr, vv_scr, kct_scr, vcv_scr, bias_scr, *, L):
    G = ATT_HEADS // ATT_KV
    B = CHUNK
    nb = L // B
    scale = ATT_D ** -0.5
    zeros = jnp.zeros((B, 128), bf16)
    for j in range(ATT_KV):
        for hf in range(2):
            kt_scr[j, hf, 0] = zeros
            kt_scr[j, hf, nb + 1] = zeros
            vv_scr[j, hf, 0:B, :] = zeros
            vv_scr[j, hf, B + L:2 * B + L, :] = zeros
    kct = _half_variants_t(kc_ref[...].T)
    vcv = _half_variants(vc_ref[...], 1.0)
    for j in range(ATT_KV):
        for hf in range(2):
            kct_scr[j, hf] = kct[j][hf].astype(bf16)
            vcv_scr[j, hf] = vcv[j][hf].astype(bf16)

    rowi = lax.broadcasted_iota(jnp.int32, (2 * B, 3 * B), 0) % B
    coli = lax.broadcasted_iota(jnp.int32, (2 * B, 3 * B), 1)
    inwin = jnp.abs(rowi - (coli - B)) <= ATT_WIN
    ninf = jnp.float32(-jnp.inf)
    bias_scr[0] = jnp.where(inwin & (coli >= B), 0.0, ninf)
    bias_scr[1] = jnp.where(inwin, 0.0, ninf)
    bias_scr[2] = jnp.where(inwin & (coli < 2 * B), 0.0, ninf)

    def prep(n, carry):
        r0 = pl.multiple_of(n * B, B)
        cos = cos_ref[pl.ds(r0, B), :]
        sin = sin_ref[pl.ds(r0, B), :]
        cos4 = jnp.concatenate([cos] * 4, axis=1)
        sin4 = jnp.concatenate([sin] * 4, axis=1)
        qr_scr[pl.ds(r0, B), :] = (_rope(q_ref[pl.ds(r0, B), :], cos4, sin4) * scale).astype(bf16)
        kvar = _half_variants_t(_rope(k_ref[pl.ds(r0, B), :], cos, sin).T)
        vvar = _half_variants(v_ref[pl.ds(r0, B), :], 1.0)
        for j in range(ATT_KV):
            for hf in range(2):
                kt_scr[j, hf, n + 1] = kvar[j][hf].astype(bf16)
                vv_scr[j, hf, pl.ds(B + r0, B), :] = vvar[j][hf].astype(bf16)
        return carry

    lax.fori_loop(0, nb, prep, 0)

    lane_lo = lax.broadcasted_iota(jnp.int32, (2 * B, 128), 1) < ATT_D

    combos = [(j, hf) for j in range(ATT_KV) for hf in range(2)]
    BLK_GROUP = 8

    def scores(n):
        r0 = pl.multiple_of(n * B, B)
        qs = [jnp.concatenate([qr_scr[pl.ds(r0, B), (2 * j) * 128:(2 * j + 1) * 128],
                               qr_scr[pl.ds(r0, B), (2 * j + 1) * 128:(2 * j + 2) * 128]], axis=0)
              for j in range(ATT_KV)]
        s_c = [_nn(qs[j], kct_scr[j, hf]) for j, hf in combos]
        s_b = [jnp.concatenate([_nn(qs[j], kt_scr[j, hf, n + t]) for t in range(3)], axis=1)
               for j, hf in combos]
        return s_c, s_b

    def finish(n, s_c, s_b):
        r0 = pl.multiple_of(n * B, B)
        bias = bias_scr[jnp.where(n == 0, 0, jnp.where(n == nb - 1, 2, 1))]
        p_c, p_b, esink = [], [], []
        for i, (j, hf) in enumerate(combos):
            sb = s_b[i] + bias
            sink = jnp.concatenate([jnp.full((B, 1), sink_ref[G * j + hf], f32),
                                    jnp.full((B, 1), sink_ref[G * j + 2 + hf], f32)], axis=0)
            m = jnp.maximum(jnp.maximum(jnp.max(s_c[i], axis=-1, keepdims=True),
                                        jnp.max(sb, axis=-1, keepdims=True)), sink)
            p_c.append(jnp.exp(s_c[i] - m).astype(bf16))
            p_b.append(jnp.exp(sb - m).astype(bf16))
            esink.append(jnp.exp(sink - m))
        oe = [_nn(p_c[i], vcv_scr[j, hf]) + _nn(p_b[i], vv_scr[j, hf, pl.ds(r0, 3 * B), :])
              for i, (j, hf) in enumerate(combos)]
        outs = []
        for i, (j, hf) in enumerate(combos):
            den = oe[i][:, (1 - hf) * ATT_D:(1 - hf) * ATT_D + 1] + esink[i]
            outs.append(oe[i] / den)
        for j in range(ATT_KV):
            o = jnp.where(lane_lo, outs[2 * j], outs[2 * j + 1])
            o_ref[pl.ds(r0, B), (2 * j) * 128:(2 * j + 1) * 128] = o[0:B].astype(o_ref.dtype)
            o_ref[pl.ds(r0, B), (2 * j + 1) * 128:(2 * j + 2) * 128] = o[B:2 * B].astype(o_ref.dtype)

    def blk_group(i, carry):
        n0 = i * BLK_GROUP
        s_next = scores(n0)
        for t in range(BLK_GROUP):
            s_cur = s_next
            if t + 1 < BLK_GROUP:
                s_next = scores(n0 + t + 1)
            finish(n0 + t, *s_cur)
        return carry

    lax.fori_loop(0, nb // BLK_GROUP, blk_group, 0)


def _rope_tables(L):
    pos = jnp.arange(L)
    rows = (pos // GRID_W).astype(f32)
    cols = (pos % GRID_W).astype(f32)
    half = ATT_D // 4
    freqs = ROPE_THETA ** (-jnp.arange(half, dtype=f32) / half)
    ang_r = rows[:, None] * freqs
    ang_c = cols[:, None] * freqs
    ang = jnp.concatenate([ang_r, ang_r, ang_c, ang_c], axis=1)
    ang = jnp.concatenate([ang, ang], axis=1)
    return jnp.cos(ang), jnp.sin(ang)


def _lat_attention(u, k_cache, v_cache, sink, layer, nb, L):
    Lc = k_cache.shape[2]
    cos, sin = _rope_tables(L)
    kc = k_cache.reshape(nb, DEPTH, Lc, ATT_KV * ATT_D)
    vc = v_cache.reshape(nb, DEPTH, Lc, ATT_KV * ATT_D)
    cspec = pl.BlockSpec((None, None, Lc, 128), lambda b: (b, layer, 0, 0))
    return pl.pallas_call(
        functools.partial(_lat_att_kernel, L=L),
        grid=(nb,),
        in_specs=[pl.BlockSpec(memory_space=pltpu.SMEM),
                  pl.BlockSpec((L, GW), lambda b: (b, C_AQ // GW)),
                  pl.BlockSpec((L, 128), lambda b: (b, C_AK // 128)),
                  pl.BlockSpec((L, 128), lambda b: (b, C_AV // 128)),
                  cspec, cspec,
                  pl.BlockSpec((L, 128), lambda b: (0, 0)),
                  pl.BlockSpec((L, 128), lambda b: (0, 0))],
        out_specs=pl.BlockSpec((L, GW), lambda b: (b, 0)),
        out_shape=jax.ShapeDtypeStruct((nb * L, GW), bf16),
        scratch_shapes=[pltpu.VMEM((L, GW), bf16),
                        pltpu.VMEM((ATT_KV, 2, L // CHUNK + 2, 128, CHUNK), bf16),
                        pltpu.VMEM((ATT_KV, 2, L + 2 * CHUNK, 128), bf16),
                        pltpu.VMEM((ATT_KV, 2, 128, Lc), bf16),
                        pltpu.VMEM((ATT_KV, 2, Lc, 128), bf16),
                        pltpu.VMEM((3, 2 * CHUNK, 3 * CHUNK), f32)],
        compiler_params=_cparams(("arbitrary",)),
        name="lat_attention",
    )(sink, u, u, u, kc, vc, cos, sin)


def _permute_w_in_kernel(wt_ref, o_ref):
    ndt = 2 * SSM_HEADS
    moves = [(0, C_CONV, 1024), (1536, C_XBC, 1024), (1024, C_Z, GW), (2560 + ndt, C_RQ, C_DT - C_RQ)]
    for src, dst, n in moves:
        for r in range(0, n, 128):
            o_ref[:, dst + r:dst + r + 128] = wt_ref[src + r:src + r + 128, :].T.astype(bf16)
    tail = wt_ref[2560:2560 + 128, :].T
    lane = lax.broadcasted_iota(jnp.int32, tail.shape, 1)
    o_ref[:, C_DT:UW] = jnp.where(lane < ndt, tail, 0.0).astype(bf16)


def _permute_w_in(w):
    tr = 256
    return pl.pallas_call(
        _permute_w_in_kernel,
        grid=(DEPTH, D // tr),
        in_specs=[pl.BlockSpec((None, N_IN, tr), lambda l, i: (l, 0, i))],
        out_specs=pl.BlockSpec((None, tr, UW), lambda l, i: (l, i, 0)),
        out_shape=jax.ShapeDtypeStruct((DEPTH, D, UW), bf16),
        compiler_params=_cparams(("arbitrary", "arbitrary")),
        name="permute_w_in",
    )(jnp.swapaxes(w, 1, 2))


def _mix_and_mlp(x, u, dt_t, mod, p, ctx, layer, nb, L, final, prev_ssm=None, prev_ret=None):
    per_seq = ctx is not None
    want_state = ctx is None
    o_conv = _conv_module(u, p['conv_w'], p['conv_b'], p['conv_ln_g'], p['conv_ln_b'], nb, L)
    if ctx is None:
        o_ssm, h_ssm = _ssm(u, dt_t, p, None, layer, nb, L, True, prev_ssm)
        o_ret, h_ret = _retention(u, p['ret_log_decay'], p['ret_gn_g'], None, layer, nb, L, True, prev_ret)
        o_att = _ctx_attention(u, p['att_sink'], nb, L)
    else:
        k_c, v_c, s_ssm, s_ret = ctx
        o_ssm, h_ssm = _ssm(u, dt_t, p, s_ssm, layer, nb, L, False)
        o_ret, h_ret = _retention(u, p['ret_log_decay'], p['ret_gn_g'], s_ret, layer, nb, L, False)
        o_att = _lat_attention(u, k_c, v_c, p['att_sink'], layer, nb, L)
    x1 = _outproj(o_conv, o_ssm, o_ret, o_att, p['w_out_b'], x, mod, L, per_seq)
    x2 = _mlp(x1, p['norm_mlp'], mod, p['final_norm'], p['w1_b'], p['w2_b'], L, per_seq, final)
    states = None
    if want_state:
        ak = u[:, C_AK:C_AK + 128].reshape(nb, L, ATT_KV, ATT_D)
        av = u[:, C_AV:C_AV + 128].reshape(nb, L, ATT_KV, ATT_D)
        states = (ak, av, h_ssm, h_ret)
    return x2, states


def kernel(x_prompt, x_sample, cache_attn_k, cache_attn_v, state_ssm, state_ret, c, c_ctx, ada_w, ada_b, norm_mix, norm_mlp, w_in, conv_w, conv_b, conv_ln_g, conv_ln_b, ssm_conv_w, ssm_conv_b, ssm_a_log, ssm_dt_bias, ssm_d, ssm_norm, ret_log_decay, ret_gn_g, att_sink, w_out, w1, w2, final_norm):
    nbp, Lp, _ = x_prompt.shape
    nbs, Ls, _ = x_sample.shape
    cvec = jnp.concatenate([c_ctx[None, :], c, jnp.zeros((16 - 1 - nbs, D), f32)], axis=0)
    mod = _ada_mod(cvec, ada_w, ada_b)
    y_p = x_prompt.reshape(nbp * Lp, D)
    y_s = x_sample.reshape(nbs * Ls, D)
    new_k, new_v = [], []
    hs, hr = None, None
    w_in_p = _permute_w_in(w_in)
    ctx = (cache_attn_k, cache_attn_v, state_ssm, state_ret)
    for l in range(DEPTH):
        mod_l = mod[l].reshape(16, 6, 1, D)
        mod_p, mod_s = mod_l[0:1], mod_l[1:1 + nbs]
        u_p, dtt_p, (w_out_b,) = _inproj(y_p, norm_mix[l], mod_p, w_in_p, l, Lp, False, [w_out])
        u_s, dtt_s, (w1_b, w2_b) = _inproj(y_s, norm_mix[l], mod_s, w_in_p, l, Ls, True, [w1, w2])
        p = dict(norm_mlp=norm_mlp[l],
                 conv_w=conv_w[l], conv_b=conv_b[l], conv_ln_g=conv_ln_g[l], conv_ln_b=conv_ln_b[l],
                 ssm_conv_w=ssm_conv_w[l], ssm_conv_b=ssm_conv_b[l], ssm_a_log=ssm_a_log[l],
                 ssm_dt_bias=ssm_dt_bias[l], ssm_d=ssm_d[l], ssm_norm=ssm_norm[l],
                 ret_log_decay=ret_log_decay[l], ret_gn_g=ret_gn_g[l], att_sink=att_sink[l],
                 w_out_b=w_out_b, w1_b=w1_b, w2_b=w2_b, final_norm=final_norm)
        final = l == DEPTH - 1
        y_p, (k_l, v_l, hs, hr) = _mix_and_mlp(y_p, u_p, dtt_p, mod_p, p, None, l, nbp, Lp, final, hs, hr)
        new_k.append(k_l)
        new_v.append(v_l)
        y_s, _ = _mix_and_mlp(y_s, u_s, dtt_s, mod_s, p, ctx, l, nbs, Ls, final)
    return (y_p.reshape(nbp, Lp, D), y_s.reshape(nbs, Ls, D),
            jnp.stack(new_k, axis=1), jnp.stack(new_v, axis=1), hs, hr)
```
